```python
import math
import jax, jax.numpy as jnp
from jax import lax
import numpy as np

D_MODEL = 1024
BATCH = 16
SEQ = 256
DEPTH = 1
DEC_BATCH = 4
DEC_SEQ = 4096
PAST_LEN = 512

GRID_W = 64
D_ATT = D_MODEL
HD_QK = 64
HD_V = 2 * HD_QK
H_ATT = D_ATT // HD_V
D_SSD = D_MODEL
SSD_HEADDIM = 64
H_SSD = D_SSD // SSD_HEADDIM
SSD_GROUPS = 4
HEADS_PER_GROUP = H_SSD // SSD_GROUPS
D_STATE = 64
CONV_K = 3
CONV_DIM = D_SSD + 2 * SSD_GROUPS * D_STATE
D_INNER = D_ATT + D_SSD
D_PROJ = 4 * D_ATT + D_SSD + CONV_DIM + 2 * H_SSD
CHUNK = 128
Q_BLOCK = 128
ROPE_BASE = 10000.0
ROPE_FREQS = HD_QK // 4
EPS = 1e-6

kernel_name = "hymba_diffattn_ssd_prefix_diffusion_step"

F32 = jnp.float32


def rms_norm(x, g):
    xf = x.astype(F32)
    y = xf * lax.rsqrt(jnp.mean(xf * xf, axis=-1, keepdims=True) + EPS)
    return (y * g.astype(F32)).astype(x.dtype)


def adaln(cond, w_mod, b_mod):
    m = jax.nn.silu(cond) @ w_mod + b_mod
    shift, scale, gate = jnp.split(m, 3, axis=-1)
    return shift[:, None, :], scale[:, None, :], gate[:, None, :]


def split_proj(u):
    idx = np.cumsum([D_ATT, D_ATT, D_ATT, D_ATT, D_SSD, CONV_DIM]).tolist()
    return jnp.split(u, idx, axis=-1)


def axial_rope_tables(L):
    rows = L // GRID_W
    row_ids = jnp.repeat(jnp.arange(rows), GRID_W).astype(F32)
    col_ids = jnp.tile(jnp.arange(GRID_W), rows).astype(F32)
    inv = ROPE_BASE ** (-jnp.arange(ROPE_FREQS, dtype=F32) / ROPE_FREQS)
    ang_r = row_ids[:, None] * inv
    ang_c = col_ids[:, None] * inv
    return jnp.cos(ang_r), jnp.sin(ang_r), jnp.cos(ang_c), jnp.sin(ang_c)


def rotate(x, cos, sin):
    x1, x2 = jnp.split(x, 2, axis=-1)
    return jnp.concatenate([x1 * cos - x2 * sin, x2 * cos + x1 * sin], axis=-1)


def apply_rope(x, tables):
    cr, sr, cc, sc = [t.astype(x.dtype)[:, None, None, :] for t in tables]
    xr, xc = jnp.split(x, 2, axis=-1)
    return jnp.concatenate([rotate(xr, cr, sr), rotate(xc, cc, sc)], axis=-1)


def diff_attention(q, k, v, lam):
    b, Lq = q.shape[0], q.shape[1]
    nb = Lq // Q_BLOCK
    qb = q.reshape(b, nb, Q_BLOCK, H_ATT, 2, HD_QK).transpose(1, 0, 2, 3, 4, 5)
    scale = 1.0 / math.sqrt(HD_QK)

    def block(qi):
        s = jnp.einsum('bqhcd,bkhcd->bchqk', qi, k).astype(F32) * scale
        p = jax.nn.softmax(s, axis=-1)
        a = p[:, 0] - lam * p[:, 1]
        return jnp.einsum('bhqk,bkhd->bqhd', a.astype(v.dtype), v)

    o = lax.map(block, qb)
    return o.transpose(1, 0, 2, 3, 4).reshape(b, Lq, H_ATT, HD_V)


def centred_conv(u, w, bias):
    L = u.shape[1]
    pad = CONV_K // 2
    up = jnp.pad(u, ((0, 0), (pad, pad), (0, 0)))
    out = bias
    for j in range(CONV_K):
        out = out + up[:, j:j + L] * w[j]
    return out


def ssd_scan(x, dt, A, B, C, h0):
    b, L, H, P = x.shape
    N = B.shape[-1]
    nc = L // CHUNK
    xc = x.astype(F32).reshape(b, nc, CHUNK, H, P)
    dtc = dt.reshape(b, nc, CHUNK, H)
    Bc = B.astype(F32).reshape(b, nc, CHUNK, H, N)
    Cc = C.astype(F32).reshape(b, nc, CHUNK, H, N)
    a_cum = jnp.cumsum(dtc * A, axis=2)
    seg = a_cum[:, :, :, None, :] - a_cum[:, :, None, :, :]
    mask = jnp.tril(jnp.ones((CHUNK, CHUNK), dtype=bool))[None, None, :, :, None]
    Lmat = jnp.exp(jnp.where(mask, seg, -jnp.inf))
    xdt = xc * dtc[..., None]
    cb = jnp.einsum('bcqhn,bcshn->bcqsh', Cc, Bc)
    y_diag = jnp.einsum('bcqsh,bcshp->bcqhp', cb * Lmat, xdt)
    decay_to_end = jnp.exp(a_cum[:, :, -1:, :] - a_cum)
    states = jnp.einsum('bcqhn,bcqh,bcqhp->bchpn', Bc, decay_to_end, xdt)
    chunk_decay = jnp.exp(a_cum[:, :, -1, :])

    def step(h, inp):
        st, dec = inp
        return h * dec[:, :, None, None] + st, h

    h_final, h_prev = lax.scan(step, h0.astype(F32),
                               (states.transpose(1, 0, 2, 3, 4), chunk_decay.transpose(1, 0, 2)))
    h_prev = h_prev.transpose(1, 0, 2, 3, 4)
    y_off = jnp.einsum('bcqhn,bchpn,bcqh->bcqhp', Cc, h_prev, jnp.exp(a_cum))
    return (y_diag + y_off).reshape(b, L, H, P), h_final


def ssd_branch(xbc_raw, dt_raw, z, conv_w, conv_b, dt_bias, A_log, D_skip, norm_g, h0):
    b, L, _ = xbc_raw.shape
    xbc = jax.nn.silu(centred_conv(xbc_raw, conv_w, conv_b))
    xs, Bm, Cm = jnp.split(xbc, [D_SSD, D_SSD + SSD_GROUPS * D_STATE], axis=-1)
    xs = xs.reshape(b, L, H_SSD, SSD_HEADDIM)
    Bm = jnp.repeat(Bm.reshape(b, L, SSD_GROUPS, D_STATE), HEADS_PER_GROUP, axis=2)
    Cm = jnp.repeat(Cm.reshape(b, L, SSD_GROUPS, D_STATE), HEADS_PER_GROUP, axis=2)
    dt = jax.nn.softplus(dt_raw.reshape(b, L, 2, H_SSD).astype(F32) + dt_bias.astype(F32))
    A = -jnp.exp(A_log.astype(F32))
    y_f, h_f = ssd_scan(xs, dt[:, :, 0], A[0], Bm, Cm, h0[:, 0])
    y_b, h_b = ssd_scan(jnp.flip(xs, 1), jnp.flip(dt[:, :, 1], 1), A[1],
                        jnp.flip(Bm, 1), jnp.flip(Cm, 1), h0[:, 1])
    y = y_f + jnp.flip(y_b, 1) + D_skip.astype(F32)[:, None] * xs.astype(F32)
    y = y.reshape(b, L, D_SSD).astype(z.dtype) * jax.nn.silu(z)
    return rms_norm(y, norm_g), jnp.stack([h_f, h_b], axis=1)


def layer(x, cond, w_mod, b_mod, norm_g, w_in, lq1, lk1, lq2, lk2, subln_g,
          conv_w, conv_b, dt_bias, A_log, D_skip, ssd_norm_g, w_out, lam_init,
          rope, k_prefix, v_prefix, h0):
    b, L, _ = x.shape
    shift, scale, gate = adaln(cond, w_mod, b_mod)
    h = rms_norm(x, norm_g) * (1.0 + scale) + shift
    q, k, v, g_att, z, xbc, dt_raw = split_proj(h @ w_in)
    q = q.reshape(b, L, H_ATT, 2, HD_QK)
    k = k.reshape(b, L, H_ATT, 2, HD_QK)
    v = v.reshape(b, L, H_ATT, HD_V)
    if rope is not None:
        q = apply_rope(q, rope)
        k = apply_rope(k, rope)
    k_all, v_all = k, v
    if k_prefix is not None:
        k_all = jnp.concatenate([k, k_prefix], axis=1)
        v_all = jnp.concatenate([v, v_prefix], axis=1)
    lam = (jnp.exp(jnp.sum(lq1.astype(F32) * lk1.astype(F32)))
           - jnp.exp(jnp.sum(lq2.astype(F32) * lk2.astype(F32))) + lam_init)
    att = diff_attention(q, k_all, v_all, lam)
    att = rms_norm(att, subln_g) * (1.0 - lam_init)
    att = att.reshape(b, L, D_ATT) * jax.nn.silu(g_att)
    if h0 is None:
        h0 = jnp.zeros((b, 2, H_SSD, SSD_HEADDIM, D_STATE), F32)
    ssd_y, h_fin = ssd_branch(xbc, dt_raw, z, conv_w, conv_b, dt_bias, A_log, D_skip,
                              ssd_norm_g, h0.astype(F32))
    out = jnp.concatenate([att, ssd_y], axis=-1) @ w_out
    return x + gate * out, k.reshape(b, L, H_ATT, 2 * HD_QK), v, h_fin


def setup_inputs(seed: int = 0) -> dict:
    key = jax.random.key(seed)
    ks = jax.random.split(key, 26)
    n = jax.random.normal
    dt0 = jnp.exp(jax.random.uniform(ks[18], (DEPTH, 2, H_SSD), minval=math.log(1e-3), maxval=math.log(1e-1)))
    return {
        "x_prompt": n(ks[0], (BATCH, SEQ, D_MODEL), F32),
        "x_sample": n(ks[1], (DEC_BATCH, DEC_SEQ, D_MODEL), F32),
        "cache_k": n(ks[2], (DEC_BATCH, DEPTH, PAST_LEN, H_ATT, 2 * HD_QK), F32),
        "cache_v": n(ks[3], (DEC_BATCH, DEPTH, PAST_LEN, H_ATT, HD_V), F32),
        "state_ssd": 0.5 * n(ks[4], (DEC_BATCH, DEPTH, 2, H_SSD, SSD_HEADDIM, D_STATE), F32),
        "c": n(ks[5], (DEC_BATCH, D_MODEL), F32),
        "c_ctx": n(ks[6], (D_MODEL,), F32),
        "w_mod": n(ks[7], (DEPTH, D_MODEL, 3 * D_MODEL), F32) * D_MODEL ** -0.5,
        "b_mod": 0.01 * n(ks[8], (DEPTH, 3 * D_MODEL), F32),
        "norm_g": 1.0 + 0.02 * n(ks[9], (DEPTH, D_MODEL), F32),
        "w_in": n(ks[10], (DEPTH, D_MODEL, D_PROJ), F32) * D_MODEL ** -0.5,
        "lambda_q1": 0.1 * n(ks[11], (DEPTH, HD_QK), F32),
        "lambda_k1": 0.1 * n(ks[12], (DEPTH, HD_QK), F32),
        "lambda_q2": 0.1 * n(ks[13], (DEPTH, HD_QK), F32),
        "lambda_k2": 0.1 * n(ks[14], (DEPTH, HD_QK), F32),
        "subln_g": 1.0 + 0.02 * n(ks[15], (DEPTH, HD_V), F32),
        "conv_w": n(ks[16], (DEPTH, CONV_K, CONV_DIM), F32) * CONV_K ** -0.5,
        "conv_b": 0.01 * n(ks[17], (DEPTH, CONV_DIM), F32),
        "dt_bias": dt0 + jnp.log(-jnp.expm1(-dt0)),
        "A_log": jnp.log(jax.random.uniform(ks[19], (DEPTH, 2, H_SSD), minval=1.0, maxval=16.0)),
        "D_skip": 1.0 + 0.02 * n(ks[20], (DEPTH, H_SSD), F32),
        "ssd_norm_g": 1.0 + 0.02 * n(ks[21], (DEPTH, D_SSD), F32),
        "w_out": n(ks[22], (DEPTH, D_INNER, D_MODEL), F32) * D_INNER ** -0.5,
        "final_g": 1.0 + 0.02 * n(ks[23], (D_MODEL,), F32),
    }


def reference(x_prompt, x_sample, cache_k, cache_v, state_ssd, c, c_ctx, w_mod, b_mod,
              norm_g, w_in, lambda_q1, lambda_k1, lambda_q2, lambda_k2, subln_g, conv_w,
              conv_b, dt_bias, A_log, D_skip, ssd_norm_g, w_out, final_g):
    b_dec, l_dec = x_sample.shape[0], x_sample.shape[1]
    l_past = cache_k.shape[2]
    rope = axial_rope_tables(l_dec)
    xp, xs = x_prompt, x_sample
    ks_new, vs_new, hs_new = [], [], []
    for l in range(DEPTH):
        lam_init = 0.8 - 0.6 * math.exp(-0.3 * l)
        p = (w_mod[l], b_mod[l], norm_g[l], w_in[l], lambda_q1[l], lambda_k1[l],
             lambda_q2[l], lambda_k2[l], subln_g[l], conv_w[l], conv_b[l], dt_bias[l],
             A_log[l], D_skip[l], ssd_norm_g[l], w_out[l], lam_init)
        xp, k_ctx, v_ctx, h_ctx = layer(xp, c_ctx[None, :], *p, None, None, None, None)
        ks_new.append(k_ctx)
        vs_new.append(v_ctx)
        hs_new.append(h_ctx)
        kp = cache_k[:, l].reshape(b_dec, l_past, H_ATT, 2, HD_QK)
        vp = cache_v[:, l]
        xs, _, _, _ = layer(xs, c, *p, rope, kp, vp, state_ssd[:, l])
    y_prompt = rms_norm(xp, final_g)
    y_sample = rms_norm(xs, final_g)
    new_cache_k = jnp.stack(ks_new, axis=1)
    new_cache_v = jnp.stack(vs_new, axis=1)
    new_state_ssd = jnp.stack(hs_new, axis=1)
    return (y_prompt, y_sample, new_cache_k, new_cache_v, new_state_ssd)
```

```python
import functools
import math

import jax
import jax.numpy as jnp
from jax import lax
from jax.experimental import pallas as pl
from jax.experimental.pallas import tpu as pltpu

F32 = jnp.float32
BF16 = jnp.bfloat16

D_MODEL = 1024
GRID_W = 64
HD_QK = 64
HD_V = 128
H_ATT = 8
D_ATT = 1024
D_SSD = 1024
SSD_HEADDIM = 64
H_SSD = 16
SSD_GROUPS = 4
HEADS_PER_GROUP = 4
D_STATE = 64
CONV_DIM = D_SSD + 2 * SSD_GROUPS * D_STATE
CHUNK = 128
ROPE_BASE = 10000.0
ROPE_FREQS = 16
EPS = 1e-6

LANES = 128
DT_PAD = LANES
GROUP_W = HEADS_PER_GROUP * SSD_HEADDIM
VMEM_LIMIT = 56 * 1024 * 1024


def _cparams(sem):
    return pltpu.CompilerParams(dimension_semantics=sem, vmem_limit_bytes=VMEM_LIMIT)


def _silu(x):
    return x * jax.nn.sigmoid(x)


def _softplus(x):
    return jnp.maximum(x, 0.0) + jnp.log1p(jnp.exp(-jnp.abs(x)))


def _adaln_kernel(cond_ref, w_ref, b_ref, o_ref):
    cond = cond_ref[...]
    s = _silu(cond).astype(BF16)
    o_ref[...] = jnp.dot(s, w_ref[...].astype(BF16), preferred_element_type=F32) + b_ref[...]


def _adaln(cond8, w_mod, b_mod):
    n = w_mod.shape[1]
    blk = 1024
    return pl.pallas_call(
        _adaln_kernel,
        grid=(n // blk,),
        in_specs=[pl.BlockSpec((8, D_MODEL), lambda j: (0, 0)),
                  pl.BlockSpec((D_MODEL, blk), lambda j: (0, j)),
                  pl.BlockSpec((1, blk), lambda j: (0, j))],
        out_specs=pl.BlockSpec((8, blk), lambda j: (0, j)),
        out_shape=jax.ShapeDtypeStruct((8, n), F32),
        compiler_params=_cparams(("arbitrary",)),
        name="adaln",
    )(cond8, w_mod, b_mod.reshape(1, n))


def _mod_norm(x, shift, scale, g):
    ms = jnp.mean(x * x, axis=-1, keepdims=True)
    y = x * lax.rsqrt(ms + EPS) * g
    return y * (1.0 + scale) + shift


def _rope(t, cos, sin_signed, first):
    outs = []
    for h in range(H_ATT):
        th = t[:, h * LANES:(h + 1) * LANES]
        partner = jnp.where(first, pltpu.roll(th, LANES - 16, 1), pltpu.roll(th, 16, 1))
        outs.append(th * cos + partner * sin_signed)
    return jnp.concatenate(outs, axis=1)


def _qkv_kernel(*refs, rope, emit_f32):
    x_ref, shift_ref, scale_ref, ng_ref, w_ref = refs[:5]
    pos = 5
    if rope:
        cos_ref, sin_ref = refs[pos:pos + 2]
        pos += 2
    q_ref, kt_ref, v_ref = refs[pos:pos + 3]
    pos += 3
    if emit_f32:
        k32_ref, v32_ref = refs[pos:pos + 2]

    hb = _mod_norm(x_ref[...], shift_ref[...], scale_ref[...], ng_ref[...]).astype(BF16)
    q = jnp.dot(hb, w_ref[:, 0:D_ATT], preferred_element_type=F32)
    k = jnp.dot(hb, w_ref[:, D_ATT:2 * D_ATT], preferred_element_type=F32)
    v = jnp.dot(hb, w_ref[:, 2 * D_ATT:3 * D_ATT], preferred_element_type=F32)
    if rope:
        cos = cos_ref[...]
        sin_signed = sin_ref[...]
        lane = lax.broadcasted_iota(jnp.int32, cos.shape, 1)
        first = (lane % 32) < 16
        q = _rope(q, cos, sin_signed, first)
        k = _rope(k, cos, sin_signed, first)
    q_ref[...] = (q * (1.0 / math.sqrt(HD_QK))).astype(BF16)
    kt_ref[...] = k.T.astype(BF16)
    v_ref[...] = v.astype(BF16)
    if emit_f32:
        k32_ref[...] = k
        v32_ref[...] = v


def _qkv_proj(x, mod, mod_row, norm_g, w_qkv, rope_tabs, emit_f32, tm):
    B, L, _ = x.shape
    grid = (B, L // tm)
    row = mod_row

    in_specs = [
        pl.BlockSpec((None, tm, D_MODEL), lambda b, i: (b, i, 0)),
        pl.BlockSpec((None, 1, D_MODEL), lambda b, i: (row(b), 0, 0)),
        pl.BlockSpec((None, 1, D_MODEL), lambda b, i: (row(b), 0, 1)),
        pl.BlockSpec((1, D_MODEL), lambda b, i: (0, 0)),
        pl.BlockSpec((D_MODEL, 3 * D_ATT), lambda b, i: (0, 0), pipeline_mode=pl.Buffered(1)),
    ]
    args = [x, mod, mod, norm_g, w_qkv]
    if rope_tabs is not None:
        in_specs += [pl.BlockSpec((tm, LANES), lambda b, i: (i, 0)),
                     pl.BlockSpec((tm, LANES), lambda b, i: (i, 0))]
        args += list(rope_tabs)
    out_specs = [
        pl.BlockSpec((None, tm, D_ATT), lambda b, i: (b, i, 0)),
        pl.BlockSpec((None, D_ATT, tm), lambda b, i: (b, 0, i)),
        pl.BlockSpec((None, tm, D_ATT), lambda b, i: (b, i, 0)),
    ]
    out_shape = [
        jax.ShapeDtypeStruct((B, L, D_ATT), BF16),
        jax.ShapeDtypeStruct((B, D_ATT, L), BF16),
        jax.ShapeDtypeStruct((B, L, D_ATT), BF16),
    ]
    if emit_f32:
        out_specs += [pl.BlockSpec((None, tm, D_ATT), lambda b, i: (b, i, 0))] * 2
        out_shape += [jax.ShapeDtypeStruct((B, L, D_ATT), F32)] * 2
    return pl.pallas_call(
        functools.partial(_qkv_kernel, rope=rope_tabs is not None, emit_f32=emit_f32),
        grid=grid, in_specs=in_specs, out_specs=out_specs, out_shape=out_shape,
        compiler_params=_cparams(("parallel", "parallel")),
        name="qkv_proj",
    )(*args)


N_GZX = 2 * D_MODEL + CONV_DIM + DT_PAD


def _gzx_kernel(x_ref, shift_ref, scale_ref, ng_ref, w_ref, g_ref, z_ref, xbc_ref, dt_ref):
    hb = _mod_norm(x_ref[...], shift_ref[...], scale_ref[...], ng_ref[...]).astype(BF16)
    g_ref[...] = jnp.dot(hb, w_ref[:, 0:D_ATT], preferred_element_type=F32)
    z_ref[...] = jnp.dot(hb, w_ref[:, D_ATT:D_ATT + D_SSD], preferred_element_type=F32)
    o = D_ATT + D_SSD
    xbc_ref[...] = jnp.dot(hb, w_ref[:, o:o + CONV_DIM], preferred_element_type=F32)
    dt_ref[...] = jnp.dot(hb, w_ref[:, o + CONV_DIM:o + CONV_DIM + DT_PAD], preferred_element_type=F32)


def _gzx_proj(x, mod, mod_row, norm_g, w_gzx, tm):
    B, L, _ = x.shape
    row = mod_row
    in_specs = [
        pl.BlockSpec((None, tm, D_MODEL), lambda b, i: (b, i, 0)),
        pl.BlockSpec((None, 1, D_MODEL), lambda b, i: (row(b), 0, 0)),
        pl.BlockSpec((None, 1, D_MODEL), lambda b, i: (row(b), 0, 1)),
        pl.BlockSpec((1, D_MODEL), lambda b, i: (0, 0)),
        pl.BlockSpec((D_MODEL, N_GZX), lambda b, i: (0, 0), pipeline_mode=pl.Buffered(1)),
    ]
    widths = (D_ATT, D_SSD, CONV_DIM, DT_PAD)
    out_specs = [pl.BlockSpec((None, tm, w), lambda b, i: (b, i, 0)) for w in widths]
    out_shape = [jax.ShapeDtypeStruct((B, L, w), F32) for w in widths]
    return pl.pallas_call(
        _gzx_kernel, grid=(B, L // tm), in_specs=in_specs, out_specs=out_specs, out_shape=out_shape,
        compiler_params=_cparams(("parallel", "parallel")),
        name="gzx_proj",
    )(x, mod, mod, norm_g, w_gzx)


def _attn_kernel(*refs, n_src, lam_init):
    q_ref = refs[0]
    srcs = [(refs[1 + 2 * i], refs[2 + 2 * i]) for i in range(n_src)]
    pos = 1 + 2 * n_src
    g_ref, sg_ref, lq1_ref, lk1_ref, lq2_ref, lk2_ref, o_ref = refs[pos:pos + 7]
    m_sc, l_sc, acc_sc = refs[pos + 7:pos + 10]

    tq = q_ref.shape[0]
    q = q_ref[...]
    lane = lax.broadcasted_iota(jnp.int32, q.shape, 1)
    zero = jnp.zeros_like(q)
    q2 = jnp.concatenate([jnp.where(lane < HD_QK, q, zero), jnp.where(lane >= HD_QK, q, zero)], axis=0)

    m_sc[...] = jnp.full(m_sc.shape, -jnp.inf, F32)
    l_sc[...] = jnp.zeros(l_sc.shape, F32)
    acc_sc[...] = jnp.zeros(acc_sc.shape, F32)

    for kt_ref, v_ref in srcs:
        n_keys = kt_ref.shape[1]
        tk = min(1024, n_keys)

        def body(j, carry, kt_ref=kt_ref, v_ref=v_ref, tk=tk):
            off = pl.multiple_of(j * tk, tk)
            s = jnp.dot(q2, kt_ref[:, pl.ds(off, tk)], preferred_element_type=F32)
            m_old = m_sc[...]
            m_new = jnp.maximum(m_old, jnp.max(s, axis=1, keepdims=True))
            alpha = jnp.exp(m_old - m_new)
            p = jnp.exp(s - m_new)
            l_sc[...] = alpha * l_sc[...] + jnp.sum(p, axis=1, keepdims=True)
            acc_sc[...] = alpha * acc_sc[...] + jnp.dot(
                p.astype(BF16), v_ref[pl.ds(off, tk), :], preferred_element_type=F32)
            m_sc[...] = m_new
            return carry

        lax.fori_loop(0, n_keys // tk, body, 0)

    lam = (jnp.exp(jnp.sum(lq1_ref[...] * lk1_ref[...], axis=1, keepdims=True))
           - jnp.exp(jnp.sum(lq2_ref[...] * lk2_ref[...], axis=1, keepdims=True)) + lam_init)
    o_n = acc_sc[...] / l_sc[...]
    o = o_n[:tq] - lam * o_n[tq:]
    ms = jnp.mean(o * o, axis=-1, keepdims=True)
    att = o * lax.rsqrt(ms + EPS) * sg_ref[...] * (1.0 - lam_init)
    o_ref[...] = (att * _silu(g_ref[...])).astype(BF16)


def _attention(q, kvs, g, subln_g, lams, lam_init, tq):
    B, L, _ = q.shape
    in_specs = [pl.BlockSpec((None, tq, HD_V), lambda b, h, i: (b, i, h))]
    args = [q]
    for kt, v in kvs:
        n_keys = kt.shape[2]
        in_specs += [pl.BlockSpec((None, HD_V, n_keys), lambda b, h, i: (b, h, 0)),
                     pl.BlockSpec((None, n_keys, HD_V), lambda b, h, i: (b, 0, h))]
        args += [kt, v]
    in_specs += [pl.BlockSpec((None, tq, HD_V), lambda b, h, i: (b, i, h)),
                 pl.BlockSpec((1, HD_V), lambda b, h, i: (0, 0))]
    in_specs += [pl.BlockSpec((1, HD_QK), lambda b, h, i: (0, 0))] * 4
    args += [g, subln_g] + list(lams)
    return pl.pallas_call(
        functools.partial(_attn_kernel, n_src=len(kvs), lam_init=lam_init),
        grid=(B, H_ATT, L // tq),
        in_specs=in_specs,
        out_specs=pl.BlockSpec((None, tq, HD_V), lambda b, h, i: (b, i, h)),
        out_shape=jax.ShapeDtypeStruct((B, L, D_ATT), BF16),
        scratch_shapes=[pltpu.VMEM((2 * tq, 1), F32), pltpu.VMEM((2 * tq, 1), F32),
                        pltpu.VMEM((2 * tq, HD_V), F32)],
        compiler_params=_cparams(("parallel", "parallel", "arbitrary")),
        name="diff_attention",
    )(*args)


def _ssd_kernel(*refs, seq_len, has_h0):
    (x_ref, bc_ref, dt_ref, cwx_ref, cwbc_ref, cbx_ref, cbbc_ref,
     dtb_ref, alog_ref, dsk_ref) = refs[:10]
    pos = 10
    if has_h0:
        h0_ref = refs[pos]
        pos += 1
    y_ref, hfin_ref, xc_sc, bcc_sc = refs[pos:pos + 4]

    nc = seq_len // CHUNK
    grp = pl.program_id(1)
    row_i = lax.broadcasted_iota(jnp.int32, (CHUNK, CHUNK), 0)
    col_i = lax.broadcasted_iota(jnp.int32, (CHUNK, CHUNK), 1)
    row1 = lax.broadcasted_iota(jnp.int32, (CHUNK, 1), 0)

    def conv_chunk(c, carry):
        r0 = pl.multiple_of(c * CHUNK, CHUNK)
        rp = pl.multiple_of(jnp.maximum(r0 - 8, 0), 8)
        rn = pl.multiple_of(jnp.minimum(r0 + CHUNK, seq_len - 8), 8)
        for src, w_ref, b_ref, dst in ((x_ref, cwx_ref, cbx_ref, xc_sc),
                                       (bc_ref, cwbc_ref, cbbc_ref, bcc_sc)):
            u = src[pl.ds(r0, CHUNK), :]
            pr = jnp.where(c > 0, src[pl.ds(rp, 8), :][7:8, :], 0.0)
            nx = jnp.where(c < nc - 1, src[pl.ds(rn, 8), :][0:1, :], 0.0)
            prev = jnp.where(row1 == 0, pr, pltpu.roll(u, 1, 0))
            nxt = jnp.where(row1 == CHUNK - 1, nx, pltpu.roll(u, CHUNK - 1, 0))
            w = w_ref[...]
            o = b_ref[...] + prev * w[0:1, :] + u * w[1:2, :] + nxt * w[2:3, :]
            dst[pl.ds(r0, CHUNK), :] = _silu(o)
        return carry

    lax.fori_loop(0, nc, conv_chunk, 0)

    dtb = dtb_ref[...]
    a_neg = -jnp.exp(alog_ref[...])
    dsk = dsk_ref[...]
    lane_shift = (LANES - grp * 2 * HEADS_PER_GROUP) % LANES

    def scan_dir(d):
        mask = (col_i <= row_i) if d == 0 else (col_i >= row_i)
        tmat = mask.astype(F32)
        end_row = CHUNK - 1 if d == 0 else 0
        if has_h0:
            s0 = h0_ref[d]
        else:
            s0 = jnp.zeros((D_STATE, GROUP_W), F32)

        def chunk(ci, state):
            c = ci if d == 0 else nc - 1 - ci
            r0 = pl.multiple_of(c * CHUNK, CHUNK)
            xc = xc_sc[pl.ds(r0, CHUNK), :]
            bc = bcc_sc[pl.ds(r0, CHUNK), :]
            dtl = pltpu.roll(dt_ref[pl.ds(r0, CHUNK), :], lane_shift, 1)
            sp = _softplus(dtl + dtb)
            cum = jnp.dot(tmat, sp * a_neg, precision=lax.Precision.HIGHEST,
                          preferred_element_type=F32)
            cum_t = cum.T
            bcb = bc.astype(BF16)
            b_m = bcb[:, :D_STATE]
            c_m = bcb[:, D_STATE:]
            b_t = bc.T[:D_STATE, :].astype(BF16)
            cb = lax.dot_general(c_m, b_m, (((1,), (1,)), ((), ())), preferred_element_type=F32)
            ys, sts = [], []
            for hh in range(HEADS_PER_GROUP):
                j = d * HEADS_PER_GROUP + hh
                aq = cum[:, j:j + 1]
                seg = aq - cum_t[j:j + 1, :]
                lmat = jnp.exp(jnp.where(mask, seg, -jnp.inf))
                xh = xc[:, hh * SSD_HEADDIM:(hh + 1) * SSD_HEADDIM]
                xdt = xh * sp[:, j:j + 1]
                y_diag = jnp.dot((cb * lmat).astype(BF16), xdt.astype(BF16), preferred_element_type=F32)
                sh = state[:, hh * SSD_HEADDIM:(hh + 1) * SSD_HEADDIM]
                y_off = jnp.dot(c_m, sh.astype(BF16), preferred_element_type=F32) * jnp.exp(aq)
                a_end = cum[end_row:end_row + 1, j:j + 1]
                st = jnp.dot(b_t, (xdt * jnp.exp(a_end - aq)).astype(BF16), preferred_element_type=F32)
                ys.append(y_diag + y_off)
                sts.append(sh * jnp.exp(a_end) + st)
            y_chunk = jnp.concatenate(ys, axis=1)
            if d == 0:
                y_ref[pl.ds(r0, CHUNK), :] = y_chunk + dsk * xc
            else:
                y_ref[pl.ds(r0, CHUNK), :] = y_ref[pl.ds(r0, CHUNK), :] + y_chunk
            return jnp.concatenate(sts, axis=1)

        hfin_ref[d] = lax.fori_loop(0, nc, chunk, s0)

    scan_dir(0)
    scan_dir(1)


def _ssd(xbc, dt, conv_w, conv_b, dtb_g, alog_g, dsk, h0):
    B, L, _ = xbc.shape
    xblk = D_SSD // GROUP_W
    in_specs = [
        pl.BlockSpec((None, L, GROUP_W), lambda b, g: (b, 0, g)),
        pl.BlockSpec((None, L, LANES), lambda b, g: (b, 0, D_SSD // LANES + g)),
        pl.BlockSpec((None, L, DT_PAD), lambda b, g: (b, 0, 0)),
        pl.BlockSpec((3, GROUP_W), lambda b, g: (0, g)),
        pl.BlockSpec((3, LANES), lambda b, g: (0, D_SSD // LANES + g)),
        pl.BlockSpec((1, GROUP_W), lambda b, g: (0, g)),
        pl.BlockSpec((1, LANES), lambda b, g: (0, D_SSD // LANES + g)),
        pl.BlockSpec((None, 1, LANES), lambda b, g: (g, 0, 0)),
        pl.BlockSpec((None, 1, LANES), lambda b, g: (g, 0, 0)),
        pl.BlockSpec((1, GROUP_W), lambda b, g: (0, g)),
    ]
    del xblk
    args = [xbc, xbc, dt, conv_w, conv_w, conv_b, conv_b, dtb_g, alog_g, dsk]
    if h0 is not None:
        in_specs.append(pl.BlockSpec((None, 2, None, D_STATE, GROUP_W), lambda b, g: (b, 0, g, 0, 0)))
        args.append(h0)
    return pl.pallas_call(
        functools.partial(_ssd_kernel, seq_len=L, has_h0=h0 is not None),
        grid=(B, SSD_GROUPS),
        in_specs=in_specs,
        out_specs=[pl.BlockSpec((None, L, GROUP_W), lambda b, g: (b, 0, g)),
                   pl.BlockSpec((None, 2, None, D_STATE, GROUP_W), lambda b, g: (b, 0, g, 0, 0))],
        out_shape=[jax.ShapeDtypeStruct((B, L, D_SSD), F32),
                   jax.ShapeDtypeStruct((B, 2, SSD_GROUPS, D_STATE, GROUP_W), F32)],
        scratch_shapes=[pltpu.VMEM((L, GROUP_W), F32), pltpu.VMEM((L, LANES), F32)],
        compiler_params=_cparams(("parallel", "parallel")),
        name="ssd_scan",
    )(*args)


def _out_kernel(x_ref, att_ref, y_ref, z_ref, sng_ref, w_ref, gate_ref, fg_ref, o_ref):
    yz = y_ref[...] * _silu(z_ref[...])
    ms = jnp.mean(yz * yz, axis=-1, keepdims=True)
    ssd_y = (yz * lax.rsqrt(ms + EPS) * sng_ref[...]).astype(BF16)
    out = jnp.dot(att_ref[...], w_ref[0:D_ATT, :], preferred_element_type=F32)
    out = out + jnp.dot(ssd_y, w_ref[D_ATT:D_ATT + D_SSD, :], preferred_element_type=F32)
    r = x_ref[...] + gate_ref[...] * out
    ms2 = jnp.mean(r * r, axis=-1, keepdims=True)
    o_ref[...] = r * lax.rsqrt(ms2 + EPS) * fg_ref[...]


def _out_proj(x, att, y, z, ssd_norm_g, w_out, mod, mod_row, final_g, tm):
    B, L, _ = x.shape
    row = mod_row
    tok = lambda b, i: (b, i, 0)
    in_specs = [
        pl.BlockSpec((None, tm, D_MODEL), tok),
        pl.BlockSpec((None, tm, D_ATT), tok),
        pl.BlockSpec((None, tm, D_SSD), tok),
        pl.BlockSpec((None, tm, D_SSD), tok),
        pl.BlockSpec((1, D_SSD), lambda b, i: (0, 0)),
        pl.BlockSpec((D_ATT + D_SSD, D_MODEL), lambda b, i: (0, 0), pipeline_mode=pl.Buffered(1)),
        pl.BlockSpec((None, 1, D_MODEL), lambda b, i: (row(b), 0, 2)),
        pl.BlockSpec((1, D_MODEL), lambda b, i: (0, 0)),
    ]
    return pl.pallas_call(
        _out_kernel, grid=(B, L // tm), in_specs=in_specs,
        out_specs=pl.BlockSpec((None, tm, D_MODEL), tok),
        out_shape=jax.ShapeDtypeStruct((B, L, D_MODEL), F32),
        compiler_params=_cparams(("parallel", "parallel")),
        name="out_proj",
    )(x, att, y, z, ssd_norm_g, w_out, mod, final_g)


def _rope_tables(L):
    rows = L // GRID_W
    row_ids = jnp.repeat(jnp.arange(rows), GRID_W).astype(F32)
    col_ids = jnp.tile(jnp.arange(GRID_W), rows).astype(F32)
    inv = ROPE_BASE ** (-jnp.arange(ROPE_FREQS, dtype=F32) / ROPE_FREQS)
    ang_r = row_ids[:, None] * inv
    ang_c = col_ids[:, None] * inv
    cr, sr, cc, sc = jnp.cos(ang_r), jnp.sin(ang_r), jnp.cos(ang_c), jnp.sin(ang_c)
    cos64 = jnp.concatenate([cr, cr, cc, cc], axis=1)
    sin64 = jnp.concatenate([-sr, sr, -sc, sc], axis=1)
    return jnp.tile(cos64, (1, 2)), jnp.tile(sin64, (1, 2))


def _bc_perm():
    idx = list(range(D_SSD))
    for g in range(SSD_GROUPS):
        idx += [D_SSD + g * D_STATE + n for n in range(D_STATE)]
        idx += [D_SSD + SSD_GROUPS * D_STATE + g * D_STATE + n for n in range(D_STATE)]
    return jnp.asarray(idx, dtype=jnp.int32)


def _dt_perm():
    idx = []
    for g in range(SSD_GROUPS):
        for d in range(2):
            for hh in range(HEADS_PER_GROUP):
                idx.append(d * H_SSD + g * HEADS_PER_GROUP + hh)
    return jnp.asarray(idx, dtype=jnp.int32)


def _group_lanes(v2h):
    t = v2h.reshape(2, SSD_GROUPS, HEADS_PER_GROUP).transpose(1, 0, 2).reshape(SSD_GROUPS, 2 * HEADS_PER_GROUP)
    t = jnp.pad(t, ((0, 0), (0, LANES - 2 * HEADS_PER_GROUP)))
    return t.reshape(SSD_GROUPS, 1, LANES)


def _layer(x, mod, mod_row, params, rope_tabs, past_kv, h0, emit_f32, lam_init, final_g, tm, tq):
    (norm_g, w_qkv, w_gzx, lams, subln_g, conv_w, conv_b, dtb_g, alog_g, dsk, ssd_norm_g, w_out) = params
    proj = _qkv_proj(x, mod, mod_row, norm_g, w_qkv, rope_tabs, emit_f32, tm)
    q, kt, v = proj[:3]
    g, z, xbc, dt = _gzx_proj(x, mod, mod_row, norm_g, w_gzx, tm)
    kvs = [(kt, v)]
    if past_kv is not None:
        kvs.append(past_kv)
    att = _attention(q, kvs, g, subln_g, lams, lam_init, tq)
    y, hfin = _ssd(xbc, dt, conv_w, conv_b, dtb_g, alog_g, dsk, h0)
    out = _out_proj(x, att, y, z, ssd_norm_g, w_out, mod, mod_row, final_g, tm)
    return out, proj[3:], hfin


def kernel(x_prompt, x_sample, cache_k, cache_v, state_ssd, c, c_ctx, w_mod, b_mod, norm_g, w_in,
           lambda_q1, lambda_k1, lambda_q2, lambda_k2, subln_g, conv_w, conv_b, dt_bias, A_log,
           D_skip, ssd_norm_g, w_out, final_g):
    b_ctx, l_ctx, _ = x_prompt.shape
    b_dec, l_dec, _ = x_sample.shape
    l_past = cache_k.shape[2]
    depth = w_mod.shape[0]
    assert depth == 1
    lam_init = 0.8 - 0.6 * math.exp(-0.3 * 0)

    w = w_in[0]
    w_qkv = w[:, :3 * D_ATT].astype(BF16)
    o = 3 * D_ATT
    w_g = w[:, o:o + D_ATT]
    w_z = w[:, o + D_ATT:o + D_ATT + D_SSD]
    o2 = o + D_ATT + D_SSD
    bc_perm = _bc_perm()
    w_xbc = w[:, o2:o2 + CONV_DIM][:, bc_perm]
    w_dt = jnp.pad(w[:, o2 + CONV_DIM:][:, _dt_perm()], ((0, 0), (0, DT_PAD - 2 * H_SSD)))
    w_gzx = jnp.concatenate([w_g, w_z, w_xbc, w_dt], axis=1).astype(BF16)
    conv_w_p = conv_w[0][:, bc_perm]
    conv_b_p = conv_b[0][bc_perm].reshape(1, CONV_DIM)
    dtb_g = _group_lanes(dt_bias[0])
    alog_g = _group_lanes(A_log[0])
    dsk = jnp.repeat(D_skip[0], SSD_HEADDIM).reshape(1, D_SSD)
    lams = [a[0].reshape(1, HD_QK) for a in (lambda_q1, lambda_k1, lambda_q2, lambda_k2)]
    params = (norm_g[0].reshape(1, D_MODEL), w_qkv, w_gzx, lams, subln_g[0].reshape(1, HD_V),
              conv_w_p, conv_b_p, dtb_g, alog_g, dsk, ssd_norm_g[0].reshape(1, D_SSD),
              w_out[0].astype(BF16))
    fg = final_g.reshape(1, D_MODEL)

    cond8 = jnp.zeros((8, D_MODEL), F32).at[:b_dec].set(c).at[b_dec].set(c_ctx)
    mod = _adaln(cond8, w_mod[0], b_mod[0]).reshape(8, 1, 3 * D_MODEL)

    y_prompt, (k_ctx, v_ctx), h_ctx = _layer(
        x_prompt, mod, lambda b: b_dec, params, None, None, None, True, lam_init, fg,
        tm=256, tq=256)

    rope_tabs = _rope_tables(l_dec)
    kt_past = cache_k[:, 0].reshape(b_dec, l_past, D_ATT).transpose(0, 2, 1).astype(BF16)
    v_past = cache_v[:, 0].reshape(b_dec, l_past, D_ATT).astype(BF16)
    h0 = state_ssd[:, 0].reshape(b_dec, 2, SSD_GROUPS, HEADS_PER_GROUP, SSD_HEADDIM, D_STATE)
    h0 = h0.transpose(0, 1, 2, 5, 3, 4).reshape(b_dec, 2, SSD_GROUPS, D_STATE, GROUP_W)
    y_sample, _, _ = _layer(
        x_sample, mod, lambda b: b, params, rope_tabs, (kt_past, v_past), h0, False, lam_init, fg,
        tm=512, tq=256)

    new_cache_k = k_ctx.reshape(b_ctx, 1, l_ctx, H_ATT, 2 * HD_QK)
    new_cache_v = v_ctx.reshape(b_ctx, 1, l_ctx, H_ATT, HD_V)
    hs = h_ctx.reshape(b_ctx, 2, SSD_GROUPS, D_STATE, HEADS_PER_GROUP, SSD_HEADDIM)
    new_state = hs.transpose(0, 1, 2, 4, 5, 3).reshape(b_ctx, 1, 2, H_SSD, SSD_HEADDIM, D_STATE)
    return (y_prompt, y_sample, new_cache_k, new_cache_v, new_state)
```

```python
import functools
import math

import jax
import jax.numpy as jnp
from jax import lax
from jax.experimental import pallas as pl
from jax.experimental.pallas import tpu as pltpu

F32 = jnp.float32
BF16 = jnp.bfloat16

D_MODEL = 1024
GRID_W = 64
HD_QK = 64
HD_V = 128
H_ATT = 8
D_ATT = 1024
D_SSD = 1024
SSD_HEADDIM = 64
H_SSD = 16
SSD_GROUPS = 4
HEADS_PER_GROUP = 4
D_STATE = 64
CONV_DIM = D_SSD + 2 * SSD_GROUPS * D_STATE
CHUNK = 128
ROPE_BASE = 10000.0
ROPE_FREQS = 16
EPS = 1e-6

LANES = 128
DT_PAD = LANES
GROUP_W = HEADS_PER_GROUP * SSD_HEADDIM
VMEM_LIMIT = 56 * 1024 * 1024
ATT_TK = 512
LOG2E = 1.4426950408889634


def _cparams(sem):
    return pltpu.CompilerParams(dimension_semantics=sem, vmem_limit_bytes=VMEM_LIMIT)


def _silu(x):
    return x * jax.nn.sigmoid(x)


def _softplus(x):
    return jnp.maximum(x, 0.0) + jnp.log1p(jnp.exp(-jnp.abs(x)))


def _adaln_kernel(cond_ref, w_ref, b_ref, o_ref):
    cond = cond_ref[...]
    s = _silu(cond).astype(BF16)
    o_ref[...] = jnp.dot(s, w_ref[...].astype(BF16), preferred_element_type=F32) + b_ref[...]


def _adaln(cond8, w_mod, b_mod):
    n = w_mod.shape[1]
    blk = 1024
    return pl.pallas_call(
        _adaln_kernel,
        grid=(n // blk,),
        in_specs=[pl.BlockSpec((8, D_MODEL), lambda j: (0, 0)),
                  pl.BlockSpec((D_MODEL, blk), lambda j: (0, j)),
                  pl.BlockSpec((1, blk), lambda j: (0, j))],
        out_specs=pl.BlockSpec((8, blk), lambda j: (0, j)),
        out_shape=jax.ShapeDtypeStruct((8, n), F32),
        compiler_params=_cparams(("arbitrary",)),
        name="adaln",
    )(cond8, w_mod, b_mod.reshape(1, n))


def _mod_norm(x, shift, scale, g):
    ms = jnp.mean(x * x, axis=-1, keepdims=True)
    y = x * lax.rsqrt(ms + EPS) * g
    return y * (1.0 + scale) + shift


def _rope(t, cos, sin_signed, first):
    outs = []
    for h in range(H_ATT):
        th = t[:, h * LANES:(h + 1) * LANES]
        partner = jnp.where(first, pltpu.roll(th, LANES - 16, 1), pltpu.roll(th, 16, 1))
        outs.append(th * cos + partner * sin_signed)
    return jnp.concatenate(outs, axis=1)


def _qkv_kernel(*refs, rope, emit_f32):
    x_ref, shift_ref, scale_ref, ng_ref, w_ref = refs[:5]
    pos = 5
    if rope:
        cos_ref, sin_ref = refs[pos:pos + 2]
        pos += 2
    qt_ref, k_ref, vt_ref = refs[pos:pos + 3]
    pos += 3
    if emit_f32:
        k32_ref, v32_ref = refs[pos:pos + 2]

    hb = _mod_norm(x_ref[...], shift_ref[...], scale_ref[...], ng_ref[...]).astype(BF16)
    q = jnp.dot(hb, w_ref[:, 0:D_ATT], preferred_element_type=F32)
    k = jnp.dot(hb, w_ref[:, D_ATT:2 * D_ATT], preferred_element_type=F32)
    v = jnp.dot(hb, w_ref[:, 2 * D_ATT:3 * D_ATT], preferred_element_type=F32)
    if rope:
        cos = cos_ref[...]
        sin_signed = sin_ref[...]
        lane = lax.broadcasted_iota(jnp.int32, cos.shape, 1)
        first = (lane % 32) < 16
        q = _rope(q, cos, sin_signed, first)
        k = _rope(k, cos, sin_signed, first)
    qt_ref[...] = (q * (LOG2E / math.sqrt(HD_QK))).T.astype(BF16)
    k_ref[...] = k.astype(BF16)
    vt_ref[...] = v.T.astype(BF16)
    if emit_f32:
        k32_ref[...] = k
        v32_ref[...] = v


def _qkv_proj(x, mod, mod_row, norm_g, w_qkv, rope_tabs, emit_f32, tm):
    B, L, _ = x.shape
    grid = (B, L // tm)
    row = mod_row

    in_specs = [
        pl.BlockSpec((None, tm, D_MODEL), lambda b, i: (b, i, 0)),
        pl.BlockSpec((None, 1, D_MODEL), lambda b, i: (row(b), 0, 0)),
        pl.BlockSpec((None, 1, D_MODEL), lambda b, i: (row(b), 0, 1)),
        pl.BlockSpec((1, D_MODEL), lambda b, i: (0, 0)),
        pl.BlockSpec((D_MODEL, 3 * D_ATT), lambda b, i: (0, 0), pipeline_mode=pl.Buffered(1)),
    ]
    args = [x, mod, mod, norm_g, w_qkv]
    if rope_tabs is not None:
        in_specs += [pl.BlockSpec((tm, LANES), lambda b, i: (i, 0)),
                     pl.BlockSpec((tm, LANES), lambda b, i: (i, 0))]
        args += list(rope_tabs)
    out_specs = [
        pl.BlockSpec((None, D_ATT, tm), lambda b, i: (b, 0, i)),
        pl.BlockSpec((None, tm, D_ATT), lambda b, i: (b, i, 0)),
        pl.BlockSpec((None, D_ATT, tm), lambda b, i: (b, 0, i)),
    ]
    out_shape = [
        jax.ShapeDtypeStruct((B, D_ATT, L), BF16),
        jax.ShapeDtypeStruct((B, L, D_ATT), BF16),
        jax.ShapeDtypeStruct((B, D_ATT, L), BF16),
    ]
    if emit_f32:
        out_specs += [pl.BlockSpec((None, tm, D_ATT), lambda b, i: (b, i, 0))] * 2
        out_shape += [jax.ShapeDtypeStruct((B, L, D_ATT), F32)] * 2
    return pl.pallas_call(
        functools.partial(_qkv_kernel, rope=rope_tabs is not None, emit_f32=emit_f32),
        grid=grid, in_specs=in_specs, out_specs=out_specs, out_shape=out_shape,
        compiler_params=_cparams(("parallel", "parallel")),
        name="qkv_proj",
    )(*args)


N_GZX = 2 * D_MODEL + CONV_DIM + DT_PAD


def _gzx_kernel(x_ref, shift_ref, scale_ref, ng_ref, w_ref, g_ref, z_ref, xbc_ref, dt_ref):
    hb = _mod_norm(x_ref[...], shift_ref[...], scale_ref[...], ng_ref[...]).astype(BF16)
    g_ref[...] = jnp.dot(hb, w_ref[:, 0:D_ATT], preferred_element_type=F32)
    z_ref[...] = jnp.dot(hb, w_ref[:, D_ATT:D_ATT + D_SSD], preferred_element_type=F32)
    o = D_ATT + D_SSD
    xbc_ref[...] = jnp.dot(hb, w_ref[:, o:o + CONV_DIM], preferred_element_type=F32)
    dt_ref[...] = jnp.dot(hb, w_ref[:, o + CONV_DIM:o + CONV_DIM + DT_PAD], preferred_element_type=F32)


def _gzx_proj(x, mod, mod_row, norm_g, w_gzx, tm):
    B, L, _ = x.shape
    row = mod_row
    in_specs = [
        pl.BlockSpec((None, tm, D_MODEL), lambda b, i: (b, i, 0)),
        pl.BlockSpec((None, 1, D_MODEL), lambda b, i: (row(b), 0, 0)),
        pl.BlockSpec((None, 1, D_MODEL), lambda b, i: (row(b), 0, 1)),
        pl.BlockSpec((1, D_MODEL), lambda b, i: (0, 0)),
        pl.BlockSpec((D_MODEL, N_GZX), lambda b, i: (0, 0), pipeline_mode=pl.Buffered(1)),
    ]
    widths = (D_ATT, D_SSD, CONV_DIM, DT_PAD)
    out_specs = [pl.BlockSpec((None, tm, w), lambda b, i: (b, i, 0)) for w in widths]
    out_shape = [jax.ShapeDtypeStruct((B, L, w), F32) for w in widths]
    return pl.pallas_call(
        _gzx_kernel, grid=(B, L // tm), in_specs=in_specs, out_specs=out_specs, out_shape=out_shape,
        compiler_params=_cparams(("parallel", "parallel")),
        name="gzx_proj",
    )(x, mod, mod, norm_g, w_gzx)


def _attn_kernel(*refs, n_src, lam_init):
    qt_ref = refs[0]
    srcs = [(refs[1 + 2 * i], refs[2 + 2 * i]) for i in range(n_src)]
    pos = 1 + 2 * n_src
    g_ref, sg_ref, lq1_ref, lk1_ref, lq2_ref, lk2_ref, o_ref = refs[pos:pos + 7]

    tq = qt_ref.shape[1]
    qt = qt_ref[...]
    row = lax.broadcasted_iota(jnp.int32, qt.shape, 0)
    zero = jnp.zeros_like(qt)
    q2t = jnp.concatenate([jnp.where(row < HD_QK, qt, zero), jnp.where(row >= HD_QK, qt, zero)], axis=1)

    m = jnp.full((1, 2 * tq), -jnp.inf, F32)
    l = jnp.zeros((1, 2 * tq), F32)
    acc = jnp.zeros((HD_V, 2 * tq), F32)
    blocks = []
    for k_ref, vt_ref in srcs:
        n_keys = k_ref.shape[0]
        tk = min(ATT_TK, n_keys)
        blocks += [(k_ref, vt_ref, j * tk, tk) for j in range(n_keys // tk)]

    def scores(blk):
        k_ref, _, off, tk = blk
        return jnp.dot(k_ref[off:off + tk, :], q2t, preferred_element_type=F32)

    s_next = scores(blocks[0])
    for idx, (_, vt_ref, off, tk) in enumerate(blocks):
        s = s_next
        if idx + 1 < len(blocks):
            s_next = scores(blocks[idx + 1])
        m_new = jnp.maximum(m, jnp.max(s, axis=0, keepdims=True))
        alpha = jnp.exp2(m - m_new)
        p = jnp.exp2(s - m_new)
        l = alpha * l + jnp.sum(p, axis=0, keepdims=True)
        acc = alpha * acc + jnp.dot(vt_ref[:, off:off + tk], p.astype(BF16),
                                    preferred_element_type=F32)
        m = m_new

    lam = (jnp.exp(jnp.sum(lq1_ref[...] * lk1_ref[...], axis=1, keepdims=True))
           - jnp.exp(jnp.sum(lq2_ref[...] * lk2_ref[...], axis=1, keepdims=True)) + lam_init)
    o_n = acc / l
    o = (o_n[:, :tq] - lam * o_n[:, tq:]).T
    ms = jnp.mean(o * o, axis=-1, keepdims=True)
    att = o * lax.rsqrt(ms + EPS) * sg_ref[...] * (1.0 - lam_init)
    o_ref[...] = (att * _silu(g_ref[...])).astype(BF16)


def _attention(qt, kvs, g, subln_g, lams, lam_init, tq):
    B, _, L = qt.shape
    in_specs = [pl.BlockSpec((None, HD_V, tq), lambda b, h, i: (b, h, i))]
    args = [qt]
    for k, vt in kvs:
        n_keys = k.shape[1]
        in_specs += [pl.BlockSpec((None, n_keys, HD_V), lambda b, h, i: (b, 0, h)),
                     pl.BlockSpec((None, HD_V, n_keys), lambda b, h, i: (b, h, 0))]
        args += [k, vt]
    in_specs += [pl.BlockSpec((None, tq, HD_V), lambda b, h, i: (b, i, h)),
                 pl.BlockSpec((1, HD_V), lambda b, h, i: (0, 0))]
    in_specs += [pl.BlockSpec((1, HD_QK), lambda b, h, i: (0, 0))] * 4
    args += [g, subln_g] + list(lams)
    return pl.pallas_call(
        functools.partial(_attn_kernel, n_src=len(kvs), lam_init=lam_init),
        grid=(B, H_ATT, L // tq),
        in_specs=in_specs,
        out_specs=pl.BlockSpec((None, tq, HD_V), lambda b, h, i: (b, i, h)),
        out_shape=jax.ShapeDtypeStruct((B, L, D_ATT), BF16),
        compiler_params=_cparams(("parallel", "parallel", "arbitrary")),
        name="diff_attention",
    )(*args)


def _ssd_kernel(*refs, seq_len, has_h0):
    (x_ref, bc_ref, dt_ref, cwx_ref, cwbc_ref, cbx_ref, cbbc_ref,
     dtb_ref, alog_ref, dsk_ref) = refs[:10]
    pos = 10
    if has_h0:
        h0_ref = refs[pos]
        pos += 1
    y_ref, hfin_ref, xc_sc, bcc_sc = refs[pos:pos + 4]

    nc = seq_len // CHUNK
    grp = pl.program_id(1)
    row_i = lax.broadcasted_iota(jnp.int32, (CHUNK, CHUNK), 0)
    col_i = lax.broadcasted_iota(jnp.int32, (CHUNK, CHUNK), 1)
    row1 = lax.broadcasted_iota(jnp.int32, (CHUNK, 1), 0)

    def conv_chunk(c, carry):
        r0 = pl.multiple_of(c * CHUNK, CHUNK)
        rp = pl.multiple_of(jnp.maximum(r0 - 8, 0), 8)
        rn = pl.multiple_of(jnp.minimum(r0 + CHUNK, seq_len - 8), 8)
        for src, w_ref, b_ref, dst in ((x_ref, cwx_ref, cbx_ref, xc_sc),
                                       (bc_ref, cwbc_ref, cbbc_ref, bcc_sc)):
            u = src[pl.ds(r0, CHUNK), :]
            pr = jnp.where(c > 0, src[pl.ds(rp, 8), :][7:8, :], 0.0)
            nx = jnp.where(c < nc - 1, src[pl.ds(rn, 8), :][0:1, :], 0.0)
            prev = jnp.where(row1 == 0, pr, pltpu.roll(u, 1, 0))
            nxt = jnp.where(row1 == CHUNK - 1, nx, pltpu.roll(u, CHUNK - 1, 0))
            w = w_ref[...]
            o = b_ref[...] + prev * w[0:1, :] + u * w[1:2, :] + nxt * w[2:3, :]
            dst[pl.ds(r0, CHUNK), :] = _silu(o)
        return carry

    lax.fori_loop(0, nc, conv_chunk, 0)

    dtb = dtb_ref[...]
    a_neg = -jnp.exp(alog_ref[...])
    dsk = dsk_ref[...]
    lane_shift = (LANES - grp * 2 * HEADS_PER_GROUP) % LANES

    def scan_dir(d):
        mask = (col_i <= row_i) if d == 0 else (col_i >= row_i)
        tmat = mask.astype(F32)
        end_row = CHUNK - 1 if d == 0 else 0
        if has_h0:
            s0 = h0_ref[d]
        else:
            s0 = jnp.zeros((D_STATE, GROUP_W), F32)

        def chunk(ci, state):
            c = ci if d == 0 else nc - 1 - ci
            r0 = pl.multiple_of(c * CHUNK, CHUNK)
            xc = xc_sc[pl.ds(r0, CHUNK), :]
            bc = bcc_sc[pl.ds(r0, CHUNK), :]
            dtl = pltpu.roll(dt_ref[pl.ds(r0, CHUNK), :], lane_shift, 1)
            sp = _softplus(dtl + dtb)
            cum = jnp.dot(tmat, sp * a_neg, precision=lax.Precision.HIGHEST,
                          preferred_element_type=F32)
            cum_t = cum.T
            bcb = bc.astype(BF16)
            b_m = bcb[:, :D_STATE]
            c_m = bcb[:, D_STATE:]
            b_t = bc.T[:D_STATE, :].astype(BF16)
            cb = lax.dot_general(c_m, b_m, (((1,), (1,)), ((), ())), preferred_element_type=F32)
            ys, sts = [], []
            for hh in range(HEADS_PER_GROUP):
                j = d * HEADS_PER_GROUP + hh
                aq = cum[:, j:j + 1]
                seg = aq - cum_t[j:j + 1, :]
                lmat = jnp.exp(jnp.where(mask, seg, -jnp.inf))
                xh = xc[:, hh * SSD_HEADDIM:(hh + 1) * SSD_HEADDIM]
                xdt = xh * sp[:, j:j + 1]
                y_diag = jnp.dot((cb * lmat).astype(BF16), xdt.astype(BF16), preferred_element_type=F32)
                sh = state[:, hh * SSD_HEADDIM:(hh + 1) * SSD_HEADDIM]
                y_off = jnp.dot(c_m, sh.astype(BF16), preferred_element_type=F32) * jnp.exp(aq)
                a_end = cum[end_row:end_row + 1, j:j + 1]
                st = jnp.dot(b_t, (xdt * jnp.exp(a_end - aq)).astype(BF16), preferred_element_type=F32)
                ys.append(y_diag + y_off)
                sts.append(sh * jnp.exp(a_end) + st)
            y_chunk = jnp.concatenate(ys, axis=1)
            if d == 0:
                y_ref[pl.ds(r0, CHUNK), :] = y_chunk + dsk * xc
            else:
                y_ref[pl.ds(r0, CHUNK), :] = y_ref[pl.ds(r0, CHUNK), :] + y_chunk
            return jnp.concatenate(sts, axis=1)

        hfin_ref[d] = lax.fori_loop(0, nc, chunk, s0)

    scan_dir(0)
    scan_dir(1)


def _ssd(xbc, dt, conv_w, conv_b, dtb_g, alog_g, dsk, h0):
    B, L, _ = xbc.shape
    xblk = D_SSD // GROUP_W
    in_specs = [
        pl.BlockSpec((None, L, GROUP_W), lambda b, g: (b, 0, g)),
        pl.BlockSpec((None, L, LANES), lambda b, g: (b, 0, D_SSD // LANES + g)),
        pl.BlockSpec((None, L, DT_PAD), lambda b, g: (b, 0, 0)),
        pl.BlockSpec((3, GROUP_W), lambda b, g: (0, g)),
        pl.BlockSpec((3, LANES), lambda b, g: (0, D_SSD // LANES + g)),
        pl.BlockSpec((1, GROUP_W), lambda b, g: (0, g)),
        pl.BlockSpec((1, LANES), lambda b, g: (0, D_SSD // LANES + g)),
        pl.BlockSpec((None, 1, LANES), lambda b, g: (g, 0, 0)),
        pl.BlockSpec((None, 1, LANES), lambda b, g: (g, 0, 0)),
        pl.BlockSpec((1, GROUP_W), lambda b, g: (0, g)),
    ]
    del xblk
    args = [xbc, xbc, dt, conv_w, conv_w, conv_b, conv_b, dtb_g, alog_g, dsk]
    if h0 is not None:
        in_specs.append(pl.BlockSpec((None, 2, None, D_STATE, GROUP_W), lambda b, g: (b, 0, g, 0, 0)))
        args.append(h0)
    return pl.pallas_call(
        functools.partial(_ssd_kernel, seq_len=L, has_h0=h0 is not None),
        grid=(B, SSD_GROUPS),
        in_specs=in_specs,
        out_specs=[pl.BlockSpec((None, L, GROUP_W), lambda b, g: (b, 0, g)),
                   pl.BlockSpec((None, 2, None, D_STATE, GROUP_W), lambda b, g: (b, 0, g, 0, 0))],
        out_shape=[jax.ShapeDtypeStruct((B, L, D_SSD), F32),
                   jax.ShapeDtypeStruct((B, 2, SSD_GROUPS, D_STATE, GROUP_W), F32)],
        scratch_shapes=[pltpu.VMEM((L, GROUP_W), F32), pltpu.VMEM((L, LANES), F32)],
        compiler_params=_cparams(("parallel", "parallel")),
        name="ssd_scan",
    )(*args)


def _out_kernel(x_ref, att_ref, y_ref, z_ref, sng_ref, w_ref, gate_ref, fg_ref, o_ref):
    yz = y_ref[...] * _silu(z_ref[...])
    ms = jnp.mean(yz * yz, axis=-1, keepdims=True)
    ssd_y = (yz * lax.rsqrt(ms + EPS) * sng_ref[...]).astype(BF16)
    out = jnp.dot(att_ref[...], w_ref[0:D_ATT, :], preferred_element_type=F32)
    out = out + jnp.dot(ssd_y, w_ref[D_ATT:D_ATT + D_SSD, :], preferred_element_type=F32)
    r = x_ref[...] + gate_ref[...] * out
    ms2 = jnp.mean(r * r, axis=-1, keepdims=True)
    o_ref[...] = r * lax.rsqrt(ms2 + EPS) * fg_ref[...]


def _out_proj(x, att, y, z, ssd_norm_g, w_out, mod, mod_row, final_g, tm):
    B, L, _ = x.shape
    row = mod_row
    tok = lambda b, i: (b, i, 0)
    in_specs = [
        pl.BlockSpec((None, tm, D_MODEL), tok),
        pl.BlockSpec((None, tm, D_ATT), tok),
        pl.BlockSpec((None, tm, D_SSD), tok),
        pl.BlockSpec((None, tm, D_SSD), tok),
        pl.BlockSpec((1, D_SSD), lambda b, i: (0, 0)),
        pl.BlockSpec((D_ATT + D_SSD, D_MODEL), lambda b, i: (0, 0), pipeline_mode=pl.Buffered(1)),
        pl.BlockSpec((None, 1, D_MODEL), lambda b, i: (row(b), 0, 2)),
        pl.BlockSpec((1, D_MODEL), lambda b, i: (0, 0)),
    ]
    return pl.pallas_call(
        _out_kernel, grid=(B, L // tm), in_specs=in_specs,
        out_specs=pl.BlockSpec((None, tm, D_MODEL), tok),
        out_shape=jax.ShapeDtypeStruct((B, L, D_MODEL), F32),
        compiler_params=_cparams(("parallel", "parallel")),
        name="out_proj",
    )(x, att, y, z, ssd_norm_g, w_out, mod, final_g)


def _rope_tables(L):
    rows = L // GRID_W
    row_ids = jnp.repeat(jnp.arange(rows), GRID_W).astype(F32)
    col_ids = jnp.tile(jnp.arange(GRID_W), rows).astype(F32)
    inv = ROPE_BASE ** (-jnp.arange(ROPE_FREQS, dtype=F32) / ROPE_FREQS)
    ang_r = row_ids[:, None] * inv
    ang_c = col_ids[:, None] * inv
    cr, sr, cc, sc = jnp.cos(ang_r), jnp.sin(ang_r), jnp.cos(ang_c), jnp.sin(ang_c)
    cos64 = jnp.concatenate([cr, cr, cc, cc], axis=1)
    sin64 = jnp.concatenate([-sr, sr, -sc, sc], axis=1)
    return jnp.tile(cos64, (1, 2)), jnp.tile(sin64, (1, 2))


def _bc_perm():
    idx = list(range(D_SSD))
    for g in range(SSD_GROUPS):
        idx += [D_SSD + g * D_STATE + n for n in range(D_STATE)]
        idx += [D_SSD + SSD_GROUPS * D_STATE + g * D_STATE + n for n in range(D_STATE)]
    return jnp.asarray(idx, dtype=jnp.int32)


def _dt_perm():
    idx = []
    for g in range(SSD_GROUPS):
        for d in range(2):
            for hh in range(HEADS_PER_GROUP):
                idx.append(d * H_SSD + g * HEADS_PER_GROUP + hh)
    return jnp.asarray(idx, dtype=jnp.int32)


def _group_lanes(v2h):
    t = v2h.reshape(2, SSD_GROUPS, HEADS_PER_GROUP).transpose(1, 0, 2).reshape(SSD_GROUPS, 2 * HEADS_PER_GROUP)
    t = jnp.pad(t, ((0, 0), (0, LANES - 2 * HEADS_PER_GROUP)))
    return t.reshape(SSD_GROUPS, 1, LANES)


def _layer(x, mod, mod_row, params, rope_tabs, past_kv, h0, emit_f32, lam_init, final_g, tm, tq):
    (norm_g, w_qkv, w_gzx, lams, subln_g, conv_w, conv_b, dtb_g, alog_g, dsk, ssd_norm_g, w_out) = params
    proj = _qkv_proj(x, mod, mod_row, norm_g, w_qkv, rope_tabs, emit_f32, tm)
    qt, k, vt = proj[:3]
    g, z, xbc, dt = _gzx_proj(x, mod, mod_row, norm_g, w_gzx, tm)
    kvs = [(k, vt)]
    if past_kv is not None:
        kvs.append(past_kv)
    att = _attention(qt, kvs, g, subln_g, lams, lam_init, tq)
    y, hfin = _ssd(xbc, dt, conv_w, conv_b, dtb_g, alog_g, dsk, h0)
    out = _out_proj(x, att, y, z, ssd_norm_g, w_out, mod, mod_row, final_g, tm)
    return out, proj[3:], hfin


def kernel(x_prompt, x_sample, cache_k, cache_v, state_ssd, c, c_ctx, w_mod, b_mod, norm_g, w_in,
           lambda_q1, lambda_k1, lambda_q2, lambda_k2, subln_g, conv_w, conv_b, dt_bias, A_log,
           D_skip, ssd_norm_g, w_out, final_g):
    b_ctx, l_ctx, _ = x_prompt.shape
    b_dec, l_dec, _ = x_sample.shape
    l_past = cache_k.shape[2]
    depth = w_mod.shape[0]
    assert depth == 1
    lam_init = 0.8 - 0.6 * math.exp(-0.3 * 0)

    w = w_in[0]
    w_qkv = w[:, :3 * D_ATT].astype(BF16)
    o = 3 * D_ATT
    w_g = w[:, o:o + D_ATT]
    w_z = w[:, o + D_ATT:o + D_ATT + D_SSD]
    o2 = o + D_ATT + D_SSD
    bc_perm = _bc_perm()
    w_xbc = w[:, o2:o2 + CONV_DIM][:, bc_perm]
    w_dt = jnp.pad(w[:, o2 + CONV_DIM:][:, _dt_perm()], ((0, 0), (0, DT_PAD - 2 * H_SSD)))
    w_gzx = jnp.concatenate([w_g, w_z, w_xbc, w_dt], axis=1).astype(BF16)
    conv_w_p = conv_w[0][:, bc_perm]
    conv_b_p = conv_b[0][bc_perm].reshape(1, CONV_DIM)
    dtb_g = _group_lanes(dt_bias[0])
    alog_g = _group_lanes(A_log[0])
    dsk = jnp.repeat(D_skip[0], SSD_HEADDIM).reshape(1, D_SSD)
    lams = [a[0].reshape(1, HD_QK) for a in (lambda_q1, lambda_k1, lambda_q2, lambda_k2)]
    params = (norm_g[0].reshape(1, D_MODEL), w_qkv, w_gzx, lams, subln_g[0].reshape(1, HD_V),
              conv_w_p, conv_b_p, dtb_g, alog_g, dsk, ssd_norm_g[0].reshape(1, D_SSD),
              w_out[0].astype(BF16))
    fg = final_g.reshape(1, D_MODEL)

    cond8 = jnp.zeros((8, D_MODEL), F32).at[:b_dec].set(c).at[b_dec].set(c_ctx)
    mod = _adaln(cond8, w_mod[0], b_mod[0]).reshape(8, 1, 3 * D_MODEL)

    y_prompt, (k_ctx, v_ctx), h_ctx = _layer(
        x_prompt, mod, lambda b: b_dec, params, None, None, None, True, lam_init, fg,
        tm=256, tq=256)

    rope_tabs = _rope_tables(l_dec)
    k_past = cache_k[:, 0].reshape(b_dec, l_past, D_ATT).astype(BF16)
    vt_past = cache_v[:, 0].reshape(b_dec, l_past, D_ATT).transpose(0, 2, 1).astype(BF16)
    h0 = state_ssd[:, 0].reshape(b_dec, 2, SSD_GROUPS, HEADS_PER_GROUP, SSD_HEADDIM, D_STATE)
    h0 = h0.transpose(0, 1, 2, 5, 3, 4).reshape(b_dec, 2, SSD_GROUPS, D_STATE, GROUP_W)
    y_sample, _, _ = _layer(
        x_sample, mod, lambda b: b, params, rope_tabs, (k_past, vt_past), h0, False, lam_init, fg,
        tm=512, tq=256)

    new_cache_k = k_ctx.reshape(b_ctx, 1, l_ctx, H_ATT, 2 * HD_QK)
    new_cache_v = v_ctx.reshape(b_ctx, 1, l_ctx, H_ATT, HD_V)
    hs = h_ctx.reshape(b_ctx, 2, SSD_GROUPS, D_STATE, HEADS_PER_GROUP, SSD_HEADDIM)
    new_state = hs.transpose(0, 1, 2, 4, 5, 3).reshape(b_ctx, 1, 2, H_SSD, SSD_HEADDIM, D_STATE)
    return (y_prompt, y_sample, new_cache_k, new_cache_v, new_state)
```

```python
import functools
import math

import jax
import jax.numpy as jnp
from jax import lax
from jax.experimental import pallas as pl
from jax.experimental.pallas import tpu as pltpu

F32 = jnp.float32
BF16 = jnp.bfloat16

D_MODEL = 1024
GRID_W = 64
HD_QK = 64
HD_V = 128
H_ATT = 8
D_ATT = 1024
D_SSD = 1024
SSD_HEADDIM = 64
H_SSD = 16
SSD_GROUPS = 4
HEADS_PER_GROUP = 4
D_STATE = 64
CONV_DIM = D_SSD + 2 * SSD_GROUPS * D_STATE
CHUNK = 128
ROPE_BASE = 10000.0
ROPE_FREQS = 16
EPS = 1e-6

LANES = 128
DT_PAD = LANES
GROUP_W = HEADS_PER_GROUP * SSD_HEADDIM
VMEM_LIMIT = 56 * 1024 * 1024
ATT_TK = 1024
LOG2E = 1.4426950408889634


def _cparams(sem):
    return pltpu.CompilerParams(dimension_semantics=sem, vmem_limit_bytes=VMEM_LIMIT)


def _silu(x):
    return x * jax.nn.sigmoid(x)


def _softplus(x):
    return jnp.maximum(x, 0.0) + jnp.log1p(jnp.exp(-jnp.abs(x)))


def _adaln_kernel(cond_ref, w_ref, b_ref, o_ref):
    cond = cond_ref[...]
    s = _silu(cond).astype(BF16)
    o_ref[...] = jnp.dot(s, w_ref[...].astype(BF16), preferred_element_type=F32) + b_ref[...]


def _adaln(cond8, w_mod, b_mod):
    n = w_mod.shape[1]
    blk = 1024
    return pl.pallas_call(
        _adaln_kernel,
        grid=(n // blk,),
        in_specs=[pl.BlockSpec((8, D_MODEL), lambda j: (0, 0)),
                  pl.BlockSpec((D_MODEL, blk), lambda j: (0, j)),
                  pl.BlockSpec((1, blk), lambda j: (0, j))],
        out_specs=pl.BlockSpec((8, blk), lambda j: (0, j)),
        out_shape=jax.ShapeDtypeStruct((8, n), F32),
        compiler_params=_cparams(("arbitrary",)),
        name="adaln",
    )(cond8, w_mod, b_mod.reshape(1, n))


def _mod_norm(x, shift, scale, g):
    ms = jnp.mean(x * x, axis=-1, keepdims=True)
    y = x * lax.rsqrt(ms + EPS) * g
    return y * (1.0 + scale) + shift


def _rope(t, cos, sin_signed, first):
    outs = []
    for h in range(H_ATT):
        th = t[:, h * LANES:(h + 1) * LANES]
        partner = jnp.where(first, pltpu.roll(th, LANES - 16, 1), pltpu.roll(th, 16, 1))
        outs.append(th * cos + partner * sin_signed)
    return jnp.concatenate(outs, axis=1)


def _qkv_kernel(*refs, rope, emit_f32):
    x_ref, shift_ref, scale_ref, ng_ref, w_ref = refs[:5]
    pos = 5
    if rope:
        cos_ref, sin_ref = refs[pos:pos + 2]
        pos += 2
    qt_ref, k_ref, vt_ref = refs[pos:pos + 3]
    pos += 3
    if emit_f32:
        k32_ref, v32_ref = refs[pos:pos + 2]

    hb = _mod_norm(x_ref[...], shift_ref[...], scale_ref[...], ng_ref[...]).astype(BF16)
    q = jnp.dot(hb, w_ref[:, 0:D_ATT], preferred_element_type=F32)
    k = jnp.dot(hb, w_ref[:, D_ATT:2 * D_ATT], preferred_element_type=F32)
    v = jnp.dot(hb, w_ref[:, 2 * D_ATT:3 * D_ATT], preferred_element_type=F32)
    if rope:
        cos = cos_ref[...]
        sin_signed = sin_ref[...]
        lane = lax.broadcasted_iota(jnp.int32, cos.shape, 1)
        first = (lane % 32) < 16
        q = _rope(q, cos, sin_signed, first)
        k = _rope(k, cos, sin_signed, first)
    qt_ref[...] = (q * (LOG2E / math.sqrt(HD_QK))).T.astype(BF16)
    k_ref[...] = k.astype(BF16)
    vt_ref[...] = v.T.astype(BF16)
    if emit_f32:
        k32_ref[...] = k
        v32_ref[...] = v


def _qkv_proj(x, mod, mod_row, norm_g, w_qkv, rope_tabs, emit_f32, tm):
    B, L, _ = x.shape
    grid = (B, L // tm)
    row = mod_row

    in_specs = [
        pl.BlockSpec((None, tm, D_MODEL), lambda b, i: (b, i, 0)),
        pl.BlockSpec((None, 1, D_MODEL), lambda b, i: (row(b), 0, 0)),
        pl.BlockSpec((None, 1, D_MODEL), lambda b, i: (row(b), 0, 1)),
        pl.BlockSpec((1, D_MODEL), lambda b, i: (0, 0)),
        pl.BlockSpec((D_MODEL, 3 * D_ATT), lambda b, i: (0, 0), pipeline_mode=pl.Buffered(1)),
    ]
    args = [x, mod, mod, norm_g, w_qkv]
    if rope_tabs is not None:
        in_specs += [pl.BlockSpec((tm, LANES), lambda b, i: (i, 0)),
                     pl.BlockSpec((tm, LANES), lambda b, i: (i, 0))]
        args += list(rope_tabs)
    out_specs = [
        pl.BlockSpec((None, D_ATT, tm), lambda b, i: (b, 0, i)),
        pl.BlockSpec((None, tm, D_ATT), lambda b, i: (b, i, 0)),
        pl.BlockSpec((None, D_ATT, tm), lambda b, i: (b, 0, i)),
    ]
    out_shape = [
        jax.ShapeDtypeStruct((B, D_ATT, L), BF16),
        jax.ShapeDtypeStruct((B, L, D_ATT), BF16),
        jax.ShapeDtypeStruct((B, D_ATT, L), BF16),
    ]
    if emit_f32:
        out_specs += [pl.BlockSpec((None, tm, D_ATT), lambda b, i: (b, i, 0))] * 2
        out_shape += [jax.ShapeDtypeStruct((B, L, D_ATT), F32)] * 2
    return pl.pallas_call(
        functools.partial(_qkv_kernel, rope=rope_tabs is not None, emit_f32=emit_f32),
        grid=grid, in_specs=in_specs, out_specs=out_specs, out_shape=out_shape,
        compiler_params=_cparams(("parallel", "parallel")),
        name="qkv_proj",
    )(*args)


N_GZX = 2 * D_MODEL + CONV_DIM + DT_PAD


def _gzx_kernel(x_ref, shift_ref, scale_ref, ng_ref, w_ref, g_ref, z_ref, xbc_ref, dt_ref):
    hb = _mod_norm(x_ref[...], shift_ref[...], scale_ref[...], ng_ref[...]).astype(BF16)
    g_ref[...] = jnp.dot(hb, w_ref[:, 0:D_ATT], preferred_element_type=F32)
    z_ref[...] = jnp.dot(hb, w_ref[:, D_ATT:D_ATT + D_SSD], preferred_element_type=F32)
    o = D_ATT + D_SSD
    xbc_ref[...] = jnp.dot(hb, w_ref[:, o:o + CONV_DIM], preferred_element_type=F32)
    dt_ref[...] = jnp.dot(hb, w_ref[:, o + CONV_DIM:o + CONV_DIM + DT_PAD], preferred_element_type=F32)


def _gzx_proj(x, mod, mod_row, norm_g, w_gzx, tm):
    B, L, _ = x.shape
    row = mod_row
    in_specs = [
        pl.BlockSpec((None, tm, D_MODEL), lambda b, i: (b, i, 0)),
        pl.BlockSpec((None, 1, D_MODEL), lambda b, i: (row(b), 0, 0)),
        pl.BlockSpec((None, 1, D_MODEL), lambda b, i: (row(b), 0, 1)),
        pl.BlockSpec((1, D_MODEL), lambda b, i: (0, 0)),
        pl.BlockSpec((D_MODEL, N_GZX), lambda b, i: (0, 0), pipeline_mode=pl.Buffered(1)),
    ]
    widths = (D_ATT, D_SSD, CONV_DIM, DT_PAD)
    out_specs = [pl.BlockSpec((None, tm, w), lambda b, i: (b, i, 0)) for w in widths]
    out_shape = [jax.ShapeDtypeStruct((B, L, w), F32) for w in widths]
    return pl.pallas_call(
        _gzx_kernel, grid=(B, L // tm), in_specs=in_specs, out_specs=out_specs, out_shape=out_shape,
        compiler_params=_cparams(("parallel", "parallel")),
        name="gzx_proj",
    )(x, mod, mod, norm_g, w_gzx)


def _attn_kernel(*refs, n_src, lam_init):
    qt_ref = refs[0]
    srcs = [(refs[1 + 2 * i], refs[2 + 2 * i]) for i in range(n_src)]
    pos = 1 + 2 * n_src
    g_ref, sg_ref, lq1_ref, lk1_ref, lq2_ref, lk2_ref, o_ref = refs[pos:pos + 7]

    tq = qt_ref.shape[1]
    qt = qt_ref[...]
    row = lax.broadcasted_iota(jnp.int32, qt.shape, 0)
    zero = jnp.zeros_like(qt)
    q2t = jnp.concatenate([jnp.where(row < HD_QK, qt, zero), jnp.where(row >= HD_QK, qt, zero)], axis=1)

    m = jnp.full((1, 2 * tq), -jnp.inf, F32)
    l = jnp.zeros((1, 2 * tq), F32)
    acc = jnp.zeros((HD_V, 2 * tq), F32)
    blocks = []
    for k_ref, vt_ref in srcs:
        n_keys = k_ref.shape[0]
        tk = min(ATT_TK, n_keys)
        blocks += [(k_ref, vt_ref, j * tk, tk) for j in range(n_keys // tk)]

    def scores(blk):
        k_ref, _, off, tk = blk
        return jnp.dot(k_ref[off:off + tk, :], q2t, preferred_element_type=F32)

    s_next = scores(blocks[0])
    for idx, (_, vt_ref, off, tk) in enumerate(blocks):
        s = s_next
        if idx + 1 < len(blocks):
            s_next = scores(blocks[idx + 1])
        m_new = jnp.maximum(m, jnp.max(s, axis=0, keepdims=True))
        alpha = jnp.exp2(m - m_new)
        p = jnp.exp2(s - m_new)
        l = alpha * l + jnp.sum(p, axis=0, keepdims=True)
        acc = alpha * acc + jnp.dot(vt_ref[:, off:off + tk], p.astype(BF16),
                                    preferred_element_type=F32)
        m = m_new

    lam = (jnp.exp(jnp.sum(lq1_ref[...] * lk1_ref[...], axis=1, keepdims=True))
           - jnp.exp(jnp.sum(lq2_ref[...] * lk2_ref[...], axis=1, keepdims=True)) + lam_init)
    o_n = acc / l
    o = (o_n[:, :tq] - lam * o_n[:, tq:]).T
    ms = jnp.mean(o * o, axis=-1, keepdims=True)
    att = o * lax.rsqrt(ms + EPS) * sg_ref[...] * (1.0 - lam_init)
    o_ref[...] = (att * _silu(g_ref[...])).astype(BF16)


def _attention(qt, kvs, g, subln_g, lams, lam_init, tq):
    B, _, L = qt.shape
    in_specs = [pl.BlockSpec((None, HD_V, tq), lambda b, h, i: (b, h, i))]
    args = [qt]
    for k, vt in kvs:
        n_keys = k.shape[1]
        in_specs += [pl.BlockSpec((None, n_keys, HD_V), lambda b, h, i: (b, 0, h)),
                     pl.BlockSpec((None, HD_V, n_keys), lambda b, h, i: (b, h, 0))]
        args += [k, vt]
    in_specs += [pl.BlockSpec((None, tq, HD_V), lambda b, h, i: (b, i, h)),
                 pl.BlockSpec((1, HD_V), lambda b, h, i: (0, 0))]
    in_specs += [pl.BlockSpec((1, HD_QK), lambda b, h, i: (0, 0))] * 4
    args += [g, subln_g] + list(lams)
    return pl.pallas_call(
        functools.partial(_attn_kernel, n_src=len(kvs), lam_init=lam_init),
        grid=(B, H_ATT, L // tq),
        in_specs=in_specs,
        out_specs=pl.BlockSpec((None, tq, HD_V), lambda b, h, i: (b, i, h)),
        out_shape=jax.ShapeDtypeStruct((B, L, D_ATT), BF16),
        compiler_params=_cparams(("parallel", "parallel", "arbitrary")),
        name="diff_attention",
    )(*args)


def _ssd_kernel(*refs, seq_len, has_h0):
    (x_ref, bc_ref, dt_ref, cwx_ref, cwbc_ref, cbx_ref, cbbc_ref,
     dtb_ref, alog_ref, dsk_ref) = refs[:10]
    pos = 10
    if has_h0:
        h0_ref = refs[pos]
        pos += 1
    y_ref, hfin_ref, xc_sc, bcc_sc = refs[pos:pos + 4]

    nc = seq_len // CHUNK
    grp = pl.program_id(1)
    row_i = lax.broadcasted_iota(jnp.int32, (CHUNK, CHUNK), 0)
    col_i = lax.broadcasted_iota(jnp.int32, (CHUNK, CHUNK), 1)
    row1 = lax.broadcasted_iota(jnp.int32, (CHUNK, 1), 0)

    def conv_chunk(c, carry):
        r0 = pl.multiple_of(c * CHUNK, CHUNK)
        rp = pl.multiple_of(jnp.maximum(r0 - 8, 0), 8)
        rn = pl.multiple_of(jnp.minimum(r0 + CHUNK, seq_len - 8), 8)
        for src, w_ref, b_ref, dst in ((x_ref, cwx_ref, cbx_ref, xc_sc),
                                       (bc_ref, cwbc_ref, cbbc_ref, bcc_sc)):
            u = src[pl.ds(r0, CHUNK), :]
            pr = jnp.where(c > 0, src[pl.ds(rp, 8), :][7:8, :], 0.0)
            nx = jnp.where(c < nc - 1, src[pl.ds(rn, 8), :][0:1, :], 0.0)
            prev = jnp.where(row1 == 0, pr, pltpu.roll(u, 1, 0))
            nxt = jnp.where(row1 == CHUNK - 1, nx, pltpu.roll(u, CHUNK - 1, 0))
            w = w_ref[...]
            o = b_ref[...] + prev * w[0:1, :] + u * w[1:2, :] + nxt * w[2:3, :]
            dst[pl.ds(r0, CHUNK), :] = _silu(o)
        return carry

    lax.fori_loop(0, nc, conv_chunk, 0)

    dtb = dtb_ref[...]
    a_neg = -jnp.exp(alog_ref[...])
    dsk = dsk_ref[...]
    lane_shift = (LANES - grp * 2 * HEADS_PER_GROUP) % LANES

    def scan_dir(d):
        mask = (col_i <= row_i) if d == 0 else (col_i >= row_i)
        tmat = mask.astype(F32)
        end_row = CHUNK - 1 if d == 0 else 0
        if has_h0:
            s0 = h0_ref[d]
        else:
            s0 = jnp.zeros((D_STATE, GROUP_W), F32)

        def chunk(ci, state):
            c = ci if d == 0 else nc - 1 - ci
            r0 = pl.multiple_of(c * CHUNK, CHUNK)
            xc = xc_sc[pl.ds(r0, CHUNK), :]
            bc = bcc_sc[pl.ds(r0, CHUNK), :]
            dtl = pltpu.roll(dt_ref[pl.ds(r0, CHUNK), :], lane_shift, 1)
            sp = _softplus(dtl + dtb)
            cum = jnp.dot(tmat, sp * a_neg, precision=lax.Precision.HIGHEST,
                          preferred_element_type=F32)
            cum_t = cum.T
            bcb = bc.astype(BF16)
            b_m = bcb[:, :D_STATE]
            c_m = bcb[:, D_STATE:]
            b_t = bc.T[:D_STATE, :].astype(BF16)
            cb = lax.dot_general(c_m, b_m, (((1,), (1,)), ((), ())), preferred_element_type=F32)
            ys, sts = [], []
            for hh in range(HEADS_PER_GROUP):
                j = d * HEADS_PER_GROUP + hh
                aq = cum[:, j:j + 1]
                seg = aq - cum_t[j:j + 1, :]
                lmat = jnp.exp(jnp.where(mask, seg, -jnp.inf))
                xh = xc[:, hh * SSD_HEADDIM:(hh + 1) * SSD_HEADDIM]
                xdt = xh * sp[:, j:j + 1]
                y_diag = jnp.dot((cb * lmat).astype(BF16), xdt.astype(BF16), preferred_element_type=F32)
                sh = state[:, hh * SSD_HEADDIM:(hh + 1) * SSD_HEADDIM]
                y_off = jnp.dot(c_m, sh.astype(BF16), preferred_element_type=F32) * jnp.exp(aq)
                a_end = cum[end_row:end_row + 1, j:j + 1]
                st = jnp.dot(b_t, (xdt * jnp.exp(a_end - aq)).astype(BF16), preferred_element_type=F32)
                ys.append(y_diag + y_off)
                sts.append(sh * jnp.exp(a_end) + st)
            y_chunk = jnp.concatenate(ys, axis=1)
            if d == 0:
                y_ref[pl.ds(r0, CHUNK), :] = y_chunk + dsk * xc
            else:
                y_ref[pl.ds(r0, CHUNK), :] = y_ref[pl.ds(r0, CHUNK), :] + y_chunk
            return jnp.concatenate(sts, axis=1)

        hfin_ref[d] = lax.fori_loop(0, nc, chunk, s0)

    scan_dir(0)
    scan_dir(1)


def _ssd(xbc, dt, conv_w, conv_b, dtb_g, alog_g, dsk, h0):
    B, L, _ = xbc.shape
    xblk = D_SSD // GROUP_W
    in_specs = [
        pl.BlockSpec((None, L, GROUP_W), lambda b, g: (b, 0, g)),
        pl.BlockSpec((None, L, LANES), lambda b, g: (b, 0, D_SSD // LANES + g)),
        pl.BlockSpec((None, L, DT_PAD), lambda b, g: (b, 0, 0)),
        pl.BlockSpec((3, GROUP_W), lambda b, g: (0, g)),
        pl.BlockSpec((3, LANES), lambda b, g: (0, D_SSD // LANES + g)),
        pl.BlockSpec((1, GROUP_W), lambda b, g: (0, g)),
        pl.BlockSpec((1, LANES), lambda b, g: (0, D_SSD // LANES + g)),
        pl.BlockSpec((None, 1, LANES), lambda b, g: (g, 0, 0)),
        pl.BlockSpec((None, 1, LANES), lambda b, g: (g, 0, 0)),
        pl.BlockSpec((1, GROUP_W), lambda b, g: (0, g)),
    ]
    del xblk
    args = [xbc, xbc, dt, conv_w, conv_w, conv_b, conv_b, dtb_g, alog_g, dsk]
    if h0 is not None:
        in_specs.append(pl.BlockSpec((None, 2, None, D_STATE, GROUP_W), lambda b, g: (b, 0, g, 0, 0)))
        args.append(h0)
    return pl.pallas_call(
        functools.partial(_ssd_kernel, seq_len=L, has_h0=h0 is not None),
        grid=(B, SSD_GROUPS),
        in_specs=in_specs,
        out_specs=[pl.BlockSpec((None, L, GROUP_W), lambda b, g: (b, 0, g)),
                   pl.BlockSpec((None, 2, None, D_STATE, GROUP_W), lambda b, g: (b, 0, g, 0, 0))],
        out_shape=[jax.ShapeDtypeStruct((B, L, D_SSD), F32),
                   jax.ShapeDtypeStruct((B, 2, SSD_GROUPS, D_STATE, GROUP_W), F32)],
        scratch_shapes=[pltpu.VMEM((L, GROUP_W), F32), pltpu.VMEM((L, LANES), F32)],
        compiler_params=_cparams(("parallel", "parallel")),
        name="ssd_scan",
    )(*args)


def _out_kernel(x_ref, att_ref, y_ref, z_ref, sng_ref, w_ref, gate_ref, fg_ref, o_ref):
    yz = y_ref[...] * _silu(z_ref[...])
    ms = jnp.mean(yz * yz, axis=-1, keepdims=True)
    ssd_y = (yz * lax.rsqrt(ms + EPS) * sng_ref[...]).astype(BF16)
    out = jnp.dot(att_ref[...], w_ref[0:D_ATT, :], preferred_element_type=F32)
    out = out + jnp.dot(ssd_y, w_ref[D_ATT:D_ATT + D_SSD, :], preferred_element_type=F32)
    r = x_ref[...] + gate_ref[...] * out
    ms2 = jnp.mean(r * r, axis=-1, keepdims=True)
    o_ref[...] = r * lax.rsqrt(ms2 + EPS) * fg_ref[...]


def _out_proj(x, att, y, z, ssd_norm_g, w_out, mod, mod_row, final_g, tm):
    B, L, _ = x.shape
    row = mod_row
    tok = lambda b, i: (b, i, 0)
    in_specs = [
        pl.BlockSpec((None, tm, D_MODEL), tok),
        pl.BlockSpec((None, tm, D_ATT), tok),
        pl.BlockSpec((None, tm, D_SSD), tok),
        pl.BlockSpec((None, tm, D_SSD), tok),
        pl.BlockSpec((1, D_SSD), lambda b, i: (0, 0)),
        pl.BlockSpec((D_ATT + D_SSD, D_MODEL), lambda b, i: (0, 0), pipeline_mode=pl.Buffered(1)),
        pl.BlockSpec((None, 1, D_MODEL), lambda b, i: (row(b), 0, 2)),
        pl.BlockSpec((1, D_MODEL), lambda b, i: (0, 0)),
    ]
    return pl.pallas_call(
        _out_kernel, grid=(B, L // tm), in_specs=in_specs,
        out_specs=pl.BlockSpec((None, tm, D_MODEL), tok),
        out_shape=jax.ShapeDtypeStruct((B, L, D_MODEL), F32),
        compiler_params=_cparams(("parallel", "parallel")),
        name="out_proj",
    )(x, att, y, z, ssd_norm_g, w_out, mod, final_g)


def _rope_tables(L):
    rows = L // GRID_W
    row_ids = jnp.repeat(jnp.arange(rows), GRID_W).astype(F32)
    col_ids = jnp.tile(jnp.arange(GRID_W), rows).astype(F32)
    inv = ROPE_BASE ** (-jnp.arange(ROPE_FREQS, dtype=F32) / ROPE_FREQS)
    ang_r = row_ids[:, None] * inv
    ang_c = col_ids[:, None] * inv
    cr, sr, cc, sc = jnp.cos(ang_r), jnp.sin(ang_r), jnp.cos(ang_c), jnp.sin(ang_c)
    cos64 = jnp.concatenate([cr, cr, cc, cc], axis=1)
    sin64 = jnp.concatenate([-sr, sr, -sc, sc], axis=1)
    return jnp.tile(cos64, (1, 2)), jnp.tile(sin64, (1, 2))


def _bc_perm():
    idx = list(range(D_SSD))
    for g in range(SSD_GROUPS):
        idx += [D_SSD + g * D_STATE + n for n in range(D_STATE)]
        idx += [D_SSD + SSD_GROUPS * D_STATE + g * D_STATE + n for n in range(D_STATE)]
    return jnp.asarray(idx, dtype=jnp.int32)


def _dt_perm():
    idx = []
    for g in range(SSD_GROUPS):
        for d in range(2):
            for hh in range(HEADS_PER_GROUP):
                idx.append(d * H_SSD + g * HEADS_PER_GROUP + hh)
    return jnp.asarray(idx, dtype=jnp.int32)


def _group_lanes(v2h):
    t = v2h.reshape(2, SSD_GROUPS, HEADS_PER_GROUP).transpose(1, 0, 2).reshape(SSD_GROUPS, 2 * HEADS_PER_GROUP)
    t = jnp.pad(t, ((0, 0), (0, LANES - 2 * HEADS_PER_GROUP)))
    return t.reshape(SSD_GROUPS, 1, LANES)


def _layer(x, mod, mod_row, params, rope_tabs, past_kv, h0, emit_f32, lam_init, final_g, tm, tq):
    (norm_g, w_qkv, w_gzx, lams, subln_g, conv_w, conv_b, dtb_g, alog_g, dsk, ssd_norm_g, w_out) = params
    proj = _qkv_proj(x, mod, mod_row, norm_g, w_qkv, rope_tabs, emit_f32, tm)
    qt, k, vt = proj[:3]
    g, z, xbc, dt = _gzx_proj(x, mod, mod_row, norm_g, w_gzx, tm)
    kvs = [(k, vt)]
    if past_kv is not None:
        kvs.append(past_kv)
    att = _attention(qt, kvs, g, subln_g, lams, lam_init, tq)
    y, hfin = _ssd(xbc, dt, conv_w, conv_b, dtb_g, alog_g, dsk, h0)
    out = _out_proj(x, att, y, z, ssd_norm_g, w_out, mod, mod_row, final_g, tm)
    return out, proj[3:], hfin


def kernel(x_prompt, x_sample, cache_k, cache_v, state_ssd, c, c_ctx, w_mod, b_mod, norm_g, w_in,
           lambda_q1, lambda_k1, lambda_q2, lambda_k2, subln_g, conv_w, conv_b, dt_bias, A_log,
           D_skip, ssd_norm_g, w_out, final_g):
    b_ctx, l_ctx, _ = x_prompt.shape
    b_dec, l_dec, _ = x_sample.shape
    l_past = cache_k.shape[2]
    depth = w_mod.shape[0]
    assert depth == 1
    lam_init = 0.8 - 0.6 * math.exp(-0.3 * 0)

    w = w_in[0]
    w_qkv = w[:, :3 * D_ATT].astype(BF16)
    o = 3 * D_ATT
    w_g = w[:, o:o + D_ATT]
    w_z = w[:, o + D_ATT:o + D_ATT + D_SSD]
    o2 = o + D_ATT + D_SSD
    bc_perm = _bc_perm()
    w_xbc = w[:, o2:o2 + CONV_DIM][:, bc_perm]
    w_dt = jnp.pad(w[:, o2 + CONV_DIM:][:, _dt_perm()], ((0, 0), (0, DT_PAD - 2 * H_SSD)))
    w_gzx = jnp.concatenate([w_g, w_z, w_xbc, w_dt], axis=1).astype(BF16)
    conv_w_p = conv_w[0][:, bc_perm]
    conv_b_p = conv_b[0][bc_perm].reshape(1, CONV_DIM)
    dtb_g = _group_lanes(dt_bias[0])
    alog_g = _group_lanes(A_log[0])
    dsk = jnp.repeat(D_skip[0], SSD_HEADDIM).reshape(1, D_SSD)
    lams = [a[0].reshape(1, HD_QK) for a in (lambda_q1, lambda_k1, lambda_q2, lambda_k2)]
    params = (norm_g[0].reshape(1, D_MODEL), w_qkv, w_gzx, lams, subln_g[0].reshape(1, HD_V),
              conv_w_p, conv_b_p, dtb_g, alog_g, dsk, ssd_norm_g[0].reshape(1, D_SSD),
              w_out[0].astype(BF16))
    fg = final_g.reshape(1, D_MODEL)

    cond8 = jnp.zeros((8, D_MODEL), F32).at[:b_dec].set(c).at[b_dec].set(c_ctx)
    mod = _adaln(cond8, w_mod[0], b_mod[0]).reshape(8, 1, 3 * D_MODEL)

    y_prompt, (k_ctx, v_ctx), h_ctx = _layer(
        x_prompt, mod, lambda b: b_dec, params, None, None, None, True, lam_init, fg,
        tm=256, tq=256)

    rope_tabs = _rope_tables(l_dec)
    k_past = cache_k[:, 0].reshape(b_dec, l_past, D_ATT).astype(BF16)
    vt_past = cache_v[:, 0].reshape(b_dec, l_past, D_ATT).transpose(0, 2, 1).astype(BF16)
    h0 = state_ssd[:, 0].reshape(b_dec, 2, SSD_GROUPS, HEADS_PER_GROUP, SSD_HEADDIM, D_STATE)
    h0 = h0.transpose(0, 1, 2, 5, 3, 4).reshape(b_dec, 2, SSD_GROUPS, D_STATE, GROUP_W)
    y_sample, _, _ = _layer(
        x_sample, mod, lambda b: b, params, rope_tabs, (k_past, vt_past), h0, False, lam_init, fg,
        tm=512, tq=256)

    new_cache_k = k_ctx.reshape(b_ctx, 1, l_ctx, H_ATT, 2 * HD_QK)
    new_cache_v = v_ctx.reshape(b_ctx, 1, l_ctx, H_ATT, HD_V)
    hs = h_ctx.reshape(b_ctx, 2, SSD_GROUPS, D_STATE, HEADS_PER_GROUP, SSD_HEADDIM)
    new_state = hs.transpose(0, 1, 2, 4, 5, 3).reshape(b_ctx, 1, 2, H_SSD, SSD_HEADDIM, D_STATE)
    return (y_prompt, y_sample, new_cache_k, new_cache_v, new_state)
```

```python
import functools
import math

import jax
import jax.numpy as jnp
from jax import lax
from jax.experimental import pallas as pl
from jax.experimental.pallas import tpu as pltpu

F32 = jnp.float32
BF16 = jnp.bfloat16

D_MODEL = 1024
GRID_W = 64
HD_QK = 64
HD_V = 128
H_ATT = 8
D_ATT = 1024
D_SSD = 1024
SSD_HEADDIM = 64
H_SSD = 16
SSD_GROUPS = 4
HEADS_PER_GROUP = 4
D_STATE = 64
CONV_DIM = D_SSD + 2 * SSD_GROUPS * D_STATE
CHUNK = 128
ROPE_BASE = 10000.0
ROPE_FREQS = 16
EPS = 1e-6

LANES = 128
DT_PAD = LANES
GROUP_W = HEADS_PER_GROUP * SSD_HEADDIM
VMEM_LIMIT = 56 * 1024 * 1024
ATT_TK = 1024
LOG2E = 1.4426950408889634


def _cparams(sem):
    return pltpu.CompilerParams(dimension_semantics=sem, vmem_limit_bytes=VMEM_LIMIT)


def _silu(x):
    return x * jax.nn.sigmoid(x)


def _softplus(x):
    return jnp.maximum(x, 0.0) + jnp.log1p(jnp.exp(-jnp.abs(x)))


def _adaln_kernel(cond_ref, w_ref, b_ref, o_ref):
    cond = cond_ref[...]
    s = _silu(cond).astype(BF16)
    o_ref[...] = jnp.dot(s, w_ref[...].astype(BF16), preferred_element_type=F32) + b_ref[...]


def _adaln(cond8, w_mod, b_mod):
    n = w_mod.shape[1]
    blk = 1024
    return pl.pallas_call(
        _adaln_kernel,
        grid=(n // blk,),
        in_specs=[pl.BlockSpec((8, D_MODEL), lambda j: (0, 0)),
                  pl.BlockSpec((D_MODEL, blk), lambda j: (0, j)),
                  pl.BlockSpec((1, blk), lambda j: (0, j))],
        out_specs=pl.BlockSpec((8, blk), lambda j: (0, j)),
        out_shape=jax.ShapeDtypeStruct((8, n), F32),
        compiler_params=_cparams(("arbitrary",)),
        name="adaln",
    )(cond8, w_mod, b_mod.reshape(1, n))


def _mod_norm(x, shift, scale, g):
    ms = jnp.mean(x * x, axis=-1, keepdims=True)
    y = x * lax.rsqrt(ms + EPS) * g
    return y * (1.0 + scale) + shift


def _rope(t, cos, sin_signed, first):
    outs = []
    for h in range(H_ATT):
        th = t[:, h * LANES:(h + 1) * LANES]
        partner = jnp.where(first, pltpu.roll(th, LANES - 16, 1), pltpu.roll(th, 16, 1))
        outs.append(th * cos + partner * sin_signed)
    return jnp.concatenate(outs, axis=1)


def _qkv_kernel(*refs, rope, emit_f32):
    x_ref, shift_ref, scale_ref, ng_ref, w_ref = refs[:5]
    pos = 5
    if rope:
        cos_ref, sin_ref = refs[pos:pos + 2]
        pos += 2
    qt_ref, k_ref, vt_ref = refs[pos:pos + 3]
    pos += 3
    if emit_f32:
        k32_ref, v32_ref = refs[pos:pos + 2]

    hb = _mod_norm(x_ref[...], shift_ref[...], scale_ref[...], ng_ref[...]).astype(BF16)
    q = jnp.dot(hb, w_ref[:, 0:D_ATT], preferred_element_type=F32)
    k = jnp.dot(hb, w_ref[:, D_ATT:2 * D_ATT], preferred_element_type=F32)
    v = jnp.dot(hb, w_ref[:, 2 * D_ATT:3 * D_ATT], preferred_element_type=F32)
    if rope:
        cos = cos_ref[...]
        sin_signed = sin_ref[...]
        lane = lax.broadcasted_iota(jnp.int32, cos.shape, 1)
        first = (lane % 32) < 16
        q = _rope(q, cos, sin_signed, first)
        k = _rope(k, cos, sin_signed, first)
    qt_ref[...] = (q * (LOG2E / math.sqrt(HD_QK))).T.astype(BF16)
    k_ref[...] = k.astype(BF16)
    vt_ref[...] = v.T.astype(BF16)
    if emit_f32:
        k32_ref[...] = k
        v32_ref[...] = v


def _qkv_proj(x, mod, mod_row, norm_g, w_qkv, rope_tabs, emit_f32, tm):
    B, L, _ = x.shape
    grid = (B, L // tm)
    row = mod_row

    in_specs = [
        pl.BlockSpec((None, tm, D_MODEL), lambda b, i: (b, i, 0)),
        pl.BlockSpec((None, 1, D_MODEL), lambda b, i: (row(b), 0, 0)),
        pl.BlockSpec((None, 1, D_MODEL), lambda b, i: (row(b), 0, 1)),
        pl.BlockSpec((1, D_MODEL), lambda b, i: (0, 0)),
        pl.BlockSpec((D_MODEL, 3 * D_ATT), lambda b, i: (0, 0), pipeline_mode=pl.Buffered(1)),
    ]
    args = [x, mod, mod, norm_g, w_qkv]
    if rope_tabs is not None:
        in_specs += [pl.BlockSpec((tm, LANES), lambda b, i: (i, 0)),
                     pl.BlockSpec((tm, LANES), lambda b, i: (i, 0))]
        args += list(rope_tabs)
    out_specs = [
        pl.BlockSpec((None, D_ATT, tm), lambda b, i: (b, 0, i)),
        pl.BlockSpec((None, tm, D_ATT), lambda b, i: (b, i, 0)),
        pl.BlockSpec((None, D_ATT, tm), lambda b, i: (b, 0, i)),
    ]
    out_shape = [
        jax.ShapeDtypeStruct((B, D_ATT, L), BF16),
        jax.ShapeDtypeStruct((B, L, D_ATT), BF16),
        jax.ShapeDtypeStruct((B, D_ATT, L), BF16),
    ]
    if emit_f32:
        out_specs += [pl.BlockSpec((None, tm, D_ATT), lambda b, i: (b, i, 0))] * 2
        out_shape += [jax.ShapeDtypeStruct((B, L, D_ATT), F32)] * 2
    return pl.pallas_call(
        functools.partial(_qkv_kernel, rope=rope_tabs is not None, emit_f32=emit_f32),
        grid=grid, in_specs=in_specs, out_specs=out_specs, out_shape=out_shape,
        compiler_params=_cparams(("parallel", "parallel")),
        name="qkv_proj",
    )(*args)


N_GZX = 2 * D_MODEL + CONV_DIM + DT_PAD


def _gzx_kernel(x_ref, shift_ref, scale_ref, ng_ref, w_ref, g_ref, z_ref, xbc_ref, dtt_ref):
    hb = _mod_norm(x_ref[...], shift_ref[...], scale_ref[...], ng_ref[...]).astype(BF16)
    g_ref[...] = jnp.dot(hb, w_ref[:, 0:D_ATT], preferred_element_type=F32)
    z_ref[...] = jnp.dot(hb, w_ref[:, D_ATT:D_ATT + D_SSD], preferred_element_type=F32)
    o = D_ATT + D_SSD
    xbc_ref[...] = jnp.dot(hb, w_ref[:, o:o + CONV_DIM], preferred_element_type=F32)
    dt = jnp.dot(hb, w_ref[:, o + CONV_DIM:o + CONV_DIM + DT_PAD], preferred_element_type=F32)
    dtt_ref[...] = dt.T[:2 * H_SSD, :]


def _gzx_proj(x, mod, mod_row, norm_g, w_gzx, tm):
    B, L, _ = x.shape
    row = mod_row
    in_specs = [
        pl.BlockSpec((None, tm, D_MODEL), lambda b, i: (b, i, 0)),
        pl.BlockSpec((None, 1, D_MODEL), lambda b, i: (row(b), 0, 0)),
        pl.BlockSpec((None, 1, D_MODEL), lambda b, i: (row(b), 0, 1)),
        pl.BlockSpec((1, D_MODEL), lambda b, i: (0, 0)),
        pl.BlockSpec((D_MODEL, N_GZX), lambda b, i: (0, 0), pipeline_mode=pl.Buffered(1)),
    ]
    widths = (D_ATT, D_SSD, CONV_DIM)
    out_specs = [pl.BlockSpec((None, tm, w), lambda b, i: (b, i, 0)) for w in widths]
    out_shape = [jax.ShapeDtypeStruct((B, L, w), F32) for w in widths]
    out_specs.append(pl.BlockSpec((None, 2 * H_SSD, tm), lambda b, i: (b, 0, i)))
    out_shape.append(jax.ShapeDtypeStruct((B, 2 * H_SSD, L), F32))
    return pl.pallas_call(
        _gzx_kernel, grid=(B, L // tm), in_specs=in_specs, out_specs=out_specs, out_shape=out_shape,
        compiler_params=_cparams(("parallel", "parallel")),
        name="gzx_proj",
    )(x, mod, mod, norm_g, w_gzx)


def _attn_kernel(*refs, n_src, lam_init):
    qt_ref = refs[0]
    srcs = [(refs[1 + 2 * i], refs[2 + 2 * i]) for i in range(n_src)]
    pos = 1 + 2 * n_src
    g_ref, sg_ref, lq1_ref, lk1_ref, lq2_ref, lk2_ref, o_ref = refs[pos:pos + 7]

    tq = qt_ref.shape[1]
    qt = qt_ref[...]
    row = lax.broadcasted_iota(jnp.int32, qt.shape, 0)
    zero = jnp.zeros_like(qt)
    q2t = jnp.concatenate([jnp.where(row < HD_QK, qt, zero), jnp.where(row >= HD_QK, qt, zero)], axis=1)

    m = jnp.full((1, 2 * tq), -jnp.inf, F32)
    l = jnp.zeros((1, 2 * tq), F32)
    acc = jnp.zeros((HD_V, 2 * tq), F32)
    blocks = []
    for k_ref, vt_ref in srcs:
        n_keys = k_ref.shape[0]
        tk = min(ATT_TK, n_keys)
        blocks += [(k_ref, vt_ref, j * tk, tk) for j in range(n_keys // tk)]

    def scores(blk):
        k_ref, _, off, tk = blk
        return jnp.dot(k_ref[off:off + tk, :], q2t, preferred_element_type=F32)

    s_next = scores(blocks[0])
    for idx, (_, vt_ref, off, tk) in enumerate(blocks):
        s = s_next
        if idx + 1 < len(blocks):
            s_next = scores(blocks[idx + 1])
        m_new = jnp.maximum(m, jnp.max(s, axis=0, keepdims=True))
        alpha = jnp.exp2(m - m_new)
        p = jnp.exp2(s - m_new)
        l = alpha * l + jnp.sum(p, axis=0, keepdims=True)
        acc = alpha * acc + jnp.dot(vt_ref[:, off:off + tk], p.astype(BF16),
                                    preferred_element_type=F32)
        m = m_new

    lam = (jnp.exp(jnp.sum(lq1_ref[...] * lk1_ref[...], axis=1, keepdims=True))
           - jnp.exp(jnp.sum(lq2_ref[...] * lk2_ref[...], axis=1, keepdims=True)) + lam_init)
    o_n = acc / l
    o = (o_n[:, :tq] - lam * o_n[:, tq:]).T
    ms = jnp.mean(o * o, axis=-1, keepdims=True)
    att = o * lax.rsqrt(ms + EPS) * sg_ref[...] * (1.0 - lam_init)
    o_ref[...] = (att * _silu(g_ref[...])).astype(BF16)


def _attention(qt, kvs, g, subln_g, lams, lam_init, tq):
    B, _, L = qt.shape
    in_specs = [pl.BlockSpec((None, HD_V, tq), lambda b, h, i: (b, h, i))]
    args = [qt]
    for k, vt in kvs:
        n_keys = k.shape[1]
        in_specs += [pl.BlockSpec((None, n_keys, HD_V), lambda b, h, i: (b, 0, h)),
                     pl.BlockSpec((None, HD_V, n_keys), lambda b, h, i: (b, h, 0))]
        args += [k, vt]
    in_specs += [pl.BlockSpec((None, tq, HD_V), lambda b, h, i: (b, i, h)),
                 pl.BlockSpec((1, HD_V), lambda b, h, i: (0, 0))]
    in_specs += [pl.BlockSpec((1, HD_QK), lambda b, h, i: (0, 0))] * 4
    args += [g, subln_g] + list(lams)
    return pl.pallas_call(
        functools.partial(_attn_kernel, n_src=len(kvs), lam_init=lam_init),
        grid=(B, H_ATT, L // tq),
        in_specs=in_specs,
        out_specs=pl.BlockSpec((None, tq, HD_V), lambda b, h, i: (b, i, h)),
        out_shape=jax.ShapeDtypeStruct((B, L, D_ATT), BF16),
        compiler_params=_cparams(("parallel", "parallel", "arbitrary")),
        name="diff_attention",
    )(*args)


def _ssd_kernel(*refs, seq_len, has_h0):
    (x_ref, bc_ref, dtt_ref, cwx_ref, cwbc_ref, cbx_ref, cbbc_ref,
     dtb_ref, alog_ref, dsk_ref) = refs[:10]
    pos = 10
    if has_h0:
        h0_ref = refs[pos]
        pos += 1
    y_ref, hfin_ref, xt_sc, yt_sc, b_sc, ct_sc = refs[pos:pos + 6]

    nc = seq_len // CHUNK
    row_i = lax.broadcasted_iota(jnp.int32, (CHUNK, CHUNK), 0)
    col_i = lax.broadcasted_iota(jnp.int32, (CHUNK, CHUNK), 1)
    row1 = lax.broadcasted_iota(jnp.int32, (CHUNK, 1), 0)

    def conv_silu(src, w_ref, b_ref, c):
        r0 = pl.multiple_of(c * CHUNK, CHUNK)
        rp = pl.multiple_of(jnp.maximum(r0 - 8, 0), 8)
        rn = pl.multiple_of(jnp.minimum(r0 + CHUNK, seq_len - 8), 8)
        u = src[pl.ds(r0, CHUNK), :]
        pr = jnp.where(c > 0, src[pl.ds(rp, 8), :][7:8, :], 0.0)
        nx = jnp.where(c < nc - 1, src[pl.ds(rn, 8), :][0:1, :], 0.0)
        prev = jnp.where(row1 == 0, pr, pltpu.roll(u, 1, 0))
        nxt = jnp.where(row1 == CHUNK - 1, nx, pltpu.roll(u, CHUNK - 1, 0))
        w = w_ref[...]
        return _silu(b_ref[...] + prev * w[0:1, :] + u * w[1:2, :] + nxt * w[2:3, :])

    def conv_chunk(c, carry):
        xt = conv_silu(x_ref, cwx_ref, cbx_ref, c).T
        xt_sc[c] = xt
        yt_sc[c] = dsk_ref[...] * xt
        bc = conv_silu(bc_ref, cwbc_ref, cbbc_ref, c)
        b_sc[c] = bc[:, :D_STATE].astype(BF16)
        ct_sc[c] = bc.T[D_STATE:, :].astype(BF16)
        return carry

    lax.fori_loop(0, nc, conv_chunk, 0)

    dtb = dtb_ref[...]
    a_neg = -jnp.exp(alog_ref[...])

    def scan_chunk(d, c, state):
        mask = (row_i <= col_i) if d == 0 else (row_i >= col_i)
        end = CHUNK - 1 if d == 0 else 0
        t0 = pl.multiple_of(c * CHUNK, CHUNK)
        sp = _softplus(dtt_ref[:, pl.ds(t0, CHUNK)] + dtb)
        cum = jnp.dot(sp * a_neg, mask.astype(F32), precision=lax.Precision.HIGHEST,
                      preferred_element_type=F32)
        cum_col = cum.T
        xt = xt_sc[c]
        b_m = b_sc[c]
        c_t = ct_sc[c]
        cb_t = jnp.dot(b_m, c_t, preferred_element_type=F32)
        y_diag, x_dec, y_scale, s_scale = [], [], [], []
        for hh in range(HEADS_PER_GROUP):
            r = d * HEADS_PER_GROUP + hh
            a_row = cum[r:r + 1, :]
            seg = a_row - cum_col[:, r:r + 1]
            lmat = jnp.exp(jnp.where(mask, seg, -jnp.inf))
            xdt = xt[hh * SSD_HEADDIM:(hh + 1) * SSD_HEADDIM, :] * sp[r:r + 1, :]
            y_diag.append(jnp.dot(xdt.astype(BF16), (cb_t * lmat).astype(BF16),
                                  preferred_element_type=F32))
            a_end = a_row[:, end:end + 1]
            x_dec.append((xdt * jnp.exp(a_end - a_row)).astype(BF16))
            y_scale.append(jnp.broadcast_to(jnp.exp(a_row), (SSD_HEADDIM, CHUNK)))
            s_scale.append(jnp.broadcast_to(jnp.exp(a_end), (SSD_HEADDIM, D_STATE)))
        y_off = jnp.dot(state.astype(BF16), c_t, preferred_element_type=F32)
        yt_sc[c] = yt_sc[c] + jnp.concatenate(y_diag, axis=0) + y_off * jnp.concatenate(y_scale, axis=0)
        st = jnp.dot(jnp.concatenate(x_dec, axis=0), b_m, preferred_element_type=F32)
        return state * jnp.concatenate(s_scale, axis=0) + st

    if has_h0:
        s_init = (h0_ref[0], h0_ref[1])
    else:
        s_init = (jnp.zeros((GROUP_W, D_STATE), F32),) * 2

    def scan_step(i, states):
        return scan_chunk(0, i, states[0]), scan_chunk(1, nc - 1 - i, states[1])

    s_fwd, s_bwd = lax.fori_loop(0, nc, scan_step, s_init)
    hfin_ref[0] = s_fwd
    hfin_ref[1] = s_bwd

    def emit_chunk(c, carry):
        y_ref[pl.ds(pl.multiple_of(c * CHUNK, CHUNK), CHUNK), :] = yt_sc[c].T
        return carry

    lax.fori_loop(0, nc, emit_chunk, 0)


def _ssd(xbc, dtt, conv_w, conv_b, dtb_g, alog_g, dsk, h0):
    B, L, _ = xbc.shape
    nc = L // CHUNK
    bc_blk = D_SSD // LANES
    dir_heads = 2 * HEADS_PER_GROUP
    state_spec = pl.BlockSpec((None, 2, None, GROUP_W, D_STATE), lambda b, g: (b, 0, g, 0, 0))
    in_specs = [
        pl.BlockSpec((None, L, GROUP_W), lambda b, g: (b, 0, g)),
        pl.BlockSpec((None, L, LANES), lambda b, g: (b, 0, bc_blk + g)),
        pl.BlockSpec((None, dir_heads, L), lambda b, g: (b, g, 0)),
        pl.BlockSpec((3, GROUP_W), lambda b, g: (0, g)),
        pl.BlockSpec((3, LANES), lambda b, g: (0, bc_blk + g)),
        pl.BlockSpec((1, GROUP_W), lambda b, g: (0, g)),
        pl.BlockSpec((1, LANES), lambda b, g: (0, bc_blk + g)),
        pl.BlockSpec((None, dir_heads, LANES), lambda b, g: (g, 0, 0)),
        pl.BlockSpec((None, dir_heads, LANES), lambda b, g: (g, 0, 0)),
        pl.BlockSpec((GROUP_W, LANES), lambda b, g: (g, 0)),
    ]
    args = [xbc, xbc, dtt, conv_w, conv_w, conv_b, conv_b, dtb_g, alog_g, dsk]
    if h0 is not None:
        in_specs.append(state_spec)
        args.append(h0)
    return pl.pallas_call(
        functools.partial(_ssd_kernel, seq_len=L, has_h0=h0 is not None),
        grid=(B, SSD_GROUPS),
        in_specs=in_specs,
        out_specs=[pl.BlockSpec((None, L, GROUP_W), lambda b, g: (b, 0, g)), state_spec],
        out_shape=[jax.ShapeDtypeStruct((B, L, D_SSD), F32),
                   jax.ShapeDtypeStruct((B, 2, SSD_GROUPS, GROUP_W, D_STATE), F32)],
        scratch_shapes=[pltpu.VMEM((nc, GROUP_W, CHUNK), F32), pltpu.VMEM((nc, GROUP_W, CHUNK), F32),
                        pltpu.VMEM((nc, CHUNK, D_STATE), BF16), pltpu.VMEM((nc, D_STATE, CHUNK), BF16)],
        compiler_params=_cparams(("parallel", "parallel")),
        name="ssd_scan",
    )(*args)


def _out_kernel(x_ref, att_ref, y_ref, z_ref, sng_ref, w_ref, gate_ref, fg_ref, o_ref):
    yz = y_ref[...] * _silu(z_ref[...])
    ms = jnp.mean(yz * yz, axis=-1, keepdims=True)
    ssd_y = (yz * lax.rsqrt(ms + EPS) * sng_ref[...]).astype(BF16)
    out = jnp.dot(att_ref[...], w_ref[0:D_ATT, :], preferred_element_type=F32)
    out = out + jnp.dot(ssd_y, w_ref[D_ATT:D_ATT + D_SSD, :], preferred_element_type=F32)
    r = x_ref[...] + gate_ref[...] * out
    ms2 = jnp.mean(r * r, axis=-1, keepdims=True)
    o_ref[...] = r * lax.rsqrt(ms2 + EPS) * fg_ref[...]


def _out_proj(x, att, y, z, ssd_norm_g, w_out, mod, mod_row, final_g, tm):
    B, L, _ = x.shape
    row = mod_row
    tok = lambda b, i: (b, i, 0)
    in_specs = [
        pl.BlockSpec((None, tm, D_MODEL), tok),
        pl.BlockSpec((None, tm, D_ATT), tok),
        pl.BlockSpec((None, tm, D_SSD), tok),
        pl.BlockSpec((None, tm, D_SSD), tok),
        pl.BlockSpec((1, D_SSD), lambda b, i: (0, 0)),
        pl.BlockSpec((D_ATT + D_SSD, D_MODEL), lambda b, i: (0, 0), pipeline_mode=pl.Buffered(1)),
        pl.BlockSpec((None, 1, D_MODEL), lambda b, i: (row(b), 0, 2)),
        pl.BlockSpec((1, D_MODEL), lambda b, i: (0, 0)),
    ]
    return pl.pallas_call(
        _out_kernel, grid=(B, L // tm), in_specs=in_specs,
        out_specs=pl.BlockSpec((None, tm, D_MODEL), tok),
        out_shape=jax.ShapeDtypeStruct((B, L, D_MODEL), F32),
        compiler_params=_cparams(("parallel", "parallel")),
        name="out_proj",
    )(x, att, y, z, ssd_norm_g, w_out, mod, final_g)


def _rope_tables(L):
    rows = L // GRID_W
    row_ids = jnp.repeat(jnp.arange(rows), GRID_W).astype(F32)
    col_ids = jnp.tile(jnp.arange(GRID_W), rows).astype(F32)
    inv = ROPE_BASE ** (-jnp.arange(ROPE_FREQS, dtype=F32) / ROPE_FREQS)
    ang_r = row_ids[:, None] * inv
    ang_c = col_ids[:, None] * inv
    cr, sr, cc, sc = jnp.cos(ang_r), jnp.sin(ang_r), jnp.cos(ang_c), jnp.sin(ang_c)
    cos64 = jnp.concatenate([cr, cr, cc, cc], axis=1)
    sin64 = jnp.concatenate([-sr, sr, -sc, sc], axis=1)
    return jnp.tile(cos64, (1, 2)), jnp.tile(sin64, (1, 2))


def _bc_perm():
    idx = list(range(D_SSD))
    for g in range(SSD_GROUPS):
        idx += [D_SSD + g * D_STATE + n for n in range(D_STATE)]
        idx += [D_SSD + SSD_GROUPS * D_STATE + g * D_STATE + n for n in range(D_STATE)]
    return jnp.asarray(idx, dtype=jnp.int32)


def _dt_perm():
    idx = []
    for g in range(SSD_GROUPS):
        for d in range(2):
            for hh in range(HEADS_PER_GROUP):
                idx.append(d * H_SSD + g * HEADS_PER_GROUP + hh)
    return jnp.asarray(idx, dtype=jnp.int32)


def _group_rows(v2h):
    t = v2h.reshape(2, SSD_GROUPS, HEADS_PER_GROUP).transpose(1, 0, 2).reshape(SSD_GROUPS, 2 * HEADS_PER_GROUP)
    return jnp.broadcast_to(t[:, :, None], (SSD_GROUPS, 2 * HEADS_PER_GROUP, LANES))


def _layer(x, mod, mod_row, params, rope_tabs, past_kv, h0, emit_f32, lam_init, final_g, tm, tq):
    (norm_g, w_qkv, w_gzx, lams, subln_g, conv_w, conv_b, dtb_g, alog_g, dsk, ssd_norm_g, w_out) = params
    proj = _qkv_proj(x, mod, mod_row, norm_g, w_qkv, rope_tabs, emit_f32, tm)
    qt, k, vt = proj[:3]
    g, z, xbc, dtt = _gzx_proj(x, mod, mod_row, norm_g, w_gzx, tm)
    kvs = [(k, vt)]
    if past_kv is not None:
        kvs.append(past_kv)
    att = _attention(qt, kvs, g, subln_g, lams, lam_init, tq)
    y, hfin = _ssd(xbc, dtt, conv_w, conv_b, dtb_g, alog_g, dsk, h0)
    out = _out_proj(x, att, y, z, ssd_norm_g, w_out, mod, mod_row, final_g, tm)
    return out, proj[3:], hfin


def kernel(x_prompt, x_sample, cache_k, cache_v, state_ssd, c, c_ctx, w_mod, b_mod, norm_g, w_in,
           lambda_q1, lambda_k1, lambda_q2, lambda_k2, subln_g, conv_w, conv_b, dt_bias, A_log,
           D_skip, ssd_norm_g, w_out, final_g):
    b_ctx, l_ctx, _ = x_prompt.shape
    b_dec, l_dec, _ = x_sample.shape
    l_past = cache_k.shape[2]
    depth = w_mod.shape[0]
    assert depth == 1
    lam_init = 0.8 - 0.6 * math.exp(-0.3 * 0)

    w = w_in[0]
    w_qkv = w[:, :3 * D_ATT].astype(BF16)
    o = 3 * D_ATT
    w_g = w[:, o:o + D_ATT]
    w_z = w[:, o + D_ATT:o + D_ATT + D_SSD]
    o2 = o + D_ATT + D_SSD
    bc_perm = _bc_perm()
    w_xbc = w[:, o2:o2 + CONV_DIM][:, bc_perm]
    w_dt = jnp.pad(w[:, o2 + CONV_DIM:][:, _dt_perm()], ((0, 0), (0, DT_PAD - 2 * H_SSD)))
    w_gzx = jnp.concatenate([w_g, w_z, w_xbc, w_dt], axis=1).astype(BF16)
    conv_w_p = conv_w[0][:, bc_perm]
    conv_b_p = conv_b[0][bc_perm].reshape(1, CONV_DIM)
    dtb_g = _group_rows(dt_bias[0])
    alog_g = _group_rows(A_log[0])
    dsk = jnp.broadcast_to(jnp.repeat(D_skip[0], SSD_HEADDIM)[:, None], (D_SSD, LANES))
    lams = [a[0].reshape(1, HD_QK) for a in (lambda_q1, lambda_k1, lambda_q2, lambda_k2)]
    params = (norm_g[0].reshape(1, D_MODEL), w_qkv, w_gzx, lams, subln_g[0].reshape(1, HD_V),
              conv_w_p, conv_b_p, dtb_g, alog_g, dsk, ssd_norm_g[0].reshape(1, D_SSD),
              w_out[0].astype(BF16))
    fg = final_g.reshape(1, D_MODEL)

    cond8 = jnp.zeros((8, D_MODEL), F32).at[:b_dec].set(c).at[b_dec].set(c_ctx)
    mod = _adaln(cond8, w_mod[0], b_mod[0]).reshape(8, 1, 3 * D_MODEL)

    y_prompt, (k_ctx, v_ctx), h_ctx = _layer(
        x_prompt, mod, lambda b: b_dec, params, None, None, None, True, lam_init, fg,
        tm=256, tq=256)

    rope_tabs = _rope_tables(l_dec)
    k_past = cache_k[:, 0].reshape(b_dec, l_past, D_ATT).astype(BF16)
    vt_past = cache_v[:, 0].reshape(b_dec, l_past, D_ATT).transpose(0, 2, 1).astype(BF16)
    h0 = state_ssd[:, 0].reshape(b_dec, 2, SSD_GROUPS, GROUP_W, D_STATE)
    y_sample, _, _ = _layer(
        x_sample, mod, lambda b: b, params, rope_tabs, (k_past, vt_past), h0, False, lam_init, fg,
        tm=512, tq=256)

    new_cache_k = k_ctx.reshape(b_ctx, 1, l_ctx, H_ATT, 2 * HD_QK)
    new_cache_v = v_ctx.reshape(b_ctx, 1, l_ctx, H_ATT, HD_V)
    new_state = h_ctx.reshape(b_ctx, 1, 2, H_SSD, SSD_HEADDIM, D_STATE)
    return (y_prompt, y_sample, new_cache_k, new_cache_v, new_state)
```

```python
import functools
import math

import jax
import jax.numpy as jnp
from jax import lax
from jax.experimental import pallas as pl
from jax.experimental.pallas import tpu as pltpu

F32 = jnp.float32
BF16 = jnp.bfloat16

D_MODEL = 1024
GRID_W = 64
HD_QK = 64
HD_V = 128
H_ATT = 8
D_ATT = 1024
D_SSD = 1024
SSD_HEADDIM = 64
H_SSD = 16
SSD_GROUPS = 4
HEADS_PER_GROUP = 4
D_STATE = 64
CONV_DIM = D_SSD + 2 * SSD_GROUPS * D_STATE
CHUNK = 128
ROPE_BASE = 10000.0
ROPE_FREQS = 16
EPS = 1e-6

LANES = 128
DT_PAD = LANES
GROUP_W = HEADS_PER_GROUP * SSD_HEADDIM
VMEM_LIMIT = 56 * 1024 * 1024
ATT_TK = 512
ATT_TQ = 256
OUT_ROWS = 256
LOG2E = 1.4426950408889634
SCAN_UNROLL = 8


def _cparams(sem, flags=None):
    return pltpu.CompilerParams(dimension_semantics=sem, vmem_limit_bytes=VMEM_LIMIT, flags=flags)


def _silu(x):
    hx = 0.5 * x
    return hx + hx * jnp.tanh(hx)


def _softplus(x):
    return jnp.maximum(x, 0.0) + jnp.log1p(jnp.exp(-jnp.abs(x)))


def _adaln_kernel(cond_ref, w_ref, b_ref, o_ref):
    cond = cond_ref[...]
    s = _silu(cond).astype(BF16)
    o_ref[...] = jnp.dot(s, w_ref[...].astype(BF16), preferred_element_type=F32) + b_ref[...]


def _adaln(cond8, w_mod, b_mod):
    n = w_mod.shape[1]
    blk = 1024
    return pl.pallas_call(
        _adaln_kernel,
        grid=(n // blk,),
        in_specs=[pl.BlockSpec((8, D_MODEL), lambda j: (0, 0)),
                  pl.BlockSpec((D_MODEL, blk), lambda j: (0, j)),
                  pl.BlockSpec((1, blk), lambda j: (0, j))],
        out_specs=pl.BlockSpec((8, blk), lambda j: (0, j)),
        out_shape=jax.ShapeDtypeStruct((8, n), F32),
        compiler_params=_cparams(("arbitrary",)),
        name="adaln",
    )(cond8, w_mod, b_mod.reshape(1, n))


def _mod_norm(x, shift, scale, g):
    ms = jnp.mean(x * x, axis=-1, keepdims=True)
    y = x * lax.rsqrt(ms + EPS) * g
    return y * (1.0 + scale) + shift


def _rope(t, cos, sin_signed, first):
    outs = []
    for h in range(H_ATT):
        th = t[:, h * LANES:(h + 1) * LANES]
        partner = jnp.where(first, pltpu.roll(th, LANES - 16, 1), pltpu.roll(th, 16, 1))
        outs.append(th * cos + partner * sin_signed)
    return jnp.concatenate(outs, axis=1)


def _qkv_kernel(*refs, rope, emit_f32):
    x_ref, shift_ref, scale_ref, ng_ref, w_ref = refs[:5]
    pos = 5
    if rope:
        cos_ref, sin_ref = refs[pos:pos + 2]
        pos += 2
    qt_ref, k_ref, vt_ref = refs[pos:pos + 3]
    pos += 3
    if emit_f32:
        k32_ref, v32_ref = refs[pos:pos + 2]

    hb = _mod_norm(x_ref[...], shift_ref[...], scale_ref[...], ng_ref[...]).astype(BF16)
    q = jnp.dot(hb, w_ref[:, 0:D_ATT], preferred_element_type=F32)
    k = jnp.dot(hb, w_ref[:, D_ATT:2 * D_ATT], preferred_element_type=F32)
    v = jnp.dot(hb, w_ref[:, 2 * D_ATT:3 * D_ATT], preferred_element_type=F32)
    if rope:
        cos = cos_ref[...]
        sin_signed = sin_ref[...]
        lane = lax.broadcasted_iota(jnp.int32, cos.shape, 1)
        first = (lane % 32) < 16
        q = _rope(q, cos, sin_signed, first)
        k = _rope(k, cos, sin_signed, first)
    qt_ref[...] = (q * (LOG2E / math.sqrt(HD_QK))).T.astype(BF16)
    k_ref[...] = k.astype(BF16)
    vt_ref[...] = v.T.astype(BF16)
    if emit_f32:
        k32_ref[...] = k
        v32_ref[...] = v


def _qkv_proj(x, mod, mod_row, norm_g, w_qkv, rope_tabs, emit_f32, tm):
    B, L, _ = x.shape
    grid = (B, L // tm)
    row = mod_row

    in_specs = [
        pl.BlockSpec((None, tm, D_MODEL), lambda b, i: (b, i, 0)),
        pl.BlockSpec((None, 1, D_MODEL), lambda b, i: (row(b), 0, 0)),
        pl.BlockSpec((None, 1, D_MODEL), lambda b, i: (row(b), 0, 1)),
        pl.BlockSpec((1, D_MODEL), lambda b, i: (0, 0)),
        pl.BlockSpec((D_MODEL, 3 * D_ATT), lambda b, i: (0, 0), pipeline_mode=pl.Buffered(1)),
    ]
    args = [x, mod, mod, norm_g, w_qkv]
    if rope_tabs is not None:
        in_specs += [pl.BlockSpec((tm, LANES), lambda b, i: (i, 0)),
                     pl.BlockSpec((tm, LANES), lambda b, i: (i, 0))]
        args += list(rope_tabs)
    out_specs = [
        pl.BlockSpec((None, D_ATT, tm), lambda b, i: (b, 0, i)),
        pl.BlockSpec((None, tm, D_ATT), lambda b, i: (b, i, 0)),
        pl.BlockSpec((None, D_ATT, tm), lambda b, i: (b, 0, i)),
    ]
    out_shape = [
        jax.ShapeDtypeStruct((B, D_ATT, L), BF16),
        jax.ShapeDtypeStruct((B, L, D_ATT), BF16),
        jax.ShapeDtypeStruct((B, D_ATT, L), BF16),
    ]
    if emit_f32:
        out_specs += [pl.BlockSpec((None, tm, D_ATT), lambda b, i: (b, i, 0))] * 2
        out_shape += [jax.ShapeDtypeStruct((B, L, D_ATT), F32)] * 2
    return pl.pallas_call(
        functools.partial(_qkv_kernel, rope=rope_tabs is not None, emit_f32=emit_f32),
        grid=grid, in_specs=in_specs, out_specs=out_specs, out_shape=out_shape,
        compiler_params=_cparams(("parallel", "parallel")),
        name="qkv_proj",
    )(*args)


N_GZX = 2 * D_MODEL + CONV_DIM + DT_PAD


def _gzx_kernel(x_ref, shift_ref, scale_ref, ng_ref, w_ref, g_ref, z_ref, xbc_ref, dtt_ref):
    hb = _mod_norm(x_ref[...], shift_ref[...], scale_ref[...], ng_ref[...]).astype(BF16)
    g_ref[...] = jnp.dot(hb, w_ref[:, 0:D_ATT], preferred_element_type=F32)
    z_ref[...] = jnp.dot(hb, w_ref[:, D_ATT:D_ATT + D_SSD], preferred_element_type=F32)
    o = D_ATT + D_SSD
    xbc_ref[...] = jnp.dot(hb, w_ref[:, o:o + CONV_DIM], preferred_element_type=F32)
    dt = jnp.dot(hb, w_ref[:, o + CONV_DIM:o + CONV_DIM + DT_PAD], preferred_element_type=F32)
    dtt_ref[...] = dt.T[:2 * H_SSD, :]


def _gzx_proj(x, mod, mod_row, norm_g, w_gzx, tm):
    B, L, _ = x.shape
    row = mod_row
    in_specs = [
        pl.BlockSpec((None, tm, D_MODEL), lambda b, i: (b, i, 0)),
        pl.BlockSpec((None, 1, D_MODEL), lambda b, i: (row(b), 0, 0)),
        pl.BlockSpec((None, 1, D_MODEL), lambda b, i: (row(b), 0, 1)),
        pl.BlockSpec((1, D_MODEL), lambda b, i: (0, 0)),
        pl.BlockSpec((D_MODEL, N_GZX), lambda b, i: (0, 0), pipeline_mode=pl.Buffered(1)),
    ]
    widths = (D_ATT, D_SSD, CONV_DIM)
    out_specs = [pl.BlockSpec((None, tm, w), lambda b, i: (b, i, 0)) for w in widths]
    out_shape = [jax.ShapeDtypeStruct((B, L, w), F32) for w in widths]
    out_specs.append(pl.BlockSpec((None, 2 * H_SSD, tm), lambda b, i: (b, 0, i)))
    out_shape.append(jax.ShapeDtypeStruct((B, 2 * H_SSD, L), F32))
    return pl.pallas_call(
        _gzx_kernel, grid=(B, L // tm), in_specs=in_specs, out_specs=out_specs, out_shape=out_shape,
        compiler_params=_cparams(("parallel", "parallel")),
        name="gzx_proj",
    )(x, mod, mod, norm_g, w_gzx)


def _attn_kernel(*refs, n_src, lam_init, carry):
    qt_ref = refs[0]
    pos = 1
    if carry:
        qt_next_ref = refs[pos]
        pos += 1
    srcs = [(refs[pos + 2 * i], refs[pos + 1 + 2 * i]) for i in range(n_src)]
    pos += 2 * n_src
    g_ref, sg_ref, lq1_ref, lk1_ref, lq2_ref, lk2_ref, o_ref = refs[pos:pos + 7]
    if carry:
        s_sc, m_sc = refs[pos + 7:pos + 9]

    tq = ATT_TQ
    n_heads = qt_ref.shape[0] // HD_V
    chains = [(hh, t) for hh in range(n_heads) for t in range(qt_ref.shape[1] // tq)]
    n_ch = len(chains)
    row = lax.broadcasted_iota(jnp.int32, (HD_V, tq), 0)
    zero = jnp.zeros((HD_V, tq), BF16)

    def stacked_q(ref, hh, t):
        qt = ref[hh * HD_V:(hh + 1) * HD_V, t * tq:(t + 1) * tq]
        return jnp.concatenate([jnp.where(row < HD_QK, qt, zero), jnp.where(row >= HD_QK, qt, zero)], axis=1)

    q2t = [stacked_q(qt_ref, hh, t) for hh, t in chains]

    blocks = []
    for k_ref, vt_ref in srcs:
        n_keys = k_ref.shape[0]
        tk = min(ATT_TK, n_keys)
        blocks += [(k_ref, vt_ref, j * tk, tk) for j in range(n_keys // tk)]

    lam = (jnp.exp(jnp.sum(lq1_ref[...] * lk1_ref[...], axis=1, keepdims=True))
           - jnp.exp(jnp.sum(lq2_ref[...] * lk2_ref[...], axis=1, keepdims=True)) + lam_init)

    def score_piece(c, j, q_stacked=None):
        k_ref, _, off, tk = blocks[j]
        hh = chains[c][0]
        s = jnp.dot(k_ref[off:off + tk, hh * HD_V:(hh + 1) * HD_V],
                    q2t[c] if q_stacked is None else q_stacked, preferred_element_type=F32)
        return s, jnp.max(s, axis=0, keepdims=True)

    def prob_piece(c, j, s, m, l, acc):
        _, vt_ref, off, tk = blocks[j]
        hh = chains[c][0]
        p = jnp.exp2(s - m)
        l = l + jnp.sum(p, axis=0, keepdims=True)
        acc = acc + jnp.dot(vt_ref[hh * HD_V:(hh + 1) * HD_V, off:off + tk], p.astype(BF16),
                            preferred_element_type=F32)
        return l, acc

    def epilogue(c, l, acc):
        hh, t = chains[c]
        o_n = acc / l
        o = (o_n[:, :tq] - lam * o_n[:, tq:]).T
        ms = jnp.mean(o * o, axis=-1, keepdims=True)
        att = o * lax.rsqrt(ms + EPS) * sg_ref[...] * (1.0 - lam_init)
        gate = _silu(g_ref[t * tq:(t + 1) * tq, hh * HD_V:(hh + 1) * HD_V])
        o_ref[t * tq:(t + 1) * tq, hh * HD_V:(hh + 1) * HD_V] = (att * gate).astype(BF16)

    n_pc = len(blocks)
    neg_inf = jnp.full((1, 2 * tq), -jnp.inf, F32)
    if not carry:
        prev = None
        for c in range(n_ch + 1):
            cur_s, cur_m = [], neg_inf
            if prev is not None:
                l = jnp.zeros((1, 2 * tq), F32)
                acc = jnp.zeros((HD_V, 2 * tq), F32)
            for j in range(n_pc):
                if c < n_ch:
                    s, s_max = score_piece(c, j)
                    cur_s.append(s)
                    cur_m = jnp.maximum(cur_m, s_max)
                if prev is not None:
                    l, acc = prob_piece(prev[0], j, prev[1][j], prev[2], l, acc)
            if prev is not None:
                epilogue(prev[0], l, acc)
            prev = (c, cur_s, cur_m) if c < n_ch else None
        return

    dst = [sum(b[3] for b in blocks[:j]) for j in range(n_pc)]
    step = pl.program_id(2)
    cur, nxt = step % 2, (step + 1) % 2

    @pl.when(step == 0)
    def _():
        m0 = neg_inf
        for j in range(n_pc):
            s, s_max = score_piece(0, j)
            s_sc[0, dst[j]:dst[j] + blocks[j][3], :] = s
            m0 = jnp.maximum(m0, s_max)
        m_sc[0] = jnp.broadcast_to(m0, m_sc.shape[1:])

    hh0, t0 = chains[0]
    q_next = stacked_q(qt_next_ref, hh0, t0)
    prev_s, prev_m = None, m_sc[cur][0:1, :]
    for u in range(n_ch):
        l = jnp.zeros((1, 2 * tq), F32)
        acc = jnp.zeros((HD_V, 2 * tq), F32)
        nxt_s, nxt_m = [], neg_inf
        for j in range(n_pc):
            rows = slice(dst[j], dst[j] + blocks[j][3])
            if u + 1 < n_ch:
                s, s_max = score_piece(u + 1, j)
                nxt_s.append(s)
            else:
                s, s_max = score_piece(0, j, q_next)
                s_sc[nxt, rows, :] = s
            nxt_m = jnp.maximum(nxt_m, s_max)
            s_u = s_sc[cur, rows, :] if u == 0 else prev_s[j]
            l, acc = prob_piece(u, j, s_u, prev_m, l, acc)
        if u + 1 == n_ch:
            m_sc[nxt] = jnp.broadcast_to(nxt_m, m_sc.shape[1:])
        epilogue(u, l, acc)
        prev_s, prev_m = nxt_s, nxt_m


def _attention(qt, kvs, g, subln_g, lams, lam_init, tq, heads_per_step):
    B, _, L = qt.shape
    hw = heads_per_step * HD_V
    nq = L // tq
    carry = nq > 1
    in_specs = [pl.BlockSpec((None, hw, tq), lambda b, h, i: (b, h, i))]
    args = [qt]
    scratch = []
    if carry:
        in_specs.append(pl.BlockSpec((None, hw, tq), lambda b, h, i: (b, h, jnp.minimum(i + 1, nq - 1))))
        args.append(qt)
        n_all = sum(k.shape[1] for k, _ in kvs)
        scratch = [pltpu.VMEM((2, n_all, 2 * ATT_TQ), F32), pltpu.VMEM((2, 8, 2 * ATT_TQ), F32)]
    for k, vt in kvs:
        n_keys = k.shape[1]
        in_specs += [pl.BlockSpec((None, n_keys, hw), lambda b, h, i: (b, 0, h)),
                     pl.BlockSpec((None, hw, n_keys), lambda b, h, i: (b, h, 0))]
        args += [k, vt]
    in_specs += [pl.BlockSpec((None, tq, hw), lambda b, h, i: (b, i, h)),
                 pl.BlockSpec((1, HD_V), lambda b, h, i: (0, 0))]
    in_specs += [pl.BlockSpec((1, HD_QK), lambda b, h, i: (0, 0))] * 4
    args += [g, subln_g] + list(lams)
    return pl.pallas_call(
        functools.partial(_attn_kernel, n_src=len(kvs), lam_init=lam_init, carry=carry),
        grid=(B, H_ATT // heads_per_step, nq),
        in_specs=in_specs,
        out_specs=pl.BlockSpec((None, tq, hw), lambda b, h, i: (b, i, h)),
        out_shape=jax.ShapeDtypeStruct((B, L, D_ATT), BF16),
        scratch_shapes=scratch,
        compiler_params=_cparams(("parallel", "parallel", "arbitrary")),
        name="diff_attention",
    )(*args)


def _ssd_kernel(*refs, seq_len, has_h0):
    (x_ref, bc_ref, dtt_ref, cwx_ref, cwbc_ref, cbx_ref, cbbc_ref,
     dtb_ref, alog_ref, dsk_ref) = refs[:10]
    pos = 10
    if has_h0:
        h0_ref = refs[pos]
        pos += 1
    y_ref, hfin_ref, xt_sc, yt_sc, b_sc, ct_sc = refs[pos:pos + 6]

    nc = seq_len // CHUNK
    n_grp = x_ref.shape[1] // GROUP_W
    row_i = lax.broadcasted_iota(jnp.int32, (CHUNK, CHUNK), 0)
    col_i = lax.broadcasted_iota(jnp.int32, (CHUNK, CHUNK), 1)
    row1 = lax.broadcasted_iota(jnp.int32, (CHUNK, 1), 0)

    def conv_silu(src, w_ref, b_ref, c):
        r0 = pl.multiple_of(c * CHUNK, CHUNK)
        rp = pl.multiple_of(jnp.maximum(r0 - 8, 0), 8)
        rn = pl.multiple_of(jnp.minimum(r0 + CHUNK, seq_len - 8), 8)
        u = src[pl.ds(r0, CHUNK), :]
        pr = jnp.where(c > 0, src[pl.ds(rp, 8), :][7:8, :], 0.0)
        nx = jnp.where(c < nc - 1, src[pl.ds(rn, 8), :][0:1, :], 0.0)
        prev = jnp.where(row1 == 0, pr, pltpu.roll(u, 1, 0))
        nxt = jnp.where(row1 == CHUNK - 1, nx, pltpu.roll(u, CHUNK - 1, 0))
        w = w_ref[...]
        return _silu(b_ref[...] + prev * w[0:1, :] + u * w[1:2, :] + nxt * w[2:3, :])

    def conv_chunk(c, carry):
        xt = conv_silu(x_ref, cwx_ref, cbx_ref, c).T
        xt_sc[c] = xt
        yt_sc[c] = dsk_ref[...] * xt
        bc = conv_silu(bc_ref, cwbc_ref, cbbc_ref, c)
        for g in range(n_grp):
            bc_g = bc[:, g * LANES:(g + 1) * LANES]
            b_sc[c, g] = bc_g[:, :D_STATE].astype(BF16)
            ct_sc[c, g] = bc_g.T[D_STATE:, :].astype(BF16)
        return carry

    lax.fori_loop(0, nc, conv_chunk, 0)

    dir_heads = 2 * HEADS_PER_GROUP
    dtb = [dtb_ref[g] for g in range(n_grp)]
    a_neg = [-jnp.exp(alog_ref[g]) for g in range(n_grp)]

    masks = (row_i <= col_i, row_i >= col_i)
    tri = tuple(m.astype(F32) for m in masks)
    ends = (CHUNK - 1, 0)

    def scan_items(items, states):
        n = len(items)
        sp, cum, cb_t = [None] * n, [None] * n, [None] * n
        for k, (g, d, c) in enumerate(items):
            t0 = pl.multiple_of(c * CHUNK, CHUNK)
            dt_g = dtt_ref[g * dir_heads:(g + 1) * dir_heads, pl.ds(t0, CHUNK)]
            sp[k] = _softplus(dt_g + dtb[g])
            cum[k] = jnp.dot(sp[k] * a_neg[g], tri[d], precision=lax.Precision.HIGHEST,
                             preferred_element_type=F32)
        for k, (g, d, c) in enumerate(items):
            cb_t[k] = jnp.dot(b_sc[c, g], ct_sc[c, g], preferred_element_type=F32)
        xdt_b, lm_b, x_dec, y_scale, s_scale = [], [], [], [], []
        for k, (g, d, c) in enumerate(items):
            cum_col = cum[k].T
            xt = xt_sc[c, g * GROUP_W:(g + 1) * GROUP_W, :]
            xdt_k, lm_k, x_dec_k, y_scale_k, s_scale_k = [], [], [], [], []
            for hh in range(HEADS_PER_GROUP):
                r = d * HEADS_PER_GROUP + hh
                a_row = cum[k][r:r + 1, :]
                seg = a_row - cum_col[:, r:r + 1]
                lmat = jnp.exp(jnp.where(masks[d], seg, -jnp.inf))
                xdt = xt[hh * SSD_HEADDIM:(hh + 1) * SSD_HEADDIM, :] * sp[k][r:r + 1, :]
                a_end = a_row[:, ends[d]:ends[d] + 1]
                xdt_k.append(xdt.astype(BF16))
                lm_k.append((cb_t[k] * lmat).astype(BF16))
                x_dec_k.append((xdt * jnp.exp(a_end - a_row)).astype(BF16))
                y_scale_k.append(jnp.broadcast_to(jnp.exp(a_row), (SSD_HEADDIM, CHUNK)))
                s_scale_k.append(jnp.broadcast_to(jnp.exp(a_end), (SSD_HEADDIM, D_STATE)))
            xdt_b.append(xdt_k)
            lm_b.append(lm_k)
            x_dec.append(jnp.concatenate(x_dec_k, axis=0))
            y_scale.append(jnp.concatenate(y_scale_k, axis=0))
            s_scale.append(jnp.concatenate(s_scale_k, axis=0))
        y_diag, st = [], []
        for k, (g, d, c) in enumerate(items):
            y_diag.append(jnp.concatenate(
                [jnp.dot(xdt_b[k][hh], lm_b[k][hh], preferred_element_type=F32)
                 for hh in range(HEADS_PER_GROUP)], axis=0))
            st.append(jnp.dot(x_dec[k], b_sc[c, g], preferred_element_type=F32))
        states = list(states)
        for k, (g, d, c) in enumerate(items):
            si = 2 * g + d
            rows = slice(g * GROUP_W, (g + 1) * GROUP_W)
            y_off = jnp.dot(states[si].astype(BF16), ct_sc[c, g], preferred_element_type=F32)
            yt_sc[c, rows, :] = yt_sc[c, rows, :] + y_diag[k] + y_off * y_scale[k]
            states[si] = states[si] * s_scale[k] + st[k]
        return tuple(states)

    if has_h0:
        s_init = tuple(h0_ref[d, g] for g in range(n_grp) for d in range(2))
    else:
        s_init = (jnp.zeros((GROUP_W, D_STATE), F32),) * (2 * n_grp)

    unroll = math.gcd(max(SCAN_UNROLL // n_grp, 1), nc)

    def scan_step(i, states):
        items = []
        for u in range(unroll):
            for g in range(n_grp):
                items += [(g, 0, i * unroll + u), (g, 1, nc - 1 - (i * unroll + u))]
        return scan_items(items, states)

    s_fin = lax.fori_loop(0, nc // unroll, scan_step, s_init)
    for g in range(n_grp):
        for d in range(2):
            hfin_ref[d, g] = s_fin[2 * g + d]

    def emit_chunk(c, carry):
        y_ref[pl.ds(pl.multiple_of(c * CHUNK, CHUNK), CHUNK), :] = yt_sc[c].T
        return carry

    lax.fori_loop(0, nc, emit_chunk, 0)


def _ssd(xbc, dtt, conv_w, conv_b, dtb_g, alog_g, dsk, h0, n_grp):
    B, L, _ = xbc.shape
    nc = L // CHUNK
    xw, bw = n_grp * GROUP_W, n_grp * LANES
    bc_blk = D_SSD // bw
    dir_heads = 2 * HEADS_PER_GROUP
    state_spec = pl.BlockSpec((None, 2, n_grp, GROUP_W, D_STATE), lambda b, g: (b, 0, g, 0, 0))
    in_specs = [
        pl.BlockSpec((None, L, xw), lambda b, g: (b, 0, g)),
        pl.BlockSpec((None, L, bw), lambda b, g: (b, 0, bc_blk + g)),
        pl.BlockSpec((None, n_grp * dir_heads, L), lambda b, g: (b, g, 0)),
        pl.BlockSpec((3, xw), lambda b, g: (0, g)),
        pl.BlockSpec((3, bw), lambda b, g: (0, bc_blk + g)),
        pl.BlockSpec((1, xw), lambda b, g: (0, g)),
        pl.BlockSpec((1, bw), lambda b, g: (0, bc_blk + g)),
        pl.BlockSpec((n_grp, dir_heads, LANES), lambda b, g: (g, 0, 0)),
        pl.BlockSpec((n_grp, dir_heads, LANES), lambda b, g: (g, 0, 0)),
        pl.BlockSpec((xw, LANES), lambda b, g: (g, 0)),
    ]
    args = [xbc, xbc, dtt, conv_w, conv_w, conv_b, conv_b, dtb_g, alog_g, dsk]
    if h0 is not None:
        in_specs.append(state_spec)
        args.append(h0)
    return pl.pallas_call(
        functools.partial(_ssd_kernel, seq_len=L, has_h0=h0 is not None),
        grid=(B, SSD_GROUPS // n_grp),
        in_specs=in_specs,
        out_specs=[pl.BlockSpec((None, L, xw), lambda b, g: (b, 0, g)), state_spec],
        out_shape=[jax.ShapeDtypeStruct((B, L, D_SSD), F32),
                   jax.ShapeDtypeStruct((B, 2, SSD_GROUPS, GROUP_W, D_STATE), F32)],
        scratch_shapes=[pltpu.VMEM((nc, xw, CHUNK), F32), pltpu.VMEM((nc, xw, CHUNK), F32),
                        pltpu.VMEM((nc, n_grp, CHUNK, D_STATE), BF16),
                        pltpu.VMEM((nc, n_grp, D_STATE, CHUNK), BF16)],
        compiler_params=_cparams(("parallel", "parallel")),
        name="ssd_scan",
    )(*args)


def _out_kernel(x_ref, att_ref, y_ref, z_ref, sng_ref, w_ref, gate_ref, fg_ref, o_ref):
    tm = x_ref.shape[0]
    for r0 in range(0, tm, OUT_ROWS):
        rows = slice(r0, r0 + OUT_ROWS)
        yz = y_ref[rows, :] * _silu(z_ref[rows, :])
        ms = jnp.mean(yz * yz, axis=-1, keepdims=True)
        ssd_y = (yz * lax.rsqrt(ms + EPS) * sng_ref[...]).astype(BF16)
        out = jnp.dot(att_ref[rows, :], w_ref[0:D_ATT, :], preferred_element_type=F32)
        out = out + jnp.dot(ssd_y, w_ref[D_ATT:D_ATT + D_SSD, :], preferred_element_type=F32)
        r = x_ref[rows, :] + gate_ref[...] * out
        ms2 = jnp.mean(r * r, axis=-1, keepdims=True)
        o_ref[rows, :] = r * lax.rsqrt(ms2 + EPS) * fg_ref[...]


def _out_proj(x, att, y, z, ssd_norm_g, w_out, mod, mod_row, final_g, tm):
    B, L, _ = x.shape
    row = mod_row
    tok = lambda b, i: (b, i, 0)
    in_specs = [
        pl.BlockSpec((None, tm, D_MODEL), tok),
        pl.BlockSpec((None, tm, D_ATT), tok),
        pl.BlockSpec((None, tm, D_SSD), tok),
        pl.BlockSpec((None, tm, D_SSD), tok),
        pl.BlockSpec((1, D_SSD), lambda b, i: (0, 0)),
        pl.BlockSpec((D_ATT + D_SSD, D_MODEL), lambda b, i: (0, 0), pipeline_mode=pl.Buffered(1)),
        pl.BlockSpec((None, 1, D_MODEL), lambda b, i: (row(b), 0, 2)),
        pl.BlockSpec((1, D_MODEL), lambda b, i: (0, 0)),
    ]
    return pl.pallas_call(
        _out_kernel, grid=(B, L // tm), in_specs=in_specs,
        out_specs=pl.BlockSpec((None, tm, D_MODEL), tok),
        out_shape=jax.ShapeDtypeStruct((B, L, D_MODEL), F32),
        compiler_params=_cparams(("parallel", "parallel")),
        name="out_proj",
    )(x, att, y, z, ssd_norm_g, w_out, mod, final_g)


def _rope_tables(L):
    rows = L // GRID_W
    row_ids = jnp.repeat(jnp.arange(rows), GRID_W).astype(F32)
    col_ids = jnp.tile(jnp.arange(GRID_W), rows).astype(F32)
    inv = ROPE_BASE ** (-jnp.arange(ROPE_FREQS, dtype=F32) / ROPE_FREQS)
    ang_r = row_ids[:, None] * inv
    ang_c = col_ids[:, None] * inv
    cr, sr, cc, sc = jnp.cos(ang_r), jnp.sin(ang_r), jnp.cos(ang_c), jnp.sin(ang_c)
    cos64 = jnp.concatenate([cr, cr, cc, cc], axis=1)
    sin64 = jnp.concatenate([-sr, sr, -sc, sc], axis=1)
    return jnp.tile(cos64, (1, 2)), jnp.tile(sin64, (1, 2))


def _bc_perm():
    idx = list(range(D_SSD))
    for g in range(SSD_GROUPS):
        idx += [D_SSD + g * D_STATE + n for n in range(D_STATE)]
        idx += [D_SSD + SSD_GROUPS * D_STATE + g * D_STATE + n for n in range(D_STATE)]
    return jnp.asarray(idx, dtype=jnp.int32)


def _dt_perm():
    idx = []
    for g in range(SSD_GROUPS):
        for d in range(2):
            for hh in range(HEADS_PER_GROUP):
                idx.append(d * H_SSD + g * HEADS_PER_GROUP + hh)
    return jnp.asarray(idx, dtype=jnp.int32)


def _group_rows(v2h):
    t = v2h.reshape(2, SSD_GROUPS, HEADS_PER_GROUP).transpose(1, 0, 2).reshape(SSD_GROUPS, 2 * HEADS_PER_GROUP)
    return jnp.broadcast_to(t[:, :, None], (SSD_GROUPS, 2 * HEADS_PER_GROUP, LANES))


def _layer(x, mod, mod_row, params, rope_tabs, past_kv, h0, emit_f32, lam_init, final_g, tm, tq,
           heads_per_step, groups_per_step):
    (norm_g, w_qkv, w_gzx, lams, subln_g, conv_w, conv_b, dtb_g, alog_g, dsk, ssd_norm_g, w_out) = params
    proj = _qkv_proj(x, mod, mod_row, norm_g, w_qkv, rope_tabs, emit_f32, tm)
    qt, k, vt = proj[:3]
    g, z, xbc, dtt = _gzx_proj(x, mod, mod_row, norm_g, w_gzx, tm)
    kvs = [(k, vt)]
    if past_kv is not None:
        kvs.append(past_kv)
    att = _attention(qt, kvs, g, subln_g, lams, lam_init, tq, heads_per_step)
    y, hfin = _ssd(xbc, dtt, conv_w, conv_b, dtb_g, alog_g, dsk, h0, groups_per_step)
    out = _out_proj(x, att, y, z, ssd_norm_g, w_out, mod, mod_row, final_g, tm)
    return out, proj[3:], hfin


def kernel(x_prompt, x_sample, cache_k, cache_v, state_ssd, c, c_ctx, w_mod, b_mod, norm_g, w_in,
           lambda_q1, lambda_k1, lambda_q2, lambda_k2, subln_g, conv_w, conv_b, dt_bias, A_log,
           D_skip, ssd_norm_g, w_out, final_g):
    b_ctx, l_ctx, _ = x_prompt.shape
    b_dec, l_dec, _ = x_sample.shape
    l_past = cache_k.shape[2]
    depth = w_mod.shape[0]
    assert depth == 1
    lam_init = 0.8 - 0.6 * math.exp(-0.3 * 0)

    w = w_in[0]
    w_qkv = w[:, :3 * D_ATT].astype(BF16)
    o = 3 * D_ATT
    w_g = w[:, o:o + D_ATT]
    w_z = w[:, o + D_ATT:o + D_ATT + D_SSD]
    o2 = o + D_ATT + D_SSD
    bc_perm = _bc_perm()
    w_xbc = w[:, o2:o2 + CONV_DIM][:, bc_perm]
    w_dt = jnp.pad(w[:, o2 + CONV_DIM:][:, _dt_perm()], ((0, 0), (0, DT_PAD - 2 * H_SSD)))
    w_gzx = jnp.concatenate([w_g, w_z, w_xbc, w_dt], axis=1).astype(BF16)
    conv_w_p = conv_w[0][:, bc_perm]
    conv_b_p = conv_b[0][bc_perm].reshape(1, CONV_DIM)
    dtb_g = _group_rows(dt_bias[0])
    alog_g = _group_rows(A_log[0])
    dsk = jnp.broadcast_to(jnp.repeat(D_skip[0], SSD_HEADDIM)[:, None], (D_SSD, LANES))
    lams = [a[0].reshape(1, HD_QK) for a in (lambda_q1, lambda_k1, lambda_q2, lambda_k2)]
    params = (norm_g[0].reshape(1, D_MODEL), w_qkv, w_gzx, lams, subln_g[0].reshape(1, HD_V),
              conv_w_p, conv_b_p, dtb_g, alog_g, dsk, ssd_norm_g[0].reshape(1, D_SSD),
              w_out[0].astype(BF16))
    fg = final_g.reshape(1, D_MODEL)

    cond8 = jnp.zeros((8, D_MODEL), F32).at[:b_dec].set(c).at[b_dec].set(c_ctx)
    mod = _adaln(cond8, w_mod[0], b_mod[0]).reshape(8, 1, 3 * D_MODEL)

    y_prompt, (k_ctx, v_ctx), h_ctx = _layer(
        x_prompt, mod, lambda b: b_dec, params, None, None, None, True, lam_init, fg,
        tm=256, tq=256, heads_per_step=H_ATT, groups_per_step=SSD_GROUPS)

    rope_tabs = _rope_tables(l_dec)
    k_past = cache_k[:, 0].reshape(b_dec, l_past, D_ATT).astype(BF16)
    vt_past = cache_v[:, 0].reshape(b_dec, l_past, D_ATT).transpose(0, 2, 1).astype(BF16)
    h0 = state_ssd[:, 0].reshape(b_dec, 2, SSD_GROUPS, GROUP_W, D_STATE)
    y_sample, _, _ = _layer(
        x_sample, mod, lambda b: b, params, rope_tabs, (k_past, vt_past), h0, False, lam_init, fg,
        tm=512, tq=512, heads_per_step=1, groups_per_step=1)

    new_cache_k = k_ctx.reshape(b_ctx, 1, l_ctx, H_ATT, 2 * HD_QK)
    new_cache_v = v_ctx.reshape(b_ctx, 1, l_ctx, H_ATT, HD_V)
    new_state = h_ctx.reshape(b_ctx, 1, 2, H_SSD, SSD_HEADDIM, D_STATE)
    return (y_prompt, y_sample, new_cache_k, new_cache_v, new_state)
```

```python
import functools
import math

import jax
import jax.numpy as jnp
from jax import lax
from jax.experimental import pallas as pl
from jax.experimental.pallas import tpu as pltpu

F32 = jnp.float32
BF16 = jnp.bfloat16

D_MODEL = 1024
GRID_W = 64
HD_QK = 64
HD_V = 128
H_ATT = 8
D_ATT = 1024
D_SSD = 1024
SSD_HEADDIM = 64
H_SSD = 16
SSD_GROUPS = 4
HEADS_PER_GROUP = 4
D_STATE = 64
CONV_DIM = D_SSD + 2 * SSD_GROUPS * D_STATE
CHUNK = 128
ROPE_BASE = 10000.0
ROPE_FREQS = 16
EPS = 1e-6

LANES = 128
DT_PAD = LANES
GROUP_W = HEADS_PER_GROUP * SSD_HEADDIM
VMEM_LIMIT = 56 * 1024 * 1024
ATT_TK = 512
ATT_TQ = 256
OUT_ROWS = 256
LOG2E = 1.4426950408889634
CONV_UNROLL = 4
SCAN_UNROLL = 8


def _cparams(sem, flags=None):
    return pltpu.CompilerParams(dimension_semantics=sem, vmem_limit_bytes=VMEM_LIMIT, flags=flags)


def _silu(x):
    hx = 0.5 * x
    return hx + hx * jnp.tanh(hx)


def _softplus(x):
    return jnp.maximum(x, 0.0) + jnp.log1p(jnp.exp(-jnp.abs(x)))


def _adaln_kernel(cond_ref, w_ref, b_ref, o_ref):
    cond = cond_ref[...]
    s = _silu(cond).astype(BF16)
    o_ref[...] = jnp.dot(s, w_ref[...].astype(BF16), preferred_element_type=F32) + b_ref[...]


def _adaln(cond8, w_mod, b_mod):
    n = w_mod.shape[1]
    blk = 1024
    return pl.pallas_call(
        _adaln_kernel,
        grid=(n // blk,),
        in_specs=[pl.BlockSpec((8, D_MODEL), lambda j: (0, 0)),
                  pl.BlockSpec((D_MODEL, blk), lambda j: (0, j)),
                  pl.BlockSpec((1, blk), lambda j: (0, j))],
        out_specs=pl.BlockSpec((8, blk), lambda j: (0, j)),
        out_shape=jax.ShapeDtypeStruct((8, n), F32),
        compiler_params=_cparams(("arbitrary",)),
        name="adaln",
    )(cond8, w_mod, b_mod.reshape(1, n))


def _mod_norm(x, shift, scale, g):
    ms = jnp.mean(x * x, axis=-1, keepdims=True)
    y = x * lax.rsqrt(ms + EPS) * g
    return y * (1.0 + scale) + shift


def _rope(t, cos, sin_signed, first):
    outs = []
    for h in range(H_ATT):
        th = t[:, h * LANES:(h + 1) * LANES]
        partner = jnp.where(first, pltpu.roll(th, LANES - 16, 1), pltpu.roll(th, 16, 1))
        outs.append(th * cos + partner * sin_signed)
    return jnp.concatenate(outs, axis=1)


def _qkv_kernel(*refs, rope, emit_f32):
    x_ref, shift_ref, scale_ref, ng_ref, w_ref = refs[:5]
    pos = 5
    if rope:
        cos_ref, sin_ref = refs[pos:pos + 2]
        pos += 2
    qt_ref, k_ref, vt_ref = refs[pos:pos + 3]
    pos += 3
    if emit_f32:
        k32_ref, v32_ref = refs[pos:pos + 2]

    hb = _mod_norm(x_ref[...], shift_ref[...], scale_ref[...], ng_ref[...]).astype(BF16)
    q = jnp.dot(hb, w_ref[:, 0:D_ATT], preferred_element_type=F32)
    k = jnp.dot(hb, w_ref[:, D_ATT:2 * D_ATT], preferred_element_type=F32)
    v = jnp.dot(hb, w_ref[:, 2 * D_ATT:3 * D_ATT], preferred_element_type=F32)
    if rope:
        cos = cos_ref[...]
        sin_signed = sin_ref[...]
        lane = lax.broadcasted_iota(jnp.int32, cos.shape, 1)
        first = (lane % 32) < 16
        q = _rope(q, cos, sin_signed, first)
        k = _rope(k, cos, sin_signed, first)
    qt_ref[...] = (q * (LOG2E / math.sqrt(HD_QK))).T.astype(BF16)
    k_ref[...] = k.astype(BF16)
    vt_ref[...] = v.T.astype(BF16)
    if emit_f32:
        k32_ref[...] = k
        v32_ref[...] = v


def _qkv_proj(x, mod, mod_row, norm_g, w_qkv, rope_tabs, emit_f32, tm):
    B, L, _ = x.shape
    grid = (B, L // tm)
    row = mod_row

    in_specs = [
        pl.BlockSpec((None, tm, D_MODEL), lambda b, i: (b, i, 0)),
        pl.BlockSpec((None, 1, D_MODEL), lambda b, i: (row(b), 0, 0)),
        pl.BlockSpec((None, 1, D_MODEL), lambda b, i: (row(b), 0, 1)),
        pl.BlockSpec((1, D_MODEL), lambda b, i: (0, 0)),
        pl.BlockSpec((D_MODEL, 3 * D_ATT), lambda b, i: (0, 0), pipeline_mode=pl.Buffered(1)),
    ]
    args = [x, mod, mod, norm_g, w_qkv]
    if rope_tabs is not None:
        in_specs += [pl.BlockSpec((tm, LANES), lambda b, i: (i, 0)),
                     pl.BlockSpec((tm, LANES), lambda b, i: (i, 0))]
        args += list(rope_tabs)
    out_specs = [
        pl.BlockSpec((None, D_ATT, tm), lambda b, i: (b, 0, i)),
        pl.BlockSpec((None, tm, D_ATT), lambda b, i: (b, i, 0)),
        pl.BlockSpec((None, D_ATT, tm), lambda b, i: (b, 0, i)),
    ]
    out_shape = [
        jax.ShapeDtypeStruct((B, D_ATT, L), BF16),
        jax.ShapeDtypeStruct((B, L, D_ATT), BF16),
        jax.ShapeDtypeStruct((B, D_ATT, L), BF16),
    ]
    if emit_f32:
        out_specs += [pl.BlockSpec((None, tm, D_ATT), lambda b, i: (b, i, 0))] * 2
        out_shape += [jax.ShapeDtypeStruct((B, L, D_ATT), F32)] * 2
    return pl.pallas_call(
        functools.partial(_qkv_kernel, rope=rope_tabs is not None, emit_f32=emit_f32),
        grid=grid, in_specs=in_specs, out_specs=out_specs, out_shape=out_shape,
        compiler_params=_cparams(("parallel", "parallel")),
        name="qkv_proj",
    )(*args)


N_GZX = 2 * D_MODEL + CONV_DIM + DT_PAD


def _gzx_kernel(x_ref, shift_ref, scale_ref, ng_ref, w_ref, g_ref, z_ref, xbc_ref, dtt_ref):
    hb = _mod_norm(x_ref[...], shift_ref[...], scale_ref[...], ng_ref[...]).astype(BF16)
    g_ref[...] = jnp.dot(hb, w_ref[:, 0:D_ATT], preferred_element_type=F32)
    z_ref[...] = jnp.dot(hb, w_ref[:, D_ATT:D_ATT + D_SSD], preferred_element_type=F32)
    o = D_ATT + D_SSD
    xbc_ref[...] = jnp.dot(hb, w_ref[:, o:o + CONV_DIM], preferred_element_type=F32)
    dt = jnp.dot(hb, w_ref[:, o + CONV_DIM:o + CONV_DIM + DT_PAD], preferred_element_type=F32)
    dtt_ref[...] = dt.T[:2 * H_SSD, :]


def _gzx_proj(x, mod, mod_row, norm_g, w_gzx, tm):
    B, L, _ = x.shape
    row = mod_row
    in_specs = [
        pl.BlockSpec((None, tm, D_MODEL), lambda b, i: (b, i, 0)),
        pl.BlockSpec((None, 1, D_MODEL), lambda b, i: (row(b), 0, 0)),
        pl.BlockSpec((None, 1, D_MODEL), lambda b, i: (row(b), 0, 1)),
        pl.BlockSpec((1, D_MODEL), lambda b, i: (0, 0)),
        pl.BlockSpec((D_MODEL, N_GZX), lambda b, i: (0, 0), pipeline_mode=pl.Buffered(1)),
    ]
    widths = (D_ATT, D_SSD, CONV_DIM)
    out_specs = [pl.BlockSpec((None, tm, w), lambda b, i: (b, i, 0)) for w in widths]
    out_shape = [jax.ShapeDtypeStruct((B, L, w), F32) for w in widths]
    out_specs.append(pl.BlockSpec((None, 2 * H_SSD, tm), lambda b, i: (b, 0, i)))
    out_shape.append(jax.ShapeDtypeStruct((B, 2 * H_SSD, L), F32))
    return pl.pallas_call(
        _gzx_kernel, grid=(B, L // tm), in_specs=in_specs, out_specs=out_specs, out_shape=out_shape,
        compiler_params=_cparams(("parallel", "parallel")),
        name="gzx_proj",
    )(x, mod, mod, norm_g, w_gzx)


def _attn_kernel(*refs, n_src, lam_init, carry):
    qt_ref = refs[0]
    pos = 1
    if carry:
        qt_next_ref = refs[pos]
        pos += 1
    srcs = [(refs[pos + 2 * i], refs[pos + 1 + 2 * i]) for i in range(n_src)]
    pos += 2 * n_src
    g_ref, sg_ref, lq1_ref, lk1_ref, lq2_ref, lk2_ref, o_ref = refs[pos:pos + 7]
    if carry:
        s_sc, m_sc = refs[pos + 7:pos + 9]

    tq = ATT_TQ
    n_heads = qt_ref.shape[0] // HD_V
    chains = [(hh, t) for hh in range(n_heads) for t in range(qt_ref.shape[1] // tq)]
    n_ch = len(chains)
    row = lax.broadcasted_iota(jnp.int32, (HD_V, tq), 0)
    zero = jnp.zeros((HD_V, tq), BF16)

    def stacked_q(ref, hh, t):
        qt = ref[hh * HD_V:(hh + 1) * HD_V, t * tq:(t + 1) * tq]
        return jnp.concatenate([jnp.where(row < HD_QK, qt, zero), jnp.where(row >= HD_QK, qt, zero)], axis=1)

    q2t = [stacked_q(qt_ref, hh, t) for hh, t in chains]

    blocks = []
    for k_ref, vt_ref in srcs:
        n_keys = k_ref.shape[0]
        tk = min(ATT_TK, n_keys)
        blocks += [(k_ref, vt_ref, j * tk, tk) for j in range(n_keys // tk)]

    lam = (jnp.exp(jnp.sum(lq1_ref[...] * lk1_ref[...], axis=1, keepdims=True))
           - jnp.exp(jnp.sum(lq2_ref[...] * lk2_ref[...], axis=1, keepdims=True)) + lam_init)

    def score_piece(c, j, q_stacked=None):
        k_ref, _, off, tk = blocks[j]
        hh = chains[c][0]
        s = jnp.dot(k_ref[off:off + tk, hh * HD_V:(hh + 1) * HD_V],
                    q2t[c] if q_stacked is None else q_stacked, preferred_element_type=F32)
        return s, jnp.max(s, axis=0, keepdims=True)

    def prob_piece(c, j, s, m, l, acc):
        _, vt_ref, off, tk = blocks[j]
        hh = chains[c][0]
        p = jnp.exp2(s - m)
        l = l + jnp.sum(p, axis=0, keepdims=True)
        acc = acc + jnp.dot(vt_ref[hh * HD_V:(hh + 1) * HD_V, off:off + tk], p.astype(BF16),
                            preferred_element_type=F32)
        return l, acc

    def epilogue(c, l, acc):
        hh, t = chains[c]
        o_n = acc / l
        o = (o_n[:, :tq] - lam * o_n[:, tq:]).T
        ms = jnp.mean(o * o, axis=-1, keepdims=True)
        att = o * lax.rsqrt(ms + EPS) * sg_ref[...] * (1.0 - lam_init)
        gate = _silu(g_ref[t * tq:(t + 1) * tq, hh * HD_V:(hh + 1) * HD_V])
        o_ref[t * tq:(t + 1) * tq, hh * HD_V:(hh + 1) * HD_V] = (att * gate).astype(BF16)

    n_pc = len(blocks)
    neg_inf = jnp.full((1, 2 * tq), -jnp.inf, F32)
    if not carry:
        prev = None
        for c in range(n_ch + 1):
            cur_s, cur_m = [], neg_inf
            if prev is not None:
                l = jnp.zeros((1, 2 * tq), F32)
                acc = jnp.zeros((HD_V, 2 * tq), F32)
            for j in range(n_pc):
                if c < n_ch:
                    s, s_max = score_piece(c, j)
                    cur_s.append(s)
                    cur_m = jnp.maximum(cur_m, s_max)
                if prev is not None:
                    l, acc = prob_piece(prev[0], j, prev[1][j], prev[2], l, acc)
            if prev is not None:
                epilogue(prev[0], l, acc)
            prev = (c, cur_s, cur_m) if c < n_ch else None
        return

    dst = [sum(b[3] for b in blocks[:j]) for j in range(n_pc)]
    step = pl.program_id(2)
    cur, nxt = step % 2, (step + 1) % 2

    @pl.when(step == 0)
    def _():
        m0 = neg_inf
        for j in range(n_pc):
            s, s_max = score_piece(0, j)
            s_sc[0, dst[j]:dst[j] + blocks[j][3], :] = s
            m0 = jnp.maximum(m0, s_max)
        m_sc[0] = jnp.broadcast_to(m0, m_sc.shape[1:])

    hh0, t0 = chains[0]
    q_next = stacked_q(qt_next_ref, hh0, t0)
    prev_s, prev_m = None, m_sc[cur][0:1, :]
    for u in range(n_ch):
        l = jnp.zeros((1, 2 * tq), F32)
        acc = jnp.zeros((HD_V, 2 * tq), F32)
        nxt_s, nxt_m = [], neg_inf
        for j in range(n_pc):
            rows = slice(dst[j], dst[j] + blocks[j][3])
            if u + 1 < n_ch:
                s, s_max = score_piece(u + 1, j)
                nxt_s.append(s)
            else:
                s, s_max = score_piece(0, j, q_next)
                s_sc[nxt, rows, :] = s
            nxt_m = jnp.maximum(nxt_m, s_max)
            s_u = s_sc[cur, rows, :] if u == 0 else prev_s[j]
            l, acc = prob_piece(u, j, s_u, prev_m, l, acc)
        if u + 1 == n_ch:
            m_sc[nxt] = jnp.broadcast_to(nxt_m, m_sc.shape[1:])
        epilogue(u, l, acc)
        prev_s, prev_m = nxt_s, nxt_m


def _attention(qt, kvs, g, subln_g, lams, lam_init, tq, heads_per_step):
    B, _, L = qt.shape
    hw = heads_per_step * HD_V
    nq = L // tq
    carry = nq > 1
    in_specs = [pl.BlockSpec((None, hw, tq), lambda b, h, i: (b, h, i))]
    args = [qt]
    scratch = []
    if carry:
        in_specs.append(pl.BlockSpec((None, hw, tq), lambda b, h, i: (b, h, jnp.minimum(i + 1, nq - 1))))
        args.append(qt)
        n_all = sum(k.shape[1] for k, _ in kvs)
        scratch = [pltpu.VMEM((2, n_all, 2 * ATT_TQ), F32), pltpu.VMEM((2, 8, 2 * ATT_TQ), F32)]
    for k, vt in kvs:
        n_keys = k.shape[1]
        in_specs += [pl.BlockSpec((None, n_keys, hw), lambda b, h, i: (b, 0, h)),
                     pl.BlockSpec((None, hw, n_keys), lambda b, h, i: (b, h, 0))]
        args += [k, vt]
    in_specs += [pl.BlockSpec((None, tq, hw), lambda b, h, i: (b, i, h)),
                 pl.BlockSpec((1, HD_V), lambda b, h, i: (0, 0))]
    in_specs += [pl.BlockSpec((1, HD_QK), lambda b, h, i: (0, 0))] * 4
    args += [g, subln_g] + list(lams)
    return pl.pallas_call(
        functools.partial(_attn_kernel, n_src=len(kvs), lam_init=lam_init, carry=carry),
        grid=(B, H_ATT // heads_per_step, nq),
        in_specs=in_specs,
        out_specs=pl.BlockSpec((None, tq, hw), lambda b, h, i: (b, i, h)),
        out_shape=jax.ShapeDtypeStruct((B, L, D_ATT), BF16),
        scratch_shapes=scratch,
        compiler_params=_cparams(("parallel", "parallel", "arbitrary")),
        name="diff_attention",
    )(*args)


def _ssd_kernel(*refs, seq_len, has_h0):
    (x_ref, bc_ref, dtt_ref, cwx_ref, cwbc_ref, cbx_ref, cbbc_ref,
     dtb_ref, alog_ref, dsk_ref) = refs[:10]
    pos = 10
    if has_h0:
        h0_ref = refs[pos]
        pos += 1
    y_ref, hfin_ref, xt_sc, yt_sc, b_sc, ct_sc = refs[pos:pos + 6]

    nc = seq_len // CHUNK
    n_grp = x_ref.shape[1] // GROUP_W
    row_i = lax.broadcasted_iota(jnp.int32, (CHUNK, CHUNK), 0)
    col_i = lax.broadcasted_iota(jnp.int32, (CHUNK, CHUNK), 1)
    row1 = lax.broadcasted_iota(jnp.int32, (CHUNK, 1), 0)

    def conv_silu(src, w_ref, b_ref, c):
        r0 = pl.multiple_of(c * CHUNK, CHUNK)
        rp = pl.multiple_of(jnp.maximum(r0 - 8, 0), 8)
        rn = pl.multiple_of(jnp.minimum(r0 + CHUNK, seq_len - 8), 8)
        u = src[pl.ds(r0, CHUNK), :]
        pr = jnp.where(c > 0, src[pl.ds(rp, 8), :][7:8, :], 0.0)
        nx = jnp.where(c < nc - 1, src[pl.ds(rn, 8), :][0:1, :], 0.0)
        prev = jnp.where(row1 == 0, pr, pltpu.roll(u, 1, 0))
        nxt = jnp.where(row1 == CHUNK - 1, nx, pltpu.roll(u, CHUNK - 1, 0))
        w = 0.5 * w_ref[...]
        hx = 0.5 * b_ref[...] + prev * w[0:1, :] + u * w[1:2, :] + nxt * w[2:3, :]
        return hx + hx * jnp.tanh(hx)

    def conv_chunk(c, carry):
        xt = conv_silu(x_ref, cwx_ref, cbx_ref, c).T
        xt_sc[c] = xt
        yt_sc[c] = dsk_ref[...] * xt
        bc = conv_silu(bc_ref, cwbc_ref, cbbc_ref, c)
        for g in range(n_grp):
            bc_g = bc[:, g * LANES:(g + 1) * LANES]
            b_sc[c, g] = bc_g[:, :D_STATE].astype(BF16)
            ct_sc[c, g] = bc_g.T[D_STATE:, :].astype(BF16)
        return carry

    lax.fori_loop(0, nc, conv_chunk, 0, unroll=math.gcd(CONV_UNROLL, nc))

    dir_heads = 2 * HEADS_PER_GROUP
    dtb = [dtb_ref[g] for g in range(n_grp)]
    a_neg = [-jnp.exp(alog_ref[g]) * LOG2E for g in range(n_grp)]

    masks = (row_i <= col_i, row_i >= col_i)
    tri = tuple(jnp.where(m, 1.0, 0.0).astype(BF16) for m in masks)
    ends = (CHUNK - 1, 0)

    def scan_items(items, states):
        n = len(items)
        sp, cum, cum_col, cb_t = [None] * n, [None] * n, [None] * n, [None] * n
        for k, (g, d, c) in enumerate(items):
            t0 = pl.multiple_of(c * CHUNK, CHUNK)
            dt_g = dtt_ref[g * dir_heads:(g + 1) * dir_heads, pl.ds(t0, CHUNK)]
            sp[k] = _softplus(dt_g + dtb[g])
        for d in range(2):
            ks = [k for k, it in enumerate(items) if it[1] == d]
            x = jnp.concatenate([sp[k] * a_neg[items[k][0]] for k in ks], axis=0)
            hi = x.astype(BF16)
            r1 = x - hi.astype(F32)
            mid = r1.astype(BF16)
            lo = (r1 - mid.astype(F32)).astype(BF16)
            parts = jnp.dot(jnp.concatenate([hi, mid, lo], axis=0), tri[d], preferred_element_type=F32)
            nr = x.shape[0]
            cum_d = parts[:nr] + parts[nr:2 * nr] + parts[2 * nr:]
            cum_dt = cum_d.T
            for i, k in enumerate(ks):
                cum[k] = cum_d[i * dir_heads:(i + 1) * dir_heads]
                cum_col[k] = cum_dt[:, i * dir_heads:(i + 1) * dir_heads]
        for k, (g, d, c) in enumerate(items):
            cb_t[k] = jnp.dot(b_sc[c, g], ct_sc[c, g], preferred_element_type=F32)
        xdt_b, lm_b, x_dec, y_scale, s_scale = [], [], [], [], []
        for k, (g, d, c) in enumerate(items):
            xt = xt_sc[c, g * GROUP_W:(g + 1) * GROUP_W, :]
            xdt_k, lm_k, x_dec_k, y_scale_k, s_scale_k = [], [], [], [], []
            for hh in range(HEADS_PER_GROUP):
                r = d * HEADS_PER_GROUP + hh
                a_row = cum[k][r:r + 1, :]
                seg = a_row - cum_col[k][:, r:r + 1]
                lmat = jnp.exp2(jnp.where(masks[d], seg, -jnp.inf))
                xdt = xt[hh * SSD_HEADDIM:(hh + 1) * SSD_HEADDIM, :] * sp[k][r:r + 1, :]
                a_end = a_row[:, ends[d]:ends[d] + 1]
                xdt_k.append(xdt.astype(BF16))
                lm_k.append((cb_t[k] * lmat).astype(BF16))
                x_dec_k.append((xdt * jnp.exp2(a_end - a_row)).astype(BF16))
                y_scale_k.append(jnp.broadcast_to(jnp.exp2(a_row), (SSD_HEADDIM, CHUNK)))
                s_scale_k.append(jnp.broadcast_to(jnp.exp2(a_end), (SSD_HEADDIM, D_STATE)))
            xdt_b.append(xdt_k)
            lm_b.append(lm_k)
            x_dec.append(jnp.concatenate(x_dec_k, axis=0))
            y_scale.append(jnp.concatenate(y_scale_k, axis=0))
            s_scale.append(jnp.concatenate(s_scale_k, axis=0))
        y_diag, st = [], []
        for k, (g, d, c) in enumerate(items):
            y_diag.append(jnp.concatenate(
                [jnp.dot(xdt_b[k][hh], lm_b[k][hh], preferred_element_type=F32)
                 for hh in range(HEADS_PER_GROUP)], axis=0))
            st.append(jnp.dot(x_dec[k], b_sc[c, g], preferred_element_type=F32))
        states = list(states)
        for k, (g, d, c) in enumerate(items):
            si = 2 * g + d
            rows = slice(g * GROUP_W, (g + 1) * GROUP_W)
            y_off = jnp.dot(states[si].astype(BF16), ct_sc[c, g], preferred_element_type=F32)
            yt_sc[c, rows, :] = yt_sc[c, rows, :] + y_diag[k] + y_off * y_scale[k]
            states[si] = states[si] * s_scale[k] + st[k]
        return tuple(states)

    if has_h0:
        s_init = tuple(h0_ref[d, g] for g in range(n_grp) for d in range(2))
    else:
        s_init = (jnp.zeros((GROUP_W, D_STATE), F32),) * (2 * n_grp)

    unroll = math.gcd(max(SCAN_UNROLL // n_grp, 1), nc)

    def scan_step(i, states):
        items = []
        for u in range(unroll):
            for g in range(n_grp):
                items += [(g, 0, i * unroll + u), (g, 1, nc - 1 - (i * unroll + u))]
        return scan_items(items, states)

    s_fin = lax.fori_loop(0, nc // unroll, scan_step, s_init)
    for g in range(n_grp):
        for d in range(2):
            hfin_ref[d, g] = s_fin[2 * g + d]

    def emit_chunk(c, carry):
        y_ref[pl.ds(pl.multiple_of(c * CHUNK, CHUNK), CHUNK), :] = yt_sc[c].T
        return carry

    lax.fori_loop(0, nc, emit_chunk, 0, unroll=math.gcd(CONV_UNROLL, nc))


def _ssd(xbc, dtt, conv_w, conv_b, dtb_g, alog_g, dsk, h0, n_grp):
    B, L, _ = xbc.shape
    nc = L // CHUNK
    xw, bw = n_grp * GROUP_W, n_grp * LANES
    bc_blk = D_SSD // bw
    dir_heads = 2 * HEADS_PER_GROUP
    state_spec = pl.BlockSpec((None, 2, n_grp, GROUP_W, D_STATE), lambda b, g: (b, 0, g, 0, 0))
    in_specs = [
        pl.BlockSpec((None, L, xw), lambda b, g: (b, 0, g)),
        pl.BlockSpec((None, L, bw), lambda b, g: (b, 0, bc_blk + g)),
        pl.BlockSpec((None, n_grp * dir_heads, L), lambda b, g: (b, g, 0)),
        pl.BlockSpec((3, xw), lambda b, g: (0, g)),
        pl.BlockSpec((3, bw), lambda b, g: (0, bc_blk + g)),
        pl.BlockSpec((1, xw), lambda b, g: (0, g)),
        pl.BlockSpec((1, bw), lambda b, g: (0, bc_blk + g)),
        pl.BlockSpec((n_grp, dir_heads, LANES), lambda b, g: (g, 0, 0)),
        pl.BlockSpec((n_grp, dir_heads, LANES), lambda b, g: (g, 0, 0)),
        pl.BlockSpec((xw, LANES), lambda b, g: (g, 0)),
    ]
    args = [xbc, xbc, dtt, conv_w, conv_w, conv_b, conv_b, dtb_g, alog_g, dsk]
    if h0 is not None:
        in_specs.append(state_spec)
        args.append(h0)
    return pl.pallas_call(
        functools.partial(_ssd_kernel, seq_len=L, has_h0=h0 is not None),
        grid=(B, SSD_GROUPS // n_grp),
        in_specs=in_specs,
        out_specs=[pl.BlockSpec((None, L, xw), lambda b, g: (b, 0, g)), state_spec],
        out_shape=[jax.ShapeDtypeStruct((B, L, D_SSD), F32),
                   jax.ShapeDtypeStruct((B, 2, SSD_GROUPS, GROUP_W, D_STATE), F32)],
        scratch_shapes=[pltpu.VMEM((nc, xw, CHUNK), F32), pltpu.VMEM((nc, xw, CHUNK), F32),
                        pltpu.VMEM((nc, n_grp, CHUNK, D_STATE), BF16),
                        pltpu.VMEM((nc, n_grp, D_STATE, CHUNK), BF16)],
        compiler_params=_cparams(("parallel", "parallel")),
        name="ssd_scan",
    )(*args)


def _out_kernel(x_ref, att_ref, y_ref, z_ref, sng_ref, w_ref, gate_ref, fg_ref, o_ref):
    tm = x_ref.shape[0]
    for r0 in range(0, tm, OUT_ROWS):
        rows = slice(r0, r0 + OUT_ROWS)
        yz = y_ref[rows, :] * _silu(z_ref[rows, :])
        ms = jnp.mean(yz * yz, axis=-1, keepdims=True)
        ssd_y = (yz * lax.rsqrt(ms + EPS) * sng_ref[...]).astype(BF16)
        out = jnp.dot(att_ref[rows, :], w_ref[0:D_ATT, :], preferred_element_type=F32)
        out = out + jnp.dot(ssd_y, w_ref[D_ATT:D_ATT + D_SSD, :], preferred_element_type=F32)
        r = x_ref[rows, :] + gate_ref[...] * out
        ms2 = jnp.mean(r * r, axis=-1, keepdims=True)
        o_ref[rows, :] = r * lax.rsqrt(ms2 + EPS) * fg_ref[...]


def _out_proj(x, att, y, z, ssd_norm_g, w_out, mod, mod_row, final_g, tm):
    B, L, _ = x.shape
    row = mod_row
    tok = lambda b, i: (b, i, 0)
    in_specs = [
        pl.BlockSpec((None, tm, D_MODEL), tok),
        pl.BlockSpec((None, tm, D_ATT), tok),
        pl.BlockSpec((None, tm, D_SSD), tok),
        pl.BlockSpec((None, tm, D_SSD), tok),
        pl.BlockSpec((1, D_SSD), lambda b, i: (0, 0)),
        pl.BlockSpec((D_ATT + D_SSD, D_MODEL), lambda b, i: (0, 0), pipeline_mode=pl.Buffered(1)),
        pl.BlockSpec((None, 1, D_MODEL), lambda b, i: (row(b), 0, 2)),
        pl.BlockSpec((1, D_MODEL), lambda b, i: (0, 0)),
    ]
    return pl.pallas_call(
        _out_kernel, grid=(B, L // tm), in_specs=in_specs,
        out_specs=pl.BlockSpec((None, tm, D_MODEL), tok),
        out_shape=jax.ShapeDtypeStruct((B, L, D_MODEL), F32),
        compiler_params=_cparams(("parallel", "parallel")),
        name="out_proj",
    )(x, att, y, z, ssd_norm_g, w_out, mod, final_g)


def _rope_tables(L):
    rows = L // GRID_W
    row_ids = jnp.repeat(jnp.arange(rows), GRID_W).astype(F32)
    col_ids = jnp.tile(jnp.arange(GRID_W), rows).astype(F32)
    inv = ROPE_BASE ** (-jnp.arange(ROPE_FREQS, dtype=F32) / ROPE_FREQS)
    ang_r = row_ids[:, None] * inv
    ang_c = col_ids[:, None] * inv
    cr, sr, cc, sc = jnp.cos(ang_r), jnp.sin(ang_r), jnp.cos(ang_c), jnp.sin(ang_c)
    cos64 = jnp.concatenate([cr, cr, cc, cc], axis=1)
    sin64 = jnp.concatenate([-sr, sr, -sc, sc], axis=1)
    return jnp.tile(cos64, (1, 2)), jnp.tile(sin64, (1, 2))


def _bc_perm():
    idx = list(range(D_SSD))
    for g in range(SSD_GROUPS):
        idx += [D_SSD + g * D_STATE + n for n in range(D_STATE)]
        idx += [D_SSD + SSD_GROUPS * D_STATE + g * D_STATE + n for n in range(D_STATE)]
    return jnp.asarray(idx, dtype=jnp.int32)


def _dt_perm():
    idx = []
    for g in range(SSD_GROUPS):
        for d in range(2):
            for hh in range(HEADS_PER_GROUP):
                idx.append(d * H_SSD + g * HEADS_PER_GROUP + hh)
    return jnp.asarray(idx, dtype=jnp.int32)


def _group_rows(v2h):
    t = v2h.reshape(2, SSD_GROUPS, HEADS_PER_GROUP).transpose(1, 0, 2).reshape(SSD_GROUPS, 2 * HEADS_PER_GROUP)
    return jnp.broadcast_to(t[:, :, None], (SSD_GROUPS, 2 * HEADS_PER_GROUP, LANES))


def _layer(x, mod, mod_row, params, rope_tabs, past_kv, h0, emit_f32, lam_init, final_g, tm, tq,
           heads_per_step, groups_per_step):
    (norm_g, w_qkv, w_gzx, lams, subln_g, conv_w, conv_b, dtb_g, alog_g, dsk, ssd_norm_g, w_out) = params
    proj = _qkv_proj(x, mod, mod_row, norm_g, w_qkv, rope_tabs, emit_f32, tm)
    qt, k, vt = proj[:3]
    g, z, xbc, dtt = _gzx_proj(x, mod, mod_row, norm_g, w_gzx, tm)
    kvs = [(k, vt)]
    if past_kv is not None:
        kvs.append(past_kv)
    att = _attention(qt, kvs, g, subln_g, lams, lam_init, tq, heads_per_step)
    y, hfin = _ssd(xbc, dtt, conv_w, conv_b, dtb_g, alog_g, dsk, h0, groups_per_step)
    out = _out_proj(x, att, y, z, ssd_norm_g, w_out, mod, mod_row, final_g, tm)
    return out, proj[3:], hfin


def kernel(x_prompt, x_sample, cache_k, cache_v, state_ssd, c, c_ctx, w_mod, b_mod, norm_g, w_in,
           lambda_q1, lambda_k1, lambda_q2, lambda_k2, subln_g, conv_w, conv_b, dt_bias, A_log,
           D_skip, ssd_norm_g, w_out, final_g):
    b_ctx, l_ctx, _ = x_prompt.shape
    b_dec, l_dec, _ = x_sample.shape
    l_past = cache_k.shape[2]
    depth = w_mod.shape[0]
    assert depth == 1
    lam_init = 0.8 - 0.6 * math.exp(-0.3 * 0)

    w = w_in[0]
    w_qkv = w[:, :3 * D_ATT].astype(BF16)
    o = 3 * D_ATT
    w_g = w[:, o:o + D_ATT]
    w_z = w[:, o + D_ATT:o + D_ATT + D_SSD]
    o2 = o + D_ATT + D_SSD
    bc_perm = _bc_perm()
    w_xbc = w[:, o2:o2 + CONV_DIM][:, bc_perm]
    w_dt = jnp.pad(w[:, o2 + CONV_DIM:][:, _dt_perm()], ((0, 0), (0, DT_PAD - 2 * H_SSD)))
    w_gzx = jnp.concatenate([w_g, w_z, w_xbc, w_dt], axis=1).astype(BF16)
    conv_w_p = conv_w[0][:, bc_perm]
    conv_b_p = conv_b[0][bc_perm].reshape(1, CONV_DIM)
    dtb_g = _group_rows(dt_bias[0])
    alog_g = _group_rows(A_log[0])
    dsk = jnp.broadcast_to(jnp.repeat(D_skip[0], SSD_HEADDIM)[:, None], (D_SSD, LANES))
    lams = [a[0].reshape(1, HD_QK) for a in (lambda_q1, lambda_k1, lambda_q2, lambda_k2)]
    params = (norm_g[0].reshape(1, D_MODEL), w_qkv, w_gzx, lams, subln_g[0].reshape(1, HD_V),
              conv_w_p, conv_b_p, dtb_g, alog_g, dsk, ssd_norm_g[0].reshape(1, D_SSD),
              w_out[0].astype(BF16))
    fg = final_g.reshape(1, D_MODEL)

    cond8 = jnp.zeros((8, D_MODEL), F32).at[:b_dec].set(c).at[b_dec].set(c_ctx)
    mod = _adaln(cond8, w_mod[0], b_mod[0]).reshape(8, 1, 3 * D_MODEL)

    y_prompt, (k_ctx, v_ctx), h_ctx = _layer(
        x_prompt, mod, lambda b: b_dec, params, None, None, None, True, lam_init, fg,
        tm=256, tq=256, heads_per_step=H_ATT, groups_per_step=SSD_GROUPS)

    rope_tabs = _rope_tables(l_dec)
    k_past = cache_k[:, 0].reshape(b_dec, l_past, D_ATT).astype(BF16)
    vt_past = cache_v[:, 0].reshape(b_dec, l_past, D_ATT).transpose(0, 2, 1).astype(BF16)
    h0 = state_ssd[:, 0].reshape(b_dec, 2, SSD_GROUPS, GROUP_W, D_STATE)
    y_sample, _, _ = _layer(
        x_sample, mod, lambda b: b, params, rope_tabs, (k_past, vt_past), h0, False, lam_init, fg,
        tm=512, tq=512, heads_per_step=1, groups_per_step=1)

    new_cache_k = k_ctx.reshape(b_ctx, 1, l_ctx, H_ATT, 2 * HD_QK)
    new_cache_v = v_ctx.reshape(b_ctx, 1, l_ctx, H_ATT, HD_V)
    new_state = h_ctx.reshape(b_ctx, 1, 2, H_SSD, SSD_HEADDIM, D_STATE)
    return (y_prompt, y_sample, new_cache_k, new_cache_v, new_state)
```

```python
import functools
import math

import jax
import jax.numpy as jnp
from jax import lax
from jax.experimental import pallas as pl
from jax.experimental.pallas import tpu as pltpu

F32 = jnp.float32
BF16 = jnp.bfloat16

D_MODEL = 1024
GRID_W = 64
HD_QK = 64
HD_V = 128
H_ATT = 8
D_ATT = 1024
D_SSD = 1024
SSD_HEADDIM = 64
H_SSD = 16
SSD_GROUPS = 4
HEADS_PER_GROUP = 4
D_STATE = 64
CONV_DIM = D_SSD + 2 * SSD_GROUPS * D_STATE
CHUNK = 128
ROPE_BASE = 10000.0
ROPE_FREQS = 16
EPS = 1e-6

LANES = 128
DT_PAD = LANES
GROUP_W = HEADS_PER_GROUP * SSD_HEADDIM
VMEM_LIMIT = 56 * 1024 * 1024
ATT_TK = 512
ATT_TQ = 256
OUT_ROWS = 256
LOG2E = 1.4426950408889634
CONV_UNROLL = 4
SCAN_UNROLL = 8


def _cparams(sem, flags=None):
    return pltpu.CompilerParams(dimension_semantics=sem, vmem_limit_bytes=VMEM_LIMIT, flags=flags)


def _silu(x):
    hx = 0.5 * x
    return hx + hx * jnp.tanh(hx)


def _softplus(x):
    return jnp.maximum(x, 0.0) + jnp.log1p(jnp.exp(-jnp.abs(x)))


def _adaln_kernel(cond_ref, w_ref, b_ref, o_ref):
    cond = cond_ref[...]
    s = _silu(cond).astype(BF16)
    o_ref[...] = jnp.dot(s, w_ref[...].astype(BF16), preferred_element_type=F32) + b_ref[...]


def _adaln(cond8, w_mod, b_mod):
    n = w_mod.shape[1]
    blk = 1024
    return pl.pallas_call(
        _adaln_kernel,
        grid=(n // blk,),
        in_specs=[pl.BlockSpec((8, D_MODEL), lambda j: (0, 0)),
                  pl.BlockSpec((D_MODEL, blk), lambda j: (0, j)),
                  pl.BlockSpec((1, blk), lambda j: (0, j))],
        out_specs=pl.BlockSpec((8, blk), lambda j: (0, j)),
        out_shape=jax.ShapeDtypeStruct((8, n), F32),
        compiler_params=_cparams(("arbitrary",)),
        name="adaln",
    )(cond8, w_mod, b_mod.reshape(1, n))


def _mod_norm(x, shift, scale, g):
    ms = jnp.mean(x * x, axis=-1, keepdims=True)
    y = x * lax.rsqrt(ms + EPS) * g
    return y * (1.0 + scale) + shift


def _rope(t, cos, sin_signed, first):
    outs = []
    for h in range(H_ATT):
        th = t[:, h * LANES:(h + 1) * LANES]
        partner = jnp.where(first, pltpu.roll(th, LANES - 16, 1), pltpu.roll(th, 16, 1))
        outs.append(th * cos + partner * sin_signed)
    return jnp.concatenate(outs, axis=1)


def _qkv_kernel(*refs, rope, emit_f32):
    x_ref, shift_ref, scale_ref, ng_ref, w_ref = refs[:5]
    pos = 5
    if rope:
        cos_ref, sin_ref = refs[pos:pos + 2]
        pos += 2
    qt_ref, k_ref, vt_ref = refs[pos:pos + 3]
    pos += 3
    if emit_f32:
        k32_ref, v32_ref = refs[pos:pos + 2]

    hb = _mod_norm(x_ref[...], shift_ref[...], scale_ref[...], ng_ref[...]).astype(BF16)
    q = jnp.dot(hb, w_ref[:, 0:D_ATT], preferred_element_type=F32)
    k = jnp.dot(hb, w_ref[:, D_ATT:2 * D_ATT], preferred_element_type=F32)
    v = jnp.dot(hb, w_ref[:, 2 * D_ATT:3 * D_ATT], preferred_element_type=F32)
    if rope:
        cos = cos_ref[...]
        sin_signed = sin_ref[...]
        lane = lax.broadcasted_iota(jnp.int32, cos.shape, 1)
        first = (lane % 32) < 16
        q = _rope(q, cos, sin_signed, first)
        k = _rope(k, cos, sin_signed, first)
    qt_ref[...] = (q * (LOG2E / math.sqrt(HD_QK))).T.astype(BF16)
    k_ref[...] = k.astype(BF16)
    vt_ref[...] = v.T.astype(BF16)
    if emit_f32:
        k32_ref[...] = k
        v32_ref[...] = v


def _qkv_proj(x, mod, mod_row, norm_g, w_bf, rope_tabs, emit_f32, tm):
    B, L, _ = x.shape
    grid = (B, L // tm)
    row = mod_row

    in_specs = [
        pl.BlockSpec((None, tm, D_MODEL), lambda b, i: (b, i, 0)),
        pl.BlockSpec((None, 1, D_MODEL), lambda b, i: (row(b), 0, 0)),
        pl.BlockSpec((None, 1, D_MODEL), lambda b, i: (row(b), 0, 1)),
        pl.BlockSpec((1, D_MODEL), lambda b, i: (0, 0)),
        pl.BlockSpec((D_MODEL, 3 * D_ATT), lambda b, i: (0, 0), pipeline_mode=pl.Buffered(1)),
    ]
    args = [x, mod, mod, norm_g, w_bf]
    if rope_tabs is not None:
        in_specs += [pl.BlockSpec((tm, LANES), lambda b, i: (i, 0)),
                     pl.BlockSpec((tm, LANES), lambda b, i: (i, 0))]
        args += list(rope_tabs)
    out_specs = [
        pl.BlockSpec((None, D_ATT, tm), lambda b, i: (b, 0, i)),
        pl.BlockSpec((None, tm, D_ATT), lambda b, i: (b, i, 0)),
        pl.BlockSpec((None, D_ATT, tm), lambda b, i: (b, 0, i)),
    ]
    out_shape = [
        jax.ShapeDtypeStruct((B, D_ATT, L), BF16),
        jax.ShapeDtypeStruct((B, L, D_ATT), BF16),
        jax.ShapeDtypeStruct((B, D_ATT, L), BF16),
    ]
    if emit_f32:
        out_specs += [pl.BlockSpec((None, tm, D_ATT), lambda b, i: (b, i, 0))] * 2
        out_shape += [jax.ShapeDtypeStruct((B, L, D_ATT), F32)] * 2
    return pl.pallas_call(
        functools.partial(_qkv_kernel, rope=rope_tabs is not None, emit_f32=emit_f32),
        grid=grid, in_specs=in_specs, out_specs=out_specs, out_shape=out_shape,
        compiler_params=_cparams(("parallel", "parallel")),
        name="qkv_proj",
    )(*args)


def _gzx_kernel(x_ref, shift_ref, scale_ref, ng_ref, wg_ref, wz_ref, wx_ref, wdt_ref,
                g_ref, z_ref, xbc_ref, dtt_ref):
    hb = _mod_norm(x_ref[...], shift_ref[...], scale_ref[...], ng_ref[...]).astype(BF16)
    g_ref[...] = jnp.dot(hb, wg_ref[...], preferred_element_type=F32)
    z_ref[...] = jnp.dot(hb, wz_ref[...], preferred_element_type=F32)
    xbc_ref[...] = jnp.dot(hb, wx_ref[...], preferred_element_type=F32)
    dt = jnp.dot(hb, wdt_ref[...], preferred_element_type=F32)
    dtt_ref[...] = dt.T[:2 * H_SSD, :]


def _gzx_proj(x, mod, mod_row, norm_g, w_bf, w_xbc, w_dt, tm):
    B, L, _ = x.shape
    row = mod_row
    once = pl.Buffered(1)
    in_specs = [
        pl.BlockSpec((None, tm, D_MODEL), lambda b, i: (b, i, 0)),
        pl.BlockSpec((None, 1, D_MODEL), lambda b, i: (row(b), 0, 0)),
        pl.BlockSpec((None, 1, D_MODEL), lambda b, i: (row(b), 0, 1)),
        pl.BlockSpec((1, D_MODEL), lambda b, i: (0, 0)),
        pl.BlockSpec((D_MODEL, D_ATT), lambda b, i: (0, 3), pipeline_mode=once),
        pl.BlockSpec((D_MODEL, D_SSD), lambda b, i: (0, 4), pipeline_mode=once),
        pl.BlockSpec((D_MODEL, CONV_DIM), lambda b, i: (0, 0), pipeline_mode=once),
        pl.BlockSpec((D_MODEL, DT_PAD), lambda b, i: (0, 0), pipeline_mode=once),
    ]
    widths = (D_ATT, D_SSD, CONV_DIM)
    out_specs = [pl.BlockSpec((None, tm, w), lambda b, i: (b, i, 0)) for w in widths]
    out_shape = [jax.ShapeDtypeStruct((B, L, w), F32) for w in widths]
    out_specs.append(pl.BlockSpec((None, 2 * H_SSD, tm), lambda b, i: (b, 0, i)))
    out_shape.append(jax.ShapeDtypeStruct((B, 2 * H_SSD, L), F32))
    return pl.pallas_call(
        _gzx_kernel, grid=(B, L // tm), in_specs=in_specs, out_specs=out_specs, out_shape=out_shape,
        compiler_params=_cparams(("parallel", "parallel")),
        name="gzx_proj",
    )(x, mod, mod, norm_g, w_bf, w_bf, w_xbc, w_dt)


def _attn_kernel(*refs, n_src, lam_init, carry):
    qt_ref = refs[0]
    pos = 1
    if carry:
        qt_next_ref = refs[pos]
        pos += 1
    srcs = [(refs[pos + 2 * i], refs[pos + 1 + 2 * i]) for i in range(n_src)]
    pos += 2 * n_src
    g_ref, sg_ref, lq1_ref, lk1_ref, lq2_ref, lk2_ref, o_ref = refs[pos:pos + 7]
    if carry:
        s_sc, m_sc = refs[pos + 7:pos + 9]

    tq = ATT_TQ
    n_heads = qt_ref.shape[0] // HD_V
    chains = [(hh, t) for hh in range(n_heads) for t in range(qt_ref.shape[1] // tq)]
    n_ch = len(chains)
    row = lax.broadcasted_iota(jnp.int32, (HD_V, tq), 0)
    zero = jnp.zeros((HD_V, tq), BF16)

    def stacked_q(ref, hh, t):
        qt = ref[hh * HD_V:(hh + 1) * HD_V, t * tq:(t + 1) * tq]
        return jnp.concatenate([jnp.where(row < HD_QK, qt, zero), jnp.where(row >= HD_QK, qt, zero)], axis=1)

    q2t = [stacked_q(qt_ref, hh, t) for hh, t in chains]

    blocks = []
    for k_ref, vt_ref in srcs:
        n_keys = k_ref.shape[0]
        tk = min(ATT_TK, n_keys)
        blocks += [(k_ref, vt_ref, j * tk, tk) for j in range(n_keys // tk)]

    lam = (jnp.exp(jnp.sum(lq1_ref[...] * lk1_ref[...], axis=1, keepdims=True))
           - jnp.exp(jnp.sum(lq2_ref[...] * lk2_ref[...], axis=1, keepdims=True)) + lam_init)

    def score_piece(c, j, q_stacked=None):
        k_ref, _, off, tk = blocks[j]
        hh = chains[c][0]
        s = jnp.dot(k_ref[off:off + tk, hh * HD_V:(hh + 1) * HD_V],
                    q2t[c] if q_stacked is None else q_stacked, preferred_element_type=F32)
        return s, jnp.max(s, axis=0, keepdims=True)

    def prob_piece(c, j, s, m, l, acc):
        _, vt_ref, off, tk = blocks[j]
        hh = chains[c][0]
        p = jnp.exp2(s - m)
        l = l + jnp.sum(p, axis=0, keepdims=True)
        acc = acc + jnp.dot(vt_ref[hh * HD_V:(hh + 1) * HD_V, off:off + tk], p.astype(BF16),
                            preferred_element_type=F32)
        return l, acc

    def epilogue(c, l, acc):
        hh, t = chains[c]
        o_n = acc / l
        o = (o_n[:, :tq] - lam * o_n[:, tq:]).T
        ms = jnp.mean(o * o, axis=-1, keepdims=True)
        att = o * lax.rsqrt(ms + EPS) * sg_ref[...] * (1.0 - lam_init)
        gate = _silu(g_ref[t * tq:(t + 1) * tq, hh * HD_V:(hh + 1) * HD_V])
        o_ref[t * tq:(t + 1) * tq, hh * HD_V:(hh + 1) * HD_V] = (att * gate).astype(BF16)

    n_pc = len(blocks)
    neg_inf = jnp.full((1, 2 * tq), -jnp.inf, F32)
    if not carry:
        prev = None
        for c in range(n_ch + 1):
            cur_s, cur_m = [], neg_inf
            if prev is not None:
                l = jnp.zeros((1, 2 * tq), F32)
                acc = jnp.zeros((HD_V, 2 * tq), F32)
            for j in range(n_pc):
                if c < n_ch:
                    s, s_max = score_piece(c, j)
                    cur_s.append(s)
                    cur_m = jnp.maximum(cur_m, s_max)
                if prev is not None:
                    l, acc = prob_piece(prev[0], j, prev[1][j], prev[2], l, acc)
            if prev is not None:
                epilogue(prev[0], l, acc)
            prev = (c, cur_s, cur_m) if c < n_ch else None
        return

    dst = [sum(b[3] for b in blocks[:j]) for j in range(n_pc)]
    step = pl.program_id(2)
    cur, nxt = step % 2, (step + 1) % 2

    @pl.when(step == 0)
    def _():
        m0 = neg_inf
        for j in range(n_pc):
            s, s_max = score_piece(0, j)
            s_sc[0, dst[j]:dst[j] + blocks[j][3], :] = s
            m0 = jnp.maximum(m0, s_max)
        m_sc[0] = jnp.broadcast_to(m0, m_sc.shape[1:])

    hh0, t0 = chains[0]
    q_next = stacked_q(qt_next_ref, hh0, t0)
    prev_s, prev_m = None, m_sc[cur][0:1, :]
    for u in range(n_ch):
        l = jnp.zeros((1, 2 * tq), F32)
        acc = jnp.zeros((HD_V, 2 * tq), F32)
        nxt_s, nxt_m = [], neg_inf
        for j in range(n_pc):
            rows = slice(dst[j], dst[j] + blocks[j][3])
            if u + 1 < n_ch:
                s, s_max = score_piece(u + 1, j)
                nxt_s.append(s)
            else:
                s, s_max = score_piece(0, j, q_next)
                s_sc[nxt, rows, :] = s
            nxt_m = jnp.maximum(nxt_m, s_max)
            s_u = s_sc[cur, rows, :] if u == 0 else prev_s[j]
            l, acc = prob_piece(u, j, s_u, prev_m, l, acc)
        if u + 1 == n_ch:
            m_sc[nxt] = jnp.broadcast_to(nxt_m, m_sc.shape[1:])
        epilogue(u, l, acc)
        prev_s, prev_m = nxt_s, nxt_m


def _attention(qt, kvs, g, subln_g, lams, lam_init, tq, heads_per_step):
    B, _, L = qt.shape
    hw = heads_per_step * HD_V
    nq = L // tq
    carry = nq > 1
    in_specs = [pl.BlockSpec((None, hw, tq), lambda b, h, i: (b, h, i))]
    args = [qt]
    scratch = []
    if carry:
        in_specs.append(pl.BlockSpec((None, hw, tq), lambda b, h, i: (b, h, jnp.minimum(i + 1, nq - 1))))
        args.append(qt)
        n_all = sum(k.shape[1] for k, _ in kvs)
        scratch = [pltpu.VMEM((2, n_all, 2 * ATT_TQ), F32), pltpu.VMEM((2, 8, 2 * ATT_TQ), F32)]
    for k, vt in kvs:
        n_keys = k.shape[1]
        in_specs += [pl.BlockSpec((None, n_keys, hw), lambda b, h, i: (b, 0, h)),
                     pl.BlockSpec((None, hw, n_keys), lambda b, h, i: (b, h, 0))]
        args += [k, vt]
    in_specs += [pl.BlockSpec((None, tq, hw), lambda b, h, i: (b, i, h)),
                 pl.BlockSpec((1, HD_V), lambda b, h, i: (0, 0))]
    in_specs += [pl.BlockSpec((1, HD_QK), lambda b, h, i: (0, 0))] * 4
    args += [g, subln_g] + list(lams)
    return pl.pallas_call(
        functools.partial(_attn_kernel, n_src=len(kvs), lam_init=lam_init, carry=carry),
        grid=(B, H_ATT // heads_per_step, nq),
        in_specs=in_specs,
        out_specs=pl.BlockSpec((None, tq, hw), lambda b, h, i: (b, i, h)),
        out_shape=jax.ShapeDtypeStruct((B, L, D_ATT), BF16),
        scratch_shapes=scratch,
        compiler_params=_cparams(("parallel", "parallel", "arbitrary")),
        name="diff_attention",
    )(*args)


def _ssd_kernel(*refs, seq_len, has_h0):
    (x_ref, bc_ref, dtt_ref, cwx_ref, cwbc_ref, cbx_ref, cbbc_ref,
     dtb_ref, alog_ref, dsk_ref) = refs[:10]
    pos = 10
    if has_h0:
        h0_ref = refs[pos]
        pos += 1
    y_ref, hfin_ref, xt_sc, yt_sc, b_sc, ct_sc = refs[pos:pos + 6]

    nc = seq_len // CHUNK
    n_grp = x_ref.shape[1] // GROUP_W
    row_i = lax.broadcasted_iota(jnp.int32, (CHUNK, CHUNK), 0)
    col_i = lax.broadcasted_iota(jnp.int32, (CHUNK, CHUNK), 1)
    row1 = lax.broadcasted_iota(jnp.int32, (CHUNK, 1), 0)

    def conv_silu(src, w_ref, b_ref, c):
        r0 = pl.multiple_of(c * CHUNK, CHUNK)
        rp = pl.multiple_of(jnp.maximum(r0 - 8, 0), 8)
        rn = pl.multiple_of(jnp.minimum(r0 + CHUNK, seq_len - 8), 8)
        u = src[pl.ds(r0, CHUNK), :]
        pr = jnp.where(c > 0, src[pl.ds(rp, 8), :][7:8, :], 0.0)
        nx = jnp.where(c < nc - 1, src[pl.ds(rn, 8), :][0:1, :], 0.0)
        prev = jnp.where(row1 == 0, pr, pltpu.roll(u, 1, 0))
        nxt = jnp.where(row1 == CHUNK - 1, nx, pltpu.roll(u, CHUNK - 1, 0))
        w = 0.5 * w_ref[...]
        hx = 0.5 * b_ref[...] + prev * w[0:1, :] + u * w[1:2, :] + nxt * w[2:3, :]
        return hx + hx * jnp.tanh(hx)

    def conv_chunk(c, carry):
        xt = conv_silu(x_ref, cwx_ref, cbx_ref, c).T
        xt_sc[c] = xt
        yt_sc[c] = dsk_ref[...] * xt
        bc = conv_silu(bc_ref, cwbc_ref, cbbc_ref, c)
        for g in range(n_grp):
            bc_g = bc[:, g * LANES:(g + 1) * LANES]
            b_sc[c, g] = bc_g[:, :D_STATE].astype(BF16)
            ct_sc[c, g] = bc_g.T[D_STATE:, :].astype(BF16)
        return carry

    lax.fori_loop(0, nc, conv_chunk, 0, unroll=math.gcd(CONV_UNROLL, nc))

    dir_heads = 2 * HEADS_PER_GROUP
    dtb = [dtb_ref[g] for g in range(n_grp)]
    a_neg = [-jnp.exp(alog_ref[g]) * LOG2E for g in range(n_grp)]

    masks = (row_i <= col_i, row_i >= col_i)
    tri = tuple(jnp.where(m, 1.0, 0.0).astype(BF16) for m in masks)
    ends = (CHUNK - 1, 0)

    def scan_prep(items):
        out = []
        for d in range(2):
            sp_d = []
            for g, dd, c in items:
                if dd == d:
                    t0 = pl.multiple_of(c * CHUNK, CHUNK)
                    dt_g = dtt_ref[g * dir_heads:(g + 1) * dir_heads, pl.ds(t0, CHUNK)]
                    sp_d.append(_softplus(dt_g + dtb[g]))
            x = jnp.concatenate([s * a_neg[it[0]] for s, it in
                                 zip(sp_d, [it for it in items if it[1] == d])], axis=0)
            hi = x.astype(BF16)
            r1 = x - hi.astype(F32)
            mid = r1.astype(BF16)
            lo = (r1 - mid.astype(F32)).astype(BF16)
            parts = jnp.dot(jnp.concatenate([hi, mid, lo], axis=0), tri[d], preferred_element_type=F32)
            nr = x.shape[0]
            cum_d = parts[:nr] + parts[nr:2 * nr] + parts[2 * nr:]
            out.append((jnp.concatenate(sp_d, axis=0), cum_d, cum_d.T))
        return tuple(out)

    def scan_items(items, prep, states):
        n = len(items)
        sp, cum, cum_col, cb_t = [None] * n, [None] * n, [None] * n, [None] * n
        for d in range(2):
            sp_d, cum_d, cum_dt = prep[d]
            for i, k in enumerate([k for k, it in enumerate(items) if it[1] == d]):
                rows = slice(i * dir_heads, (i + 1) * dir_heads)
                sp[k], cum[k], cum_col[k] = sp_d[rows], cum_d[rows], cum_dt[:, rows]
        for k, (g, d, c) in enumerate(items):
            cb_t[k] = jnp.dot(b_sc[c, g], ct_sc[c, g], preferred_element_type=F32)
        xdt_b, lm_b, x_dec, y_scale, s_scale = [], [], [], [], []
        for k, (g, d, c) in enumerate(items):
            xt = xt_sc[c, g * GROUP_W:(g + 1) * GROUP_W, :]
            xdt_k, lm_k, x_dec_k, y_scale_k, s_scale_k = [], [], [], [], []
            for hh in range(HEADS_PER_GROUP):
                r = d * HEADS_PER_GROUP + hh
                a_row = cum[k][r:r + 1, :]
                seg = a_row - cum_col[k][:, r:r + 1]
                lmat = jnp.exp2(jnp.where(masks[d], seg, -jnp.inf))
                xdt = xt[hh * SSD_HEADDIM:(hh + 1) * SSD_HEADDIM, :] * sp[k][r:r + 1, :]
                a_end = a_row[:, ends[d]:ends[d] + 1]
                xdt_k.append(xdt.astype(BF16))
                lm_k.append((cb_t[k] * lmat).astype(BF16))
                x_dec_k.append((xdt * jnp.exp2(a_end - a_row)).astype(BF16))
                y_scale_k.append(jnp.broadcast_to(jnp.exp2(a_row), (SSD_HEADDIM, CHUNK)))
                s_scale_k.append(jnp.broadcast_to(jnp.exp2(a_end), (SSD_HEADDIM, D_STATE)))
            xdt_b.append(xdt_k)
            lm_b.append(lm_k)
            x_dec.append(jnp.concatenate(x_dec_k, axis=0))
            y_scale.append(jnp.concatenate(y_scale_k, axis=0))
            s_scale.append(jnp.concatenate(s_scale_k, axis=0))
        y_diag, st = [], []
        for k, (g, d, c) in enumerate(items):
            y_diag.append(jnp.concatenate(
                [jnp.dot(xdt_b[k][hh], lm_b[k][hh], preferred_element_type=F32)
                 for hh in range(HEADS_PER_GROUP)], axis=0))
            st.append(jnp.dot(x_dec[k], b_sc[c, g], preferred_element_type=F32))
        states = list(states)
        for k, (g, d, c) in enumerate(items):
            si = 2 * g + d
            rows = slice(g * GROUP_W, (g + 1) * GROUP_W)
            y_off = jnp.dot(states[si].astype(BF16), ct_sc[c, g], preferred_element_type=F32)
            yt_sc[c, rows, :] = yt_sc[c, rows, :] + y_diag[k] + y_off * y_scale[k]
            states[si] = states[si] * s_scale[k] + st[k]
        return tuple(states)

    if has_h0:
        s_init = tuple(h0_ref[d, g] for g in range(n_grp) for d in range(2))
    else:
        s_init = (jnp.zeros((GROUP_W, D_STATE), F32),) * (2 * n_grp)

    unroll = math.gcd(max(SCAN_UNROLL // n_grp, 1), nc)
    n_it = nc // unroll

    def items_of(i):
        items = []
        for u in range(unroll):
            for g in range(n_grp):
                items += [(g, 0, i * unroll + u), (g, 1, nc - 1 - (i * unroll + u))]
        return items

    def scan_step(i, carry):
        states, prep = carry
        prep_next = scan_prep(items_of(jnp.minimum(i + 1, n_it - 1)))
        return scan_items(items_of(i), prep, states), prep_next

    s_fin, _ = lax.fori_loop(0, n_it, scan_step, (s_init, scan_prep(items_of(0))))
    for g in range(n_grp):
        for d in range(2):
            hfin_ref[d, g] = s_fin[2 * g + d]

    def emit_chunk(c, carry):
        y_ref[pl.ds(pl.multiple_of(c * CHUNK, CHUNK), CHUNK), :] = yt_sc[c].T
        return carry

    lax.fori_loop(0, nc, emit_chunk, 0, unroll=math.gcd(CONV_UNROLL, nc))


def _ssd(xbc, dtt, conv_w, conv_b, dtb_g, alog_g, dsk, h0, n_grp):
    B, L, _ = xbc.shape
    nc = L // CHUNK
    xw, bw = n_grp * GROUP_W, n_grp * LANES
    bc_blk = D_SSD // bw
    dir_heads = 2 * HEADS_PER_GROUP
    state_spec = pl.BlockSpec((None, 2, n_grp, GROUP_W, D_STATE), lambda b, g: (b, 0, g, 0, 0))
    in_specs = [
        pl.BlockSpec((None, L, xw), lambda b, g: (b, 0, g)),
        pl.BlockSpec((None, L, bw), lambda b, g: (b, 0, bc_blk + g)),
        pl.BlockSpec((None, n_grp * dir_heads, L), lambda b, g: (b, g, 0)),
        pl.BlockSpec((3, xw), lambda b, g: (0, g)),
        pl.BlockSpec((3, bw), lambda b, g: (0, bc_blk + g)),
        pl.BlockSpec((1, xw), lambda b, g: (0, g)),
        pl.BlockSpec((1, bw), lambda b, g: (0, bc_blk + g)),
        pl.BlockSpec((n_grp, dir_heads, LANES), lambda b, g: (g, 0, 0)),
        pl.BlockSpec((n_grp, dir_heads, LANES), lambda b, g: (g, 0, 0)),
        pl.BlockSpec((xw, LANES), lambda b, g: (g, 0)),
    ]
    args = [xbc, xbc, dtt, conv_w, conv_w, conv_b, conv_b, dtb_g, alog_g, dsk]
    if h0 is not None:
        in_specs.append(state_spec)
        args.append(h0)
    return pl.pallas_call(
        functools.partial(_ssd_kernel, seq_len=L, has_h0=h0 is not None),
        grid=(B, SSD_GROUPS // n_grp),
        in_specs=in_specs,
        out_specs=[pl.BlockSpec((None, L, xw), lambda b, g: (b, 0, g)), state_spec],
        out_shape=[jax.ShapeDtypeStruct((B, L, D_SSD), F32),
                   jax.ShapeDtypeStruct((B, 2, SSD_GROUPS, GROUP_W, D_STATE), F32)],
        scratch_shapes=[pltpu.VMEM((nc, xw, CHUNK), F32), pltpu.VMEM((nc, xw, CHUNK), F32),
                        pltpu.VMEM((nc, n_grp, CHUNK, D_STATE), BF16),
                        pltpu.VMEM((nc, n_grp, D_STATE, CHUNK), BF16)],
        compiler_params=_cparams(("parallel", "parallel")),
        name="ssd_scan",
    )(*args)


def _out_kernel(x_ref, att_ref, y_ref, z_ref, sng_ref, w_ref, gate_ref, fg_ref, o_ref):
    tm = x_ref.shape[0]
    for r0 in range(0, tm, OUT_ROWS):
        rows = slice(r0, r0 + OUT_ROWS)
        yz = y_ref[rows, :] * _silu(z_ref[rows, :])
        ms = jnp.mean(yz * yz, axis=-1, keepdims=True)
        ssd_y = (yz * lax.rsqrt(ms + EPS) * sng_ref[...]).astype(BF16)
        out = jnp.dot(att_ref[rows, :], w_ref[0:D_ATT, :], preferred_element_type=F32)
        out = out + jnp.dot(ssd_y, w_ref[D_ATT:D_ATT + D_SSD, :], preferred_element_type=F32)
        r = x_ref[rows, :] + gate_ref[...] * out
        ms2 = jnp.mean(r * r, axis=-1, keepdims=True)
        o_ref[rows, :] = r * lax.rsqrt(ms2 + EPS) * fg_ref[...]


def _out_proj(x, att, y, z, ssd_norm_g, w_out, mod, mod_row, final_g, tm):
    B, L, _ = x.shape
    row = mod_row
    tok = lambda b, i: (b, i, 0)
    in_specs = [
        pl.BlockSpec((None, tm, D_MODEL), tok),
        pl.BlockSpec((None, tm, D_ATT), tok),
        pl.BlockSpec((None, tm, D_SSD), tok),
        pl.BlockSpec((None, tm, D_SSD), tok),
        pl.BlockSpec((1, D_SSD), lambda b, i: (0, 0)),
        pl.BlockSpec((D_ATT + D_SSD, D_MODEL), lambda b, i: (0, 0), pipeline_mode=pl.Buffered(1)),
        pl.BlockSpec((None, 1, D_MODEL), lambda b, i: (row(b), 0, 2)),
        pl.BlockSpec((1, D_MODEL), lambda b, i: (0, 0)),
    ]
    return pl.pallas_call(
        _out_kernel, grid=(B, L // tm), in_specs=in_specs,
        out_specs=pl.BlockSpec((None, tm, D_MODEL), tok),
        out_shape=jax.ShapeDtypeStruct((B, L, D_MODEL), F32),
        compiler_params=_cparams(("parallel", "parallel")),
        name="out_proj",
    )(x, att, y, z, ssd_norm_g, w_out, mod, final_g)


def _rope_tables(L):
    rows = L // GRID_W
    row_ids = jnp.repeat(jnp.arange(rows), GRID_W).astype(F32)
    col_ids = jnp.tile(jnp.arange(GRID_W), rows).astype(F32)
    inv = ROPE_BASE ** (-jnp.arange(ROPE_FREQS, dtype=F32) / ROPE_FREQS)
    ang_r = row_ids[:, None] * inv
    ang_c = col_ids[:, None] * inv
    cr, sr, cc, sc = jnp.cos(ang_r), jnp.sin(ang_r), jnp.cos(ang_c), jnp.sin(ang_c)
    cos64 = jnp.concatenate([cr, cr, cc, cc], axis=1)
    sin64 = jnp.concatenate([-sr, sr, -sc, sc], axis=1)
    return jnp.tile(cos64, (1, 2)), jnp.tile(sin64, (1, 2))


def _bc_perm():
    idx = list(range(D_SSD))
    for g in range(SSD_GROUPS):
        idx += [D_SSD + g * D_STATE + n for n in range(D_STATE)]
        idx += [D_SSD + SSD_GROUPS * D_STATE + g * D_STATE + n for n in range(D_STATE)]
    return jnp.asarray(idx, dtype=jnp.int32)


def _dt_perm():
    idx = []
    for g in range(SSD_GROUPS):
        for d in range(2):
            for hh in range(HEADS_PER_GROUP):
                idx.append(d * H_SSD + g * HEADS_PER_GROUP + hh)
    return jnp.asarray(idx, dtype=jnp.int32)


def _group_rows(v2h):
    t = v2h.reshape(2, SSD_GROUPS, HEADS_PER_GROUP).transpose(1, 0, 2).reshape(SSD_GROUPS, 2 * HEADS_PER_GROUP)
    return jnp.broadcast_to(t[:, :, None], (SSD_GROUPS, 2 * HEADS_PER_GROUP, LANES))


def _layer(x, mod, mod_row, params, rope_tabs, past_kv, h0, emit_f32, lam_init, final_g, tm, tq,
           heads_per_step, groups_per_step):
    (norm_g, w_bf, w_xbc, w_dt, lams, subln_g, conv_w, conv_b, dtb_g, alog_g, dsk, ssd_norm_g,
     w_out) = params
    proj = _qkv_proj(x, mod, mod_row, norm_g, w_bf, rope_tabs, emit_f32, tm)
    qt, k, vt = proj[:3]
    g, z, xbc, dtt = _gzx_proj(x, mod, mod_row, norm_g, w_bf, w_xbc, w_dt, tm)
    kvs = [(k, vt)]
    if past_kv is not None:
        kvs.append(past_kv)
    att = _attention(qt, kvs, g, subln_g, lams, lam_init, tq, heads_per_step)
    y, hfin = _ssd(xbc, dtt, conv_w, conv_b, dtb_g, alog_g, dsk, h0, groups_per_step)
    out = _out_proj(x, att, y, z, ssd_norm_g, w_out, mod, mod_row, final_g, tm)
    return out, proj[3:], hfin


def kernel(x_prompt, x_sample, cache_k, cache_v, state_ssd, c, c_ctx, w_mod, b_mod, norm_g, w_in,
           lambda_q1, lambda_k1, lambda_q2, lambda_k2, subln_g, conv_w, conv_b, dt_bias, A_log,
           D_skip, ssd_norm_g, w_out, final_g):
    b_ctx, l_ctx, _ = x_prompt.shape
    b_dec, l_dec, _ = x_sample.shape
    l_past = cache_k.shape[2]
    depth = w_mod.shape[0]
    assert depth == 1
    lam_init = 0.8 - 0.6 * math.exp(-0.3 * 0)

    w_bf = w_in[0].astype(BF16)
    o2 = 4 * D_ATT + D_SSD
    bc_perm = _bc_perm()
    w_xbc = w_bf[:, o2:o2 + CONV_DIM][:, bc_perm]
    w_dt = jnp.pad(w_bf[:, o2 + CONV_DIM:][:, _dt_perm()], ((0, 0), (0, DT_PAD - 2 * H_SSD)))
    conv_w_p = conv_w[0][:, bc_perm]
    conv_b_p = conv_b[0][bc_perm].reshape(1, CONV_DIM)
    dtb_g = _group_rows(dt_bias[0])
    alog_g = _group_rows(A_log[0])
    dsk = jnp.broadcast_to(jnp.repeat(D_skip[0], SSD_HEADDIM)[:, None], (D_SSD, LANES))
    lams = [a[0].reshape(1, HD_QK) for a in (lambda_q1, lambda_k1, lambda_q2, lambda_k2)]
    params = (norm_g[0].reshape(1, D_MODEL), w_bf, w_xbc, w_dt, lams, subln_g[0].reshape(1, HD_V),
              conv_w_p, conv_b_p, dtb_g, alog_g, dsk, ssd_norm_g[0].reshape(1, D_SSD),
              w_out[0].astype(BF16))
    fg = final_g.reshape(1, D_MODEL)

    cond8 = jnp.zeros((8, D_MODEL), F32).at[:b_dec].set(c).at[b_dec].set(c_ctx)
    mod = _adaln(cond8, w_mod[0], b_mod[0]).reshape(8, 1, 3 * D_MODEL)

    y_prompt, (k_ctx, v_ctx), h_ctx = _layer(
        x_prompt, mod, lambda b: b_dec, params, None, None, None, True, lam_init, fg,
        tm=256, tq=256, heads_per_step=H_ATT, groups_per_step=SSD_GROUPS)

    rope_tabs = _rope_tables(l_dec)
    k_past = cache_k[:, 0].reshape(b_dec, l_past, D_ATT).astype(BF16)
    vt_past = cache_v[:, 0].reshape(b_dec, l_past, D_ATT).transpose(0, 2, 1).astype(BF16)
    h0 = state_ssd[:, 0].reshape(b_dec, 2, SSD_GROUPS, GROUP_W, D_STATE)
    y_sample, _, _ = _layer(
        x_sample, mod, lambda b: b, params, rope_tabs, (k_past, vt_past), h0, False, lam_init, fg,
        tm=512, tq=512, heads_per_step=1, groups_per_step=1)

    new_cache_k = k_ctx.reshape(b_ctx, 1, l_ctx, H_ATT, 2 * HD_QK)
    new_cache_v = v_ctx.reshape(b_ctx, 1, l_ctx, H_ATT, HD_V)
    new_state = h_ctx.reshape(b_ctx, 1, 2, H_SSD, SSD_HEADDIM, D_STATE)
    return (y_prompt, y_sample, new_cache_k, new_cache_v, new_state)
```

```python
import functools
import math

import jax
import jax.numpy as jnp
from jax import lax
from jax.experimental import pallas as pl
from jax.experimental.pallas import tpu as pltpu

F32 = jnp.float32
BF16 = jnp.bfloat16

D_MODEL = 1024
GRID_W = 64
HD_QK = 64
HD_V = 128
H_ATT = 8
D_ATT = 1024
D_SSD = 1024
SSD_HEADDIM = 64
H_SSD = 16
SSD_GROUPS = 4
HEADS_PER_GROUP = 4
D_STATE = 64
CONV_DIM = D_SSD + 2 * SSD_GROUPS * D_STATE
CHUNK = 128
ROPE_BASE = 10000.0
ROPE_FREQS = 16
EPS = 1e-6

LANES = 128
DT_PAD = LANES
GROUP_W = HEADS_PER_GROUP * SSD_HEADDIM
VMEM_LIMIT = 56 * 1024 * 1024
ATT_TK = 512
ATT_TQ = 256
OUT_ROWS = 256
LOG2E = 1.4426950408889634
CONV_UNROLL = 4
SCAN_UNROLL = 8


def _cparams(sem, flags=None):
    return pltpu.CompilerParams(dimension_semantics=sem, vmem_limit_bytes=VMEM_LIMIT, flags=flags)


def _silu(x):
    hx = 0.5 * x
    return hx + hx * jnp.tanh(hx)


def _softplus(x):
    return jnp.maximum(x, 0.0) + jnp.log1p(jnp.exp(-jnp.abs(x)))


def _adaln_kernel(cond_ref, w_ref, b_ref, o_ref):
    cond = cond_ref[...]
    s = _silu(cond).astype(BF16)
    o_ref[...] = jnp.dot(s, w_ref[...].astype(BF16), preferred_element_type=F32) + b_ref[...]


def _adaln(cond8, w_mod, b_mod):
    n = w_mod.shape[1]
    blk = 1024
    return pl.pallas_call(
        _adaln_kernel,
        grid=(n // blk,),
        in_specs=[pl.BlockSpec((8, D_MODEL), lambda j: (0, 0)),
                  pl.BlockSpec((D_MODEL, blk), lambda j: (0, j)),
                  pl.BlockSpec((1, blk), lambda j: (0, j))],
        out_specs=pl.BlockSpec((8, blk), lambda j: (0, j)),
        out_shape=jax.ShapeDtypeStruct((8, n), F32),
        compiler_params=_cparams(("arbitrary",)),
        name="adaln",
    )(cond8, w_mod, b_mod.reshape(1, n))


def _mod_norm(x, shift, scale, g):
    ms = jnp.mean(x * x, axis=-1, keepdims=True)
    y = x * lax.rsqrt(ms + EPS) * g
    return y * (1.0 + scale) + shift


def _rope(t, cos, sin_signed, first):
    outs = []
    for h in range(H_ATT):
        th = t[:, h * LANES:(h + 1) * LANES]
        partner = jnp.where(first, pltpu.roll(th, LANES - 16, 1), pltpu.roll(th, 16, 1))
        outs.append(th * cos + partner * sin_signed)
    return jnp.concatenate(outs, axis=1)


def _qkv_kernel(*refs, rope, emit_f32):
    x_ref, shift_ref, scale_ref, ng_ref, w_ref = refs[:5]
    pos = 5
    if rope:
        cos_ref, sin_ref = refs[pos:pos + 2]
        pos += 2
    qt_ref, k_ref, vt_ref = refs[pos:pos + 3]
    pos += 3
    if emit_f32:
        k32_ref, v32_ref = refs[pos:pos + 2]

    hb = _mod_norm(x_ref[...], shift_ref[...], scale_ref[...], ng_ref[...]).astype(BF16)
    q = jnp.dot(hb, w_ref[:, 0:D_ATT], preferred_element_type=F32)
    k = jnp.dot(hb, w_ref[:, D_ATT:2 * D_ATT], preferred_element_type=F32)
    v = jnp.dot(hb, w_ref[:, 2 * D_ATT:3 * D_ATT], preferred_element_type=F32)
    if rope:
        cos = cos_ref[...]
        sin_signed = sin_ref[...]
        lane = lax.broadcasted_iota(jnp.int32, cos.shape, 1)
        first = (lane % 32) < 16
        q = _rope(q, cos, sin_signed, first)
        k = _rope(k, cos, sin_signed, first)
    qt_ref[...] = (q * (LOG2E / math.sqrt(HD_QK))).T.astype(BF16)
    k_ref[...] = k.astype(BF16)
    vt_ref[...] = v.T.astype(BF16)
    if emit_f32:
        k32_ref[...] = k
        v32_ref[...] = v


def _qkv_proj(x, mod, mod_row, norm_g, w_bf, rope_tabs, emit_f32, tm):
    B, L, _ = x.shape
    grid = (B, L // tm)
    row = mod_row

    in_specs = [
        pl.BlockSpec((None, tm, D_MODEL), lambda b, i: (b, i, 0)),
        pl.BlockSpec((None, 1, D_MODEL), lambda b, i: (row(b), 0, 0)),
        pl.BlockSpec((None, 1, D_MODEL), lambda b, i: (row(b), 0, 1)),
        pl.BlockSpec((1, D_MODEL), lambda b, i: (0, 0)),
        pl.BlockSpec((D_MODEL, 3 * D_ATT), lambda b, i: (0, 0), pipeline_mode=pl.Buffered(1)),
    ]
    args = [x, mod, mod, norm_g, w_bf]
    if rope_tabs is not None:
        in_specs += [pl.BlockSpec((tm, LANES), lambda b, i: (i, 0)),
                     pl.BlockSpec((tm, LANES), lambda b, i: (i, 0))]
        args += list(rope_tabs)
    out_specs = [
        pl.BlockSpec((None, D_ATT, tm), lambda b, i: (b, 0, i)),
        pl.BlockSpec((None, tm, D_ATT), lambda b, i: (b, i, 0)),
        pl.BlockSpec((None, D_ATT, tm), lambda b, i: (b, 0, i)),
    ]
    out_shape = [
        jax.ShapeDtypeStruct((B, D_ATT, L), BF16),
        jax.ShapeDtypeStruct((B, L, D_ATT), BF16),
        jax.ShapeDtypeStruct((B, D_ATT, L), BF16),
    ]
    if emit_f32:
        out_specs += [pl.BlockSpec((None, tm, D_ATT), lambda b, i: (b, i, 0))] * 2
        out_shape += [jax.ShapeDtypeStruct((B, L, D_ATT), F32)] * 2
    return pl.pallas_call(
        functools.partial(_qkv_kernel, rope=rope_tabs is not None, emit_f32=emit_f32),
        grid=grid, in_specs=in_specs, out_specs=out_specs, out_shape=out_shape,
        compiler_params=_cparams(("parallel", "parallel")),
        name="qkv_proj",
    )(*args)


def _gzx_kernel(x_ref, shift_ref, scale_ref, ng_ref, wg_ref, wz_ref, wx_ref, wdt_ref,
                g_ref, z_ref, xbc_ref, dtt_ref):
    hb = _mod_norm(x_ref[...], shift_ref[...], scale_ref[...], ng_ref[...]).astype(BF16)
    g_ref[...] = jnp.dot(hb, wg_ref[...], preferred_element_type=F32)
    z_ref[...] = jnp.dot(hb, wz_ref[...], preferred_element_type=F32)
    xbc_ref[...] = jnp.dot(hb, wx_ref[...], preferred_element_type=F32)
    dt = jnp.dot(hb, wdt_ref[...], preferred_element_type=F32)
    dtt_ref[...] = dt.T[:2 * H_SSD, :]


def _gzx_proj(x, mod, mod_row, norm_g, w_bf, w_xbc, w_dt, tm):
    B, L, _ = x.shape
    row = mod_row
    once = pl.Buffered(1)
    in_specs = [
        pl.BlockSpec((None, tm, D_MODEL), lambda b, i: (b, i, 0)),
        pl.BlockSpec((None, 1, D_MODEL), lambda b, i: (row(b), 0, 0)),
        pl.BlockSpec((None, 1, D_MODEL), lambda b, i: (row(b), 0, 1)),
        pl.BlockSpec((1, D_MODEL), lambda b, i: (0, 0)),
        pl.BlockSpec((D_MODEL, D_ATT), lambda b, i: (0, 3), pipeline_mode=once),
        pl.BlockSpec((D_MODEL, D_SSD), lambda b, i: (0, 4), pipeline_mode=once),
        pl.BlockSpec((D_MODEL, CONV_DIM), lambda b, i: (0, 0), pipeline_mode=once),
        pl.BlockSpec((D_MODEL, DT_PAD), lambda b, i: (0, 0), pipeline_mode=once),
    ]
    widths = (D_ATT, D_SSD, CONV_DIM)
    out_specs = [pl.BlockSpec((None, tm, w), lambda b, i: (b, i, 0)) for w in widths]
    out_shape = [jax.ShapeDtypeStruct((B, L, w), F32) for w in widths]
    out_specs.append(pl.BlockSpec((None, 2 * H_SSD, tm), lambda b, i: (b, 0, i)))
    out_shape.append(jax.ShapeDtypeStruct((B, 2 * H_SSD, L), F32))
    return pl.pallas_call(
        _gzx_kernel, grid=(B, L // tm), in_specs=in_specs, out_specs=out_specs, out_shape=out_shape,
        compiler_params=_cparams(("parallel", "parallel")),
        name="gzx_proj",
    )(x, mod, mod, norm_g, w_bf, w_bf, w_xbc, w_dt)


def _attn_kernel(*refs, n_src, lam_init, carry):
    qt_ref = refs[0]
    pos = 1
    if carry:
        qt_next_ref = refs[pos]
        k_next_refs = refs[pos + 1:pos + 1 + n_src]
        pos += 1 + n_src
    srcs = [(refs[pos + 2 * i], refs[pos + 1 + 2 * i]) for i in range(n_src)]
    pos += 2 * n_src
    g_ref, sg_ref, lq1_ref, lk1_ref, lq2_ref, lk2_ref, o_ref = refs[pos:pos + 7]
    if carry:
        s_sc, m_sc = refs[pos + 7:pos + 9]

    tq = ATT_TQ
    n_heads = qt_ref.shape[0] // HD_V
    chains = [(hh, t) for hh in range(n_heads) for t in range(qt_ref.shape[1] // tq)]
    n_ch = len(chains)
    row = lax.broadcasted_iota(jnp.int32, (HD_V, tq), 0)
    zero = jnp.zeros((HD_V, tq), BF16)

    def stacked_q(ref, hh, t):
        qt = ref[hh * HD_V:(hh + 1) * HD_V, t * tq:(t + 1) * tq]
        return jnp.concatenate([jnp.where(row < HD_QK, qt, zero), jnp.where(row >= HD_QK, qt, zero)], axis=1)

    q2t = [stacked_q(qt_ref, hh, t) for hh, t in chains]

    blocks = []
    for si, (k_ref, vt_ref) in enumerate(srcs):
        n_keys = k_ref.shape[0]
        tk = min(ATT_TK, n_keys)
        blocks += [(k_ref, vt_ref, j * tk, tk, si) for j in range(n_keys // tk)]

    lam = (jnp.exp(jnp.sum(lq1_ref[...] * lk1_ref[...], axis=1, keepdims=True))
           - jnp.exp(jnp.sum(lq2_ref[...] * lk2_ref[...], axis=1, keepdims=True)) + lam_init)

    def score_piece(c, j, handover=None):
        k_ref, _, off, tk, si = blocks[j]
        hh = chains[c][0]
        if handover is not None:
            k_ref = k_next_refs[si]
        s = jnp.dot(k_ref[off:off + tk, hh * HD_V:(hh + 1) * HD_V],
                    q2t[c] if handover is None else handover, preferred_element_type=F32)
        return s, jnp.max(s, axis=0, keepdims=True)

    def prob_piece(c, j, s, m, l, acc):
        _, vt_ref, off, tk, _ = blocks[j]
        hh = chains[c][0]
        p = jnp.exp2(s - m)
        l = l + jnp.sum(p, axis=0, keepdims=True)
        acc = acc + jnp.dot(vt_ref[hh * HD_V:(hh + 1) * HD_V, off:off + tk], p.astype(BF16),
                            preferred_element_type=F32)
        return l, acc

    def epilogue(c, l, acc):
        hh, t = chains[c]
        o_n = acc / l
        o = (o_n[:, :tq] - lam * o_n[:, tq:]).T
        ms = jnp.mean(o * o, axis=-1, keepdims=True)
        att = o * lax.rsqrt(ms + EPS) * sg_ref[...] * (1.0 - lam_init)
        gate = _silu(g_ref[t * tq:(t + 1) * tq, hh * HD_V:(hh + 1) * HD_V])
        o_ref[t * tq:(t + 1) * tq, hh * HD_V:(hh + 1) * HD_V] = (att * gate).astype(BF16)

    n_pc = len(blocks)
    neg_inf = jnp.full((1, 2 * tq), -jnp.inf, F32)
    if not carry:
        prev = None
        for c in range(n_ch + 1):
            cur_s, cur_m = [], neg_inf
            if prev is not None:
                l = jnp.zeros((1, 2 * tq), F32)
                acc = jnp.zeros((HD_V, 2 * tq), F32)
            for j in range(n_pc):
                if c < n_ch:
                    s, s_max = score_piece(c, j)
                    cur_s.append(s)
                    cur_m = jnp.maximum(cur_m, s_max)
                if prev is not None:
                    l, acc = prob_piece(prev[0], j, prev[1][j], prev[2], l, acc)
            if prev is not None:
                epilogue(prev[0], l, acc)
            prev = (c, cur_s, cur_m) if c < n_ch else None
        return

    dst = [sum(b[3] for b in blocks[:j]) for j in range(n_pc)]
    step = ((pl.program_id(0) * pl.num_programs(1) + pl.program_id(1)) * pl.num_programs(2)
            + pl.program_id(2))
    cur, nxt = step % 2, (step + 1) % 2

    @pl.when(step == 0)
    def _():
        m0 = neg_inf
        for j in range(n_pc):
            s, s_max = score_piece(0, j)
            s_sc[0, dst[j]:dst[j] + blocks[j][3], :] = s
            m0 = jnp.maximum(m0, s_max)
        m_sc[0] = jnp.broadcast_to(m0, m_sc.shape[1:])

    hh0, t0 = chains[0]
    q_next = stacked_q(qt_next_ref, hh0, t0)
    prev_s, prev_m = None, m_sc[cur][0:1, :]
    for u in range(n_ch):
        l = jnp.zeros((1, 2 * tq), F32)
        acc = jnp.zeros((HD_V, 2 * tq), F32)
        nxt_s, nxt_m = [], neg_inf
        for j in range(n_pc):
            rows = slice(dst[j], dst[j] + blocks[j][3])
            if u + 1 < n_ch:
                s, s_max = score_piece(u + 1, j)
                nxt_s.append(s)
            else:
                s, s_max = score_piece(0, j, q_next)
                s_sc[nxt, rows, :] = s
            nxt_m = jnp.maximum(nxt_m, s_max)
            s_u = s_sc[cur, rows, :] if u == 0 else prev_s[j]
            l, acc = prob_piece(u, j, s_u, prev_m, l, acc)
        if u + 1 == n_ch:
            m_sc[nxt] = jnp.broadcast_to(nxt_m, m_sc.shape[1:])
        epilogue(u, l, acc)
        prev_s, prev_m = nxt_s, nxt_m


def _attention(qt, kvs, g, subln_g, lams, lam_init, tq, heads_per_step):
    B, _, L = qt.shape
    hw = heads_per_step * HD_V
    nq = L // tq
    nh = H_ATT // heads_per_step
    carry = nq > 1
    in_specs = [pl.BlockSpec((None, hw, tq), lambda b, h, i: (b, h, i))]
    args = [qt]
    scratch = []
    if carry:
        def succ(b, h, i):
            end_i, end_h = i == nq - 1, h == nh - 1
            last = end_i & end_h & (b == B - 1)
            b1 = jnp.where(end_i & end_h & ~last, b + 1, b)
            h1 = jnp.where(last, h, jnp.where(end_i, jnp.where(end_h, 0, h + 1), h))
            i1 = jnp.where(last, i, jnp.where(end_i, 0, i + 1))
            return b1, h1, i1

        def q_next(b, h, i):
            b1, h1, i1 = succ(b, h, i)
            return b1, h1, i1

        def k_next(b, h, i):
            b1, h1, _ = succ(b, h, i)
            return b1, 0, h1

        in_specs.append(pl.BlockSpec((None, hw, tq), q_next))
        args.append(qt)
        for k, _ in kvs:
            in_specs.append(pl.BlockSpec((None, k.shape[1], hw), k_next))
            args.append(k)
        n_all = sum(k.shape[1] for k, _ in kvs)
        scratch = [pltpu.VMEM((2, n_all, 2 * ATT_TQ), F32), pltpu.VMEM((2, 8, 2 * ATT_TQ), F32)]
    for k, vt in kvs:
        n_keys = k.shape[1]
        in_specs += [pl.BlockSpec((None, n_keys, hw), lambda b, h, i: (b, 0, h)),
                     pl.BlockSpec((None, hw, n_keys), lambda b, h, i: (b, h, 0))]
        args += [k, vt]
    in_specs += [pl.BlockSpec((None, tq, hw), lambda b, h, i: (b, i, h)),
                 pl.BlockSpec((1, HD_V), lambda b, h, i: (0, 0))]
    in_specs += [pl.BlockSpec((1, HD_QK), lambda b, h, i: (0, 0))] * 4
    args += [g, subln_g] + list(lams)
    return pl.pallas_call(
        functools.partial(_attn_kernel, n_src=len(kvs), lam_init=lam_init, carry=carry),
        grid=(B, nh, nq),
        in_specs=in_specs,
        out_specs=pl.BlockSpec((None, tq, hw), lambda b, h, i: (b, i, h)),
        out_shape=jax.ShapeDtypeStruct((B, L, D_ATT), BF16),
        scratch_shapes=scratch,
        compiler_params=_cparams(("arbitrary",) * 3 if carry else ("parallel", "parallel", "arbitrary")),
        name="diff_attention",
    )(*args)


def _ssd_kernel(*refs, seq_len, has_h0):
    (x_ref, bc_ref, dtt_ref, cwx_ref, cwbc_ref, cbx_ref, cbbc_ref,
     dtb_ref, alog_ref, dsk_ref) = refs[:10]
    pos = 10
    if has_h0:
        h0_ref = refs[pos]
        pos += 1
    y_ref, hfin_ref, xt_sc, yt_sc, b_sc, ct_sc = refs[pos:pos + 6]

    nc = seq_len // CHUNK
    n_grp = x_ref.shape[1] // GROUP_W
    row_i = lax.broadcasted_iota(jnp.int32, (CHUNK, CHUNK), 0)
    col_i = lax.broadcasted_iota(jnp.int32, (CHUNK, CHUNK), 1)
    row1 = lax.broadcasted_iota(jnp.int32, (CHUNK, 1), 0)

    def conv_silu(src, w_ref, b_ref, c):
        r0 = pl.multiple_of(c * CHUNK, CHUNK)
        rp = pl.multiple_of(jnp.maximum(r0 - 8, 0), 8)
        rn = pl.multiple_of(jnp.minimum(r0 + CHUNK, seq_len - 8), 8)
        u = src[pl.ds(r0, CHUNK), :]
        pr = jnp.where(c > 0, src[pl.ds(rp, 8), :][7:8, :], 0.0)
        nx = jnp.where(c < nc - 1, src[pl.ds(rn, 8), :][0:1, :], 0.0)
        prev = jnp.where(row1 == 0, pr, pltpu.roll(u, 1, 0))
        nxt = jnp.where(row1 == CHUNK - 1, nx, pltpu.roll(u, CHUNK - 1, 0))
        w = 0.5 * w_ref[...]
        hx = 0.5 * b_ref[...] + prev * w[0:1, :] + u * w[1:2, :] + nxt * w[2:3, :]
        return hx + hx * jnp.tanh(hx)

    def conv_chunk(c, carry):
        xt = conv_silu(x_ref, cwx_ref, cbx_ref, c).T
        xt_sc[c] = xt
        yt_sc[c] = dsk_ref[...] * xt
        bc = conv_silu(bc_ref, cwbc_ref, cbbc_ref, c)
        for g in range(n_grp):
            bc_g = bc[:, g * LANES:(g + 1) * LANES]
            b_sc[c, g] = bc_g[:, :D_STATE].astype(BF16)
            ct_sc[c, g] = bc_g.T[D_STATE:, :].astype(BF16)
        return carry

    lax.fori_loop(0, nc, conv_chunk, 0, unroll=math.gcd(CONV_UNROLL, nc))

    dir_heads = 2 * HEADS_PER_GROUP
    dtb = [dtb_ref[g] for g in range(n_grp)]
    a_neg = [-jnp.exp(alog_ref[g]) * LOG2E for g in range(n_grp)]

    masks = (row_i <= col_i, row_i >= col_i)
    tri = tuple(jnp.where(m, 1.0, 0.0).astype(BF16) for m in masks)
    ends = (CHUNK - 1, 0)

    def scan_prep(items):
        out = []
        for d in range(2):
            sp_d = []
            for g, dd, c in items:
                if dd == d:
                    t0 = pl.multiple_of(c * CHUNK, CHUNK)
                    dt_g = dtt_ref[g * dir_heads:(g + 1) * dir_heads, pl.ds(t0, CHUNK)]
                    sp_d.append(_softplus(dt_g + dtb[g]))
            x = jnp.concatenate([s * a_neg[it[0]] for s, it in
                                 zip(sp_d, [it for it in items if it[1] == d])], axis=0)
            hi = x.astype(BF16)
            r1 = x - hi.astype(F32)
            mid = r1.astype(BF16)
            lo = (r1 - mid.astype(F32)).astype(BF16)
            parts = jnp.dot(jnp.concatenate([hi, mid, lo], axis=0), tri[d], preferred_element_type=F32)
            nr = x.shape[0]
            cum_d = parts[:nr] + parts[nr:2 * nr] + parts[2 * nr:]
            out.append((jnp.concatenate(sp_d, axis=0), cum_d, cum_d.T))
        return tuple(out)

    def scan_items(items, prep, states):
        n = len(items)
        sp, cum, cum_col, cb_t = [None] * n, [None] * n, [None] * n, [None] * n
        for d in range(2):
            sp_d, cum_d, cum_dt = prep[d]
            for i, k in enumerate([k for k, it in enumerate(items) if it[1] == d]):
                rows = slice(i * dir_heads, (i + 1) * dir_heads)
                sp[k], cum[k], cum_col[k] = sp_d[rows], cum_d[rows], cum_dt[:, rows]
        for k, (g, d, c) in enumerate(items):
            cb_t[k] = jnp.dot(b_sc[c, g], ct_sc[c, g], preferred_element_type=F32)
        xdt_b, lm_b, x_dec, y_scale, s_scale = [], [], [], [], []
        for k, (g, d, c) in enumerate(items):
            xt = xt_sc[c, g * GROUP_W:(g + 1) * GROUP_W, :]
            xdt_k, lm_k, x_dec_k, y_scale_k, s_scale_k = [], [], [], [], []
            for hh in range(HEADS_PER_GROUP):
                r = d * HEADS_PER_GROUP + hh
                a_row = cum[k][r:r + 1, :]
                seg = a_row - cum_col[k][:, r:r + 1]
                lmat = jnp.exp2(jnp.where(masks[d], seg, -jnp.inf))
                xdt = xt[hh * SSD_HEADDIM:(hh + 1) * SSD_HEADDIM, :] * sp[k][r:r + 1, :]
                a_end = a_row[:, ends[d]:ends[d] + 1]
                xdt_k.append(xdt.astype(BF16))
                lm_k.append((cb_t[k] * lmat).astype(BF16))
                x_dec_k.append((xdt * jnp.exp2(a_end - a_row)).astype(BF16))
                y_scale_k.append(jnp.broadcast_to(jnp.exp2(a_row), (SSD_HEADDIM, CHUNK)))
                s_scale_k.append(jnp.broadcast_to(jnp.exp2(a_end), (SSD_HEADDIM, D_STATE)))
            xdt_b.append(xdt_k)
            lm_b.append(lm_k)
            x_dec.append(jnp.concatenate(x_dec_k, axis=0))
            y_scale.append(jnp.concatenate(y_scale_k, axis=0))
            s_scale.append(jnp.concatenate(s_scale_k, axis=0))
        y_diag, st = [], []
        for k, (g, d, c) in enumerate(items):
            y_diag.append(jnp.concatenate(
                [jnp.dot(xdt_b[k][hh], lm_b[k][hh], preferred_element_type=F32)
                 for hh in range(HEADS_PER_GROUP)], axis=0))
            st.append(jnp.dot(x_dec[k], b_sc[c, g], preferred_element_type=F32))
        states = list(states)
        for k, (g, d, c) in enumerate(items):
            si = 2 * g + d
            rows = slice(g * GROUP_W, (g + 1) * GROUP_W)
            y_off = jnp.dot(states[si].astype(BF16), ct_sc[c, g], preferred_element_type=F32)
            yt_sc[c, rows, :] = yt_sc[c, rows, :] + y_diag[k] + y_off * y_scale[k]
            states[si] = states[si] * s_scale[k] + st[k]
        return tuple(states)

    if has_h0:
        s_init = tuple(h0_ref[d, g] for g in range(n_grp) for d in range(2))
    else:
        s_init = (jnp.zeros((GROUP_W, D_STATE), F32),) * (2 * n_grp)

    unroll = math.gcd(max(SCAN_UNROLL // n_grp, 1), nc)
    n_it = nc // unroll

    def items_of(i):
        items = []
        for u in range(unroll):
            for g in range(n_grp):
                items += [(g, 0, i * unroll + u), (g, 1, nc - 1 - (i * unroll + u))]
        return items

    def scan_step(i, carry):
        states, prep = carry
        prep_next = scan_prep(items_of(jnp.minimum(i + 1, n_it - 1)))
        return scan_items(items_of(i), prep, states), prep_next

    s_fin, _ = lax.fori_loop(0, n_it, scan_step, (s_init, scan_prep(items_of(0))))
    for g in range(n_grp):
        for d in range(2):
            hfin_ref[d, g] = s_fin[2 * g + d]

    def emit_chunk(c, carry):
        y_ref[pl.ds(pl.multiple_of(c * CHUNK, CHUNK), CHUNK), :] = yt_sc[c].T
        return carry

    lax.fori_loop(0, nc, emit_chunk, 0, unroll=math.gcd(CONV_UNROLL, nc))


def _ssd(xbc, dtt, conv_w, conv_b, dtb_g, alog_g, dsk, h0, n_grp):
    B, L, _ = xbc.shape
    nc = L // CHUNK
    xw, bw = n_grp * GROUP_W, n_grp * LANES
    bc_blk = D_SSD // bw
    dir_heads = 2 * HEADS_PER_GROUP
    state_spec = pl.BlockSpec((None, 2, n_grp, GROUP_W, D_STATE), lambda b, g: (b, 0, g, 0, 0))
    in_specs = [
        pl.BlockSpec((None, L, xw), lambda b, g: (b, 0, g)),
        pl.BlockSpec((None, L, bw), lambda b, g: (b, 0, bc_blk + g)),
        pl.BlockSpec((None, n_grp * dir_heads, L), lambda b, g: (b, g, 0)),
        pl.BlockSpec((3, xw), lambda b, g: (0, g)),
        pl.BlockSpec((3, bw), lambda b, g: (0, bc_blk + g)),
        pl.BlockSpec((1, xw), lambda b, g: (0, g)),
        pl.BlockSpec((1, bw), lambda b, g: (0, bc_blk + g)),
        pl.BlockSpec((n_grp, dir_heads, LANES), lambda b, g: (g, 0, 0)),
        pl.BlockSpec((n_grp, dir_heads, LANES), lambda b, g: (g, 0, 0)),
        pl.BlockSpec((xw, LANES), lambda b, g: (g, 0)),
    ]
    args = [xbc, xbc, dtt, conv_w, conv_w, conv_b, conv_b, dtb_g, alog_g, dsk]
    if h0 is not None:
        in_specs.append(state_spec)
        args.append(h0)
    return pl.pallas_call(
        functools.partial(_ssd_kernel, seq_len=L, has_h0=h0 is not None),
        grid=(B, SSD_GROUPS // n_grp),
        in_specs=in_specs,
        out_specs=[pl.BlockSpec((None, L, xw), lambda b, g: (b, 0, g)), state_spec],
        out_shape=[jax.ShapeDtypeStruct((B, L, D_SSD), F32),
                   jax.ShapeDtypeStruct((B, 2, SSD_GROUPS, GROUP_W, D_STATE), F32)],
        scratch_shapes=[pltpu.VMEM((nc, xw, CHUNK), F32), pltpu.VMEM((nc, xw, CHUNK), F32),
                        pltpu.VMEM((nc, n_grp, CHUNK, D_STATE), BF16),
                        pltpu.VMEM((nc, n_grp, D_STATE, CHUNK), BF16)],
        compiler_params=_cparams(("parallel", "parallel")),
        name="ssd_scan",
    )(*args)


def _out_kernel(x_ref, att_ref, y_ref, z_ref, sng_ref, w_ref, gate_ref, fg_ref, o_ref):
    n_sub = x_ref.shape[0] // OUT_ROWS

    def prologue(r):
        rows = slice(r * OUT_ROWS, (r + 1) * OUT_ROWS)
        yz = y_ref[rows, :] * _silu(z_ref[rows, :])
        ms = jnp.mean(yz * yz, axis=-1, keepdims=True)
        return (yz * lax.rsqrt(ms + EPS) * sng_ref[...]).astype(BF16)

    ssd_y = prologue(0)
    for r in range(n_sub):
        rows = slice(r * OUT_ROWS, (r + 1) * OUT_ROWS)
        ssd_next = prologue(r + 1) if r + 1 < n_sub else None
        out = jnp.dot(att_ref[rows, :], w_ref[0:D_ATT, :], preferred_element_type=F32)
        out = out + jnp.dot(ssd_y, w_ref[D_ATT:D_ATT + D_SSD, :], preferred_element_type=F32)
        res = x_ref[rows, :] + gate_ref[...] * out
        ms2 = jnp.mean(res * res, axis=-1, keepdims=True)
        o_ref[rows, :] = res * lax.rsqrt(ms2 + EPS) * fg_ref[...]
        ssd_y = ssd_next


def _out_proj(x, att, y, z, ssd_norm_g, w_out, mod, mod_row, final_g, tm):
    B, L, _ = x.shape
    row = mod_row
    tok = lambda b, i: (b, i, 0)
    in_specs = [
        pl.BlockSpec((None, tm, D_MODEL), tok),
        pl.BlockSpec((None, tm, D_ATT), tok),
        pl.BlockSpec((None, tm, D_SSD), tok),
        pl.BlockSpec((None, tm, D_SSD), tok),
        pl.BlockSpec((1, D_SSD), lambda b, i: (0, 0)),
        pl.BlockSpec((D_ATT + D_SSD, D_MODEL), lambda b, i: (0, 0), pipeline_mode=pl.Buffered(1)),
        pl.BlockSpec((None, 1, D_MODEL), lambda b, i: (row(b), 0, 2)),
        pl.BlockSpec((1, D_MODEL), lambda b, i: (0, 0)),
    ]
    return pl.pallas_call(
        _out_kernel, grid=(B, L // tm), in_specs=in_specs,
        out_specs=pl.BlockSpec((None, tm, D_MODEL), tok),
        out_shape=jax.ShapeDtypeStruct((B, L, D_MODEL), F32),
        compiler_params=_cparams(("parallel", "parallel")),
        name="out_proj",
    )(x, att, y, z, ssd_norm_g, w_out, mod, final_g)


def _rope_tables(L):
    rows = L // GRID_W
    row_ids = jnp.repeat(jnp.arange(rows), GRID_W).astype(F32)
    col_ids = jnp.tile(jnp.arange(GRID_W), rows).astype(F32)
    inv = ROPE_BASE ** (-jnp.arange(ROPE_FREQS, dtype=F32) / ROPE_FREQS)
    ang_r = row_ids[:, None] * inv
    ang_c = col_ids[:, None] * inv
    cr, sr, cc, sc = jnp.cos(ang_r), jnp.sin(ang_r), jnp.cos(ang_c), jnp.sin(ang_c)
    cos64 = jnp.concatenate([cr, cr, cc, cc], axis=1)
    sin64 = jnp.concatenate([-sr, sr, -sc, sc], axis=1)
    return jnp.tile(cos64, (1, 2)), jnp.tile(sin64, (1, 2))


def _bc_perm():
    idx = list(range(D_SSD))
    for g in range(SSD_GROUPS):
        idx += [D_SSD + g * D_STATE + n for n in range(D_STATE)]
        idx += [D_SSD + SSD_GROUPS * D_STATE + g * D_STATE + n for n in range(D_STATE)]
    return jnp.asarray(idx, dtype=jnp.int32)


def _dt_perm():
    idx = []
    for g in range(SSD_GROUPS):
        for d in range(2):
            for hh in range(HEADS_PER_GROUP):
                idx.append(d * H_SSD + g * HEADS_PER_GROUP + hh)
    return jnp.asarray(idx, dtype=jnp.int32)


def _group_rows(v2h):
    t = v2h.reshape(2, SSD_GROUPS, HEADS_PER_GROUP).transpose(1, 0, 2).reshape(SSD_GROUPS, 2 * HEADS_PER_GROUP)
    return jnp.broadcast_to(t[:, :, None], (SSD_GROUPS, 2 * HEADS_PER_GROUP, LANES))


def _layer(x, mod, mod_row, params, rope_tabs, past_kv, h0, emit_f32, lam_init, final_g, tm, tq,
           heads_per_step, groups_per_step, tm_out):
    (norm_g, w_bf, w_xbc, w_dt, lams, subln_g, conv_w, conv_b, dtb_g, alog_g, dsk, ssd_norm_g,
     w_out) = params
    proj = _qkv_proj(x, mod, mod_row, norm_g, w_bf, rope_tabs, emit_f32, tm)
    qt, k, vt = proj[:3]
    g, z, xbc, dtt = _gzx_proj(x, mod, mod_row, norm_g, w_bf, w_xbc, w_dt, tm)
    kvs = [(k, vt)]
    if past_kv is not None:
        kvs.append(past_kv)
    att = _attention(qt, kvs, g, subln_g, lams, lam_init, tq, heads_per_step)
    y, hfin = _ssd(xbc, dtt, conv_w, conv_b, dtb_g, alog_g, dsk, h0, groups_per_step)
    out = _out_proj(x, att, y, z, ssd_norm_g, w_out, mod, mod_row, final_g, tm_out)
    return out, proj[3:], hfin


def kernel(x_prompt, x_sample, cache_k, cache_v, state_ssd, c, c_ctx, w_mod, b_mod, norm_g, w_in,
           lambda_q1, lambda_k1, lambda_q2, lambda_k2, subln_g, conv_w, conv_b, dt_bias, A_log,
           D_skip, ssd_norm_g, w_out, final_g):
    b_ctx, l_ctx, _ = x_prompt.shape
    b_dec, l_dec, _ = x_sample.shape
    l_past = cache_k.shape[2]
    depth = w_mod.shape[0]
    assert depth == 1
    lam_init = 0.8 - 0.6 * math.exp(-0.3 * 0)

    w_bf = w_in[0].astype(BF16)
    o2 = 4 * D_ATT + D_SSD
    bc_perm = _bc_perm()
    w_xbc = w_bf[:, o2:o2 + CONV_DIM][:, bc_perm]
    w_dt = jnp.pad(w_bf[:, o2 + CONV_DIM:][:, _dt_perm()], ((0, 0), (0, DT_PAD - 2 * H_SSD)))
    conv_w_p = conv_w[0][:, bc_perm]
    conv_b_p = conv_b[0][bc_perm].reshape(1, CONV_DIM)
    dtb_g = _group_rows(dt_bias[0])
    alog_g = _group_rows(A_log[0])
    dsk = jnp.broadcast_to(jnp.repeat(D_skip[0], SSD_HEADDIM)[:, None], (D_SSD, LANES))
    lams = [a[0].reshape(1, HD_QK) for a in (lambda_q1, lambda_k1, lambda_q2, lambda_k2)]
    params = (norm_g[0].reshape(1, D_MODEL), w_bf, w_xbc, w_dt, lams, subln_g[0].reshape(1, HD_V),
              conv_w_p, conv_b_p, dtb_g, alog_g, dsk, ssd_norm_g[0].reshape(1, D_SSD),
              w_out[0].astype(BF16))
    fg = final_g.reshape(1, D_MODEL)

    cond8 = jnp.zeros((8, D_MODEL), F32).at[:b_dec].set(c).at[b_dec].set(c_ctx)
    mod = _adaln(cond8, w_mod[0], b_mod[0]).reshape(8, 1, 3 * D_MODEL)

    y_prompt, (k_ctx, v_ctx), h_ctx = _layer(
        x_prompt, mod, lambda b: b_dec, params, None, None, None, True, lam_init, fg,
        tm=256, tq=256, heads_per_step=H_ATT, groups_per_step=SSD_GROUPS, tm_out=256)

    rope_tabs = _rope_tables(l_dec)
    k_past = cache_k[:, 0].reshape(b_dec, l_past, D_ATT).astype(BF16)
    vt_past = cache_v[:, 0].reshape(b_dec, l_past, D_ATT).transpose(0, 2, 1).astype(BF16)
    h0 = state_ssd[:, 0].reshape(b_dec, 2, SSD_GROUPS, GROUP_W, D_STATE)
    y_sample, _, _ = _layer(
        x_sample, mod, lambda b: b, params, rope_tabs, (k_past, vt_past), h0, False, lam_init, fg,
        tm=512, tq=512, heads_per_step=1, groups_per_step=1, tm_out=1024)

    new_cache_k = k_ctx.reshape(b_ctx, 1, l_ctx, H_ATT, 2 * HD_QK)
    new_cache_v = v_ctx.reshape(b_ctx, 1, l_ctx, H_ATT, HD_V)
    new_state = h_ctx.reshape(b_ctx, 1, 2, H_SSD, SSD_HEADDIM, D_STATE)
    return (y_prompt, y_sample, new_cache_k, new_cache_v, new_state)
```

```python
import functools
import math

import jax
import jax.numpy as jnp
from jax import lax
from jax.experimental import pallas as pl
from jax.experimental.pallas import tpu as pltpu

F32 = jnp.float32
BF16 = jnp.bfloat16

D_MODEL = 1024
GRID_W = 64
HD_QK = 64
HD_V = 128
H_ATT = 8
D_ATT = 1024
D_SSD = 1024
SSD_HEADDIM = 64
H_SSD = 16
SSD_GROUPS = 4
HEADS_PER_GROUP = 4
D_STATE = 64
CONV_DIM = D_SSD + 2 * SSD_GROUPS * D_STATE
CHUNK = 128
ROPE_BASE = 10000.0
ROPE_FREQS = 16
EPS = 1e-6

LANES = 128
SUBLANES = 8
DT_PAD = LANES
GROUP_W = HEADS_PER_GROUP * SSD_HEADDIM
VMEM_LIMIT = 56 * 1024 * 1024
ADALN_BLK = 1024
ATT_TK = 512
ATT_TQ = 256
SUB_ROWS = 256
LOG2E = 1.4426950408889634
CONV_UNROLL = 4
SCAN_UNROLL = 8


def _cparams(sem):
    return pltpu.CompilerParams(dimension_semantics=sem, vmem_limit_bytes=VMEM_LIMIT)


def _silu(x):
    hx = 0.5 * x
    return hx + hx * jnp.tanh(hx)


def _softplus(x):
    return jnp.maximum(x, 0.0) + jnp.log1p(jnp.exp(-jnp.abs(x)))


def _adaln_kernel(cond_ref, w_ref, b_ref, o_ref):
    cond = cond_ref[...]
    s = _silu(cond).astype(BF16)
    o_ref[...] = jnp.dot(s, w_ref[...].astype(BF16), preferred_element_type=F32) + b_ref[...]


def _adaln(cond, w_mod, b_mod):
    rows, n = cond.shape[0], w_mod.shape[1]
    return pl.pallas_call(
        _adaln_kernel,
        grid=(n // ADALN_BLK,),
        in_specs=[pl.BlockSpec((rows, D_MODEL), lambda j: (0, 0)),
                  pl.BlockSpec((D_MODEL, ADALN_BLK), lambda j: (0, j)),
                  pl.BlockSpec((1, ADALN_BLK), lambda j: (0, j))],
        out_specs=pl.BlockSpec((rows, ADALN_BLK), lambda j: (0, j)),
        out_shape=jax.ShapeDtypeStruct((rows, n), F32),
        compiler_params=_cparams(("parallel",)),
        name="adaln",
    )(cond, w_mod, b_mod.reshape(1, n))


def _mod_norm(x, shift, scale, g):
    ms = jnp.mean(x * x, axis=-1, keepdims=True)
    y = x * lax.rsqrt(ms + EPS) * g
    return y * (1.0 + scale) + shift


def _rope(t, cos, sin_signed, first):
    outs = []
    for h in range(H_ATT):
        th = t[:, h * LANES:(h + 1) * LANES]
        partner = jnp.where(first, pltpu.roll(th, LANES - 16, 1), pltpu.roll(th, 16, 1))
        outs.append(th * cos + partner * sin_signed)
    return jnp.concatenate(outs, axis=1)


def _qkv_kernel(*refs, rope, emit_f32):
    x_ref, shift_ref, scale_ref, ng_ref, w_ref = refs[:5]
    pos = 5
    if rope:
        cos_ref, sin_ref = refs[pos:pos + 2]
        pos += 2
    qt_ref, k_ref, vt_ref = refs[pos:pos + 3]
    pos += 3
    if emit_f32:
        k32_ref, v32_ref = refs[pos:pos + 2]

    hb = _mod_norm(x_ref[...], shift_ref[...], scale_ref[...], ng_ref[...]).astype(BF16)
    q = jnp.dot(hb, w_ref[:, 0:D_ATT], preferred_element_type=F32)
    k = jnp.dot(hb, w_ref[:, D_ATT:2 * D_ATT], preferred_element_type=F32)
    v = jnp.dot(hb, w_ref[:, 2 * D_ATT:3 * D_ATT], preferred_element_type=F32)
    if rope:
        cos = cos_ref[...]
        sin_signed = sin_ref[...]
        lane = lax.broadcasted_iota(jnp.int32, cos.shape, 1)
        first = (lane % 32) < 16
        q = _rope(q, cos, sin_signed, first)
        k = _rope(k, cos, sin_signed, first)
    qt_ref[...] = (q * (LOG2E / math.sqrt(HD_QK))).T.astype(BF16)
    k_ref[...] = k.astype(BF16)
    vt_ref[...] = v.T.astype(BF16)
    if emit_f32:
        k32_ref[...] = k
        v32_ref[...] = v


def _qkv_proj(x, mod, mod_row, norm_g, w_bf, rope_tabs, emit_f32, tm):
    B, L, _ = x.shape
    grid = (B, L // tm)
    row = mod_row

    in_specs = [
        pl.BlockSpec((None, tm, D_MODEL), lambda b, i: (b, i, 0)),
        pl.BlockSpec((None, 1, D_MODEL), lambda b, i: (row(b), 0, 0)),
        pl.BlockSpec((None, 1, D_MODEL), lambda b, i: (row(b), 0, 1)),
        pl.BlockSpec((1, D_MODEL), lambda b, i: (0, 0)),
        pl.BlockSpec((D_MODEL, 3 * D_ATT), lambda b, i: (0, 0), pipeline_mode=pl.Buffered(1)),
    ]
    args = [x, mod, mod, norm_g, w_bf]
    if rope_tabs is not None:
        in_specs += [pl.BlockSpec((tm, LANES), lambda b, i: (i, 0)),
                     pl.BlockSpec((tm, LANES), lambda b, i: (i, 0))]
        args += list(rope_tabs)
    out_specs = [
        pl.BlockSpec((None, D_ATT, tm), lambda b, i: (b, 0, i)),
        pl.BlockSpec((None, tm, D_ATT), lambda b, i: (b, i, 0)),
        pl.BlockSpec((None, D_ATT, tm), lambda b, i: (b, 0, i)),
    ]
    out_shape = [
        jax.ShapeDtypeStruct((B, D_ATT, L), BF16),
        jax.ShapeDtypeStruct((B, L, D_ATT), BF16),
        jax.ShapeDtypeStruct((B, D_ATT, L), BF16),
    ]
    if emit_f32:
        out_specs += [pl.BlockSpec((None, tm, D_ATT), lambda b, i: (b, i, 0))] * 2
        out_shape += [jax.ShapeDtypeStruct((B, L, D_ATT), F32)] * 2
    return pl.pallas_call(
        functools.partial(_qkv_kernel, rope=rope_tabs is not None, emit_f32=emit_f32),
        grid=grid, in_specs=in_specs, out_specs=out_specs, out_shape=out_shape,
        compiler_params=_cparams(("parallel", "parallel")),
        name="qkv_proj",
    )(*args)


def _gzx_kernel(x_ref, shift_ref, scale_ref, ng_ref, wg_ref, wz_ref, wx_ref, wdt_ref,
                g_ref, z_ref, xbc_ref, dtt_ref):
    n_sub = x_ref.shape[0] // SUB_ROWS

    def prologue(r):
        rows = slice(r * SUB_ROWS, (r + 1) * SUB_ROWS)
        return _mod_norm(x_ref[rows, :], shift_ref[...], scale_ref[...], ng_ref[...]).astype(BF16)

    hb = prologue(0)
    for r in range(n_sub):
        rows = slice(r * SUB_ROWS, (r + 1) * SUB_ROWS)
        hb_next = prologue(r + 1) if r + 1 < n_sub else None
        g_ref[rows, :] = jnp.dot(hb, wg_ref[...], preferred_element_type=F32)
        z_ref[rows, :] = jnp.dot(hb, wz_ref[...], preferred_element_type=F32)
        xbc_ref[rows, :] = jnp.dot(hb, wx_ref[...], preferred_element_type=F32)
        dt = jnp.dot(hb, wdt_ref[...], preferred_element_type=F32)
        dtt_ref[:, rows] = dt.T[:2 * H_SSD, :]
        hb = hb_next


def _gzx_proj(x, mod, mod_row, norm_g, w_bf, w_xbc, w_dt, tm):
    B, L, _ = x.shape
    row = mod_row
    once = pl.Buffered(1)
    in_specs = [
        pl.BlockSpec((None, tm, D_MODEL), lambda b, i: (b, i, 0)),
        pl.BlockSpec((None, 1, D_MODEL), lambda b, i: (row(b), 0, 0)),
        pl.BlockSpec((None, 1, D_MODEL), lambda b, i: (row(b), 0, 1)),
        pl.BlockSpec((1, D_MODEL), lambda b, i: (0, 0)),
        pl.BlockSpec((D_MODEL, D_ATT), lambda b, i: (0, 3), pipeline_mode=once),
        pl.BlockSpec((D_MODEL, D_SSD), lambda b, i: (0, 4), pipeline_mode=once),
        pl.BlockSpec((D_MODEL, CONV_DIM), lambda b, i: (0, 0), pipeline_mode=once),
        pl.BlockSpec((D_MODEL, DT_PAD), lambda b, i: (0, 0), pipeline_mode=once),
    ]
    widths = (D_ATT, D_SSD, CONV_DIM)
    out_specs = [pl.BlockSpec((None, tm, w), lambda b, i: (b, i, 0)) for w in widths]
    out_shape = [jax.ShapeDtypeStruct((B, L, w), F32) for w in widths]
    out_specs.append(pl.BlockSpec((None, 2 * H_SSD, tm), lambda b, i: (b, 0, i)))
    out_shape.append(jax.ShapeDtypeStruct((B, 2 * H_SSD, L), F32))
    return pl.pallas_call(
        _gzx_kernel, grid=(B, L // tm), in_specs=in_specs, out_specs=out_specs, out_shape=out_shape,
        compiler_params=_cparams(("parallel", "parallel")),
        name="gzx_proj",
    )(x, mod, mod, norm_g, w_bf, w_bf, w_xbc, w_dt)


def _attn_kernel(*refs, n_src, lam_init, carry):
    qt_ref = refs[0]
    pos = 1
    if carry:
        qt_next_ref = refs[pos]
        k_next_refs = refs[pos + 1:pos + 1 + n_src]
        pos += 1 + n_src
    srcs = [(refs[pos + 2 * i], refs[pos + 1 + 2 * i]) for i in range(n_src)]
    pos += 2 * n_src
    g_ref, sg_ref, lq1_ref, lk1_ref, lq2_ref, lk2_ref, o_ref = refs[pos:pos + 7]
    if carry:
        s_sc, m_sc = refs[pos + 7:pos + 9]

    tq = ATT_TQ
    n_heads = qt_ref.shape[0] // HD_V
    chains = [(hh, t) for hh in range(n_heads) for t in range(qt_ref.shape[1] // tq)]
    n_ch = len(chains)
    row = lax.broadcasted_iota(jnp.int32, (HD_V, tq), 0)
    zero = jnp.zeros((HD_V, tq), BF16)

    def stacked_q(ref, hh, t):
        qt = ref[hh * HD_V:(hh + 1) * HD_V, t * tq:(t + 1) * tq]
        return jnp.concatenate([jnp.where(row < HD_QK, qt, zero), jnp.where(row >= HD_QK, qt, zero)], axis=1)

    q2t = [stacked_q(qt_ref, hh, t) for hh, t in chains]

    blocks = []
    for si, (k_ref, vt_ref) in enumerate(srcs):
        n_keys = k_ref.shape[0]
        tk = min(ATT_TK, n_keys)
        blocks += [(k_ref, vt_ref, j * tk, tk, si) for j in range(n_keys // tk)]

    lam = (jnp.exp(jnp.sum(lq1_ref[...] * lk1_ref[...], axis=1, keepdims=True))
           - jnp.exp(jnp.sum(lq2_ref[...] * lk2_ref[...], axis=1, keepdims=True)) + lam_init)

    def score_piece(c, j, handover=None):
        k_ref, _, off, tk, si = blocks[j]
        hh = chains[c][0]
        if handover is not None:
            k_ref = k_next_refs[si]
        s = jnp.dot(k_ref[off:off + tk, hh * HD_V:(hh + 1) * HD_V],
                    q2t[c] if handover is None else handover, preferred_element_type=F32)
        return s, jnp.max(s, axis=0, keepdims=True)

    def prob_piece(c, j, s, m, l, acc):
        _, vt_ref, off, tk, _ = blocks[j]
        hh = chains[c][0]
        p = jnp.exp2(s - m)
        l = l + jnp.sum(p, axis=0, keepdims=True)
        acc = acc + jnp.dot(vt_ref[hh * HD_V:(hh + 1) * HD_V, off:off + tk], p.astype(BF16),
                            preferred_element_type=F32)
        return l, acc

    def epilogue(c, l, acc):
        hh, t = chains[c]
        o_n = acc / l
        o = (o_n[:, :tq] - lam * o_n[:, tq:]).T
        ms = jnp.mean(o * o, axis=-1, keepdims=True)
        att = o * lax.rsqrt(ms + EPS) * sg_ref[...] * (1.0 - lam_init)
        gate = _silu(g_ref[t * tq:(t + 1) * tq, hh * HD_V:(hh + 1) * HD_V])
        o_ref[t * tq:(t + 1) * tq, hh * HD_V:(hh + 1) * HD_V] = (att * gate).astype(BF16)

    n_pc = len(blocks)
    neg_inf = jnp.full((1, 2 * tq), -jnp.inf, F32)
    if not carry:
        prev = None
        for c in range(n_ch + 1):
            cur_s, cur_m = [], neg_inf
            if prev is not None:
                l = jnp.zeros((1, 2 * tq), F32)
                acc = jnp.zeros((HD_V, 2 * tq), F32)
            for j in range(n_pc):
                if c < n_ch:
                    s, s_max = score_piece(c, j)
                    cur_s.append(s)
                    cur_m = jnp.maximum(cur_m, s_max)
                if prev is not None:
                    l, acc = prob_piece(prev[0], j, prev[1][j], prev[2], l, acc)
            if prev is not None:
                epilogue(prev[0], l, acc)
            prev = (c, cur_s, cur_m) if c < n_ch else None
        return

    dst = [sum(b[3] for b in blocks[:j]) for j in range(n_pc)]
    step = ((pl.program_id(0) * pl.num_programs(1) + pl.program_id(1)) * pl.num_programs(2)
            + pl.program_id(2))
    cur, nxt = step % 2, (step + 1) % 2

    @pl.when(step == 0)
    def _():
        m0 = neg_inf
        for j in range(n_pc):
            s, s_max = score_piece(0, j)
            s_sc[0, dst[j]:dst[j] + blocks[j][3], :] = s
            m0 = jnp.maximum(m0, s_max)
        m_sc[0] = jnp.broadcast_to(m0, m_sc.shape[1:])

    hh0, t0 = chains[0]
    q_next = stacked_q(qt_next_ref, hh0, t0)
    prev_s, prev_m = None, m_sc[cur][0:1, :]
    for u in range(n_ch):
        l = jnp.zeros((1, 2 * tq), F32)
        acc = jnp.zeros((HD_V, 2 * tq), F32)
        nxt_s, nxt_m = [], neg_inf
        for j in range(n_pc):
            rows = slice(dst[j], dst[j] + blocks[j][3])
            if u + 1 < n_ch:
                s, s_max = score_piece(u + 1, j)
                nxt_s.append(s)
            else:
                s, s_max = score_piece(0, j, q_next)
                s_sc[nxt, rows, :] = s
            nxt_m = jnp.maximum(nxt_m, s_max)
            s_u = s_sc[cur, rows, :] if u == 0 else prev_s[j]
            l, acc = prob_piece(u, j, s_u, prev_m, l, acc)
        if u + 1 == n_ch:
            m_sc[nxt] = jnp.broadcast_to(nxt_m, m_sc.shape[1:])
        epilogue(u, l, acc)
        prev_s, prev_m = nxt_s, nxt_m


def _attention(qt, kvs, g, subln_g, lams, lam_init, tq, heads_per_step):
    B, _, L = qt.shape
    hw = heads_per_step * HD_V
    nq = L // tq
    nh = H_ATT // heads_per_step
    carry = nq > 1
    in_specs = [pl.BlockSpec((None, hw, tq), lambda b, h, i: (b, h, i))]
    args = [qt]
    scratch = []
    if carry:
        def succ(b, h, i):
            end_i, end_h = i == nq - 1, h == nh - 1
            last = end_i & end_h & (b == B - 1)
            b1 = jnp.where(end_i & end_h & ~last, b + 1, b)
            h1 = jnp.where(last, h, jnp.where(end_i, jnp.where(end_h, 0, h + 1), h))
            i1 = jnp.where(last, i, jnp.where(end_i, 0, i + 1))
            return b1, h1, i1

        def q_next(b, h, i):
            b1, h1, i1 = succ(b, h, i)
            return b1, h1, i1

        def k_next(b, h, i):
            b1, h1, _ = succ(b, h, i)
            return b1, 0, h1

        in_specs.append(pl.BlockSpec((None, hw, tq), q_next))
        args.append(qt)
        for k, _ in kvs:
            in_specs.append(pl.BlockSpec((None, k.shape[1], hw), k_next))
            args.append(k)
        n_all = sum(k.shape[1] for k, _ in kvs)
        scratch = [pltpu.VMEM((2, n_all, 2 * ATT_TQ), F32), pltpu.VMEM((2, SUBLANES, 2 * ATT_TQ), F32)]
    for k, vt in kvs:
        n_keys = k.shape[1]
        in_specs += [pl.BlockSpec((None, n_keys, hw), lambda b, h, i: (b, 0, h)),
                     pl.BlockSpec((None, hw, n_keys), lambda b, h, i: (b, h, 0))]
        args += [k, vt]
    in_specs += [pl.BlockSpec((None, tq, hw), lambda b, h, i: (b, i, h)),
                 pl.BlockSpec((1, HD_V), lambda b, h, i: (0, 0))]
    in_specs += [pl.BlockSpec((1, HD_QK), lambda b, h, i: (0, 0))] * 4
    args += [g, subln_g] + list(lams)
    return pl.pallas_call(
        functools.partial(_attn_kernel, n_src=len(kvs), lam_init=lam_init, carry=carry),
        grid=(B, nh, nq),
        in_specs=in_specs,
        out_specs=pl.BlockSpec((None, tq, hw), lambda b, h, i: (b, i, h)),
        out_shape=jax.ShapeDtypeStruct((B, L, D_ATT), BF16),
        scratch_shapes=scratch,
        compiler_params=_cparams(("arbitrary",) * 3 if carry else ("parallel", "parallel", "arbitrary")),
        name="diff_attention",
    )(*args)


def _ssd_kernel(*refs, seq_len, has_h0):
    (x_ref, bc_ref, dtt_ref, cwx_ref, cwbc_ref, cbx_ref, cbbc_ref,
     dtb_ref, alog_ref, dsk_ref) = refs[:10]
    pos = 10
    if has_h0:
        h0_ref = refs[pos]
        pos += 1
    y_ref, hfin_ref, xt_sc, yt_sc, b_sc, ct_sc = refs[pos:pos + 6]

    nc = seq_len // CHUNK
    n_grp = x_ref.shape[1] // GROUP_W
    row_i = lax.broadcasted_iota(jnp.int32, (CHUNK, CHUNK), 0)
    col_i = lax.broadcasted_iota(jnp.int32, (CHUNK, CHUNK), 1)
    row1 = lax.broadcasted_iota(jnp.int32, (CHUNK, 1), 0)

    def conv_silu(src, w_ref, b_ref, c):
        r0 = pl.multiple_of(c * CHUNK, CHUNK)
        rp = pl.multiple_of(jnp.maximum(r0 - SUBLANES, 0), SUBLANES)
        rn = pl.multiple_of(jnp.minimum(r0 + CHUNK, seq_len - SUBLANES), SUBLANES)
        u = src[pl.ds(r0, CHUNK), :]
        pr = jnp.where(c > 0, src[pl.ds(rp, SUBLANES), :][SUBLANES - 1:SUBLANES, :], 0.0)
        nx = jnp.where(c < nc - 1, src[pl.ds(rn, SUBLANES), :][0:1, :], 0.0)
        prev = jnp.where(row1 == 0, pr, pltpu.roll(u, 1, 0))
        nxt = jnp.where(row1 == CHUNK - 1, nx, pltpu.roll(u, CHUNK - 1, 0))
        w = 0.5 * w_ref[...]
        hx = 0.5 * b_ref[...] + prev * w[0:1, :] + u * w[1:2, :] + nxt * w[2:3, :]
        return hx + hx * jnp.tanh(hx)

    def conv_chunk(c, carry):
        xt = conv_silu(x_ref, cwx_ref, cbx_ref, c).T
        xt_sc[c] = xt
        yt_sc[c] = dsk_ref[...] * xt
        bc = conv_silu(bc_ref, cwbc_ref, cbbc_ref, c)
        for g in range(n_grp):
            bc_g = bc[:, g * LANES:(g + 1) * LANES]
            b_sc[c, g] = bc_g[:, :D_STATE].astype(BF16)
            ct_sc[c, g] = bc_g.T[D_STATE:, :].astype(BF16)
        return carry

    lax.fori_loop(0, nc, conv_chunk, 0, unroll=math.gcd(CONV_UNROLL, nc))

    dir_heads = 2 * HEADS_PER_GROUP
    dtb = [dtb_ref[g] for g in range(n_grp)]
    a_neg = [-jnp.exp(alog_ref[g]) * LOG2E for g in range(n_grp)]

    masks = (row_i <= col_i, row_i >= col_i)
    tri = tuple(jnp.where(m, 1.0, 0.0).astype(BF16) for m in masks)
    ends = (CHUNK - 1, 0)

    def scan_prep(items):
        out = []
        for d in range(2):
            sp_d = []
            for g, dd, c in items:
                if dd == d:
                    t0 = pl.multiple_of(c * CHUNK, CHUNK)
                    dt_g = dtt_ref[g * dir_heads:(g + 1) * dir_heads, pl.ds(t0, CHUNK)]
                    sp_d.append(_softplus(dt_g + dtb[g]))
            x = jnp.concatenate([s * a_neg[it[0]] for s, it in
                                 zip(sp_d, [it for it in items if it[1] == d])], axis=0)
            hi = x.astype(BF16)
            r1 = x - hi.astype(F32)
            mid = r1.astype(BF16)
            lo = (r1 - mid.astype(F32)).astype(BF16)
            parts = jnp.dot(jnp.concatenate([hi, mid, lo], axis=0), tri[d], preferred_element_type=F32)
            nr = x.shape[0]
            cum_d = parts[:nr] + parts[nr:2 * nr] + parts[2 * nr:]
            out.append((jnp.concatenate(sp_d, axis=0), cum_d, cum_d.T))
        return tuple(out)

    def scan_items(items, prep, states):
        n = len(items)
        sp, cum, cum_col, cb_t = [None] * n, [None] * n, [None] * n, [None] * n
        for d in range(2):
            sp_d, cum_d, cum_dt = prep[d]
            for i, k in enumerate([k for k, it in enumerate(items) if it[1] == d]):
                rows = slice(i * dir_heads, (i + 1) * dir_heads)
                sp[k], cum[k], cum_col[k] = sp_d[rows], cum_d[rows], cum_dt[:, rows]
        for k, (g, d, c) in enumerate(items):
            cb_t[k] = jnp.dot(b_sc[c, g], ct_sc[c, g], preferred_element_type=F32)
        xdt_b, lm_b, x_dec, y_scale, s_scale = [], [], [], [], []
        for k, (g, d, c) in enumerate(items):
            xt = xt_sc[c, g * GROUP_W:(g + 1) * GROUP_W, :]
            xdt_k, lm_k, x_dec_k, y_scale_k, s_scale_k = [], [], [], [], []
            for hh in range(HEADS_PER_GROUP):
                r = d * HEADS_PER_GROUP + hh
                a_row = cum[k][r:r + 1, :]
                seg = a_row - cum_col[k][:, r:r + 1]
                lmat = jnp.exp2(jnp.where(masks[d], seg, -jnp.inf))
                xdt = xt[hh * SSD_HEADDIM:(hh + 1) * SSD_HEADDIM, :] * sp[k][r:r + 1, :]
                a_end = a_row[:, ends[d]:ends[d] + 1]
                xdt_k.append(xdt.astype(BF16))
                lm_k.append((cb_t[k] * lmat).astype(BF16))
                x_dec_k.append((xdt * jnp.exp2(a_end - a_row)).astype(BF16))
                y_scale_k.append(jnp.broadcast_to(jnp.exp2(a_row), (SSD_HEADDIM, CHUNK)))
                s_scale_k.append(jnp.broadcast_to(jnp.exp2(a_end), (SSD_HEADDIM, D_STATE)))
            xdt_b.append(xdt_k)
            lm_b.append(lm_k)
            x_dec.append(jnp.concatenate(x_dec_k, axis=0))
            y_scale.append(jnp.concatenate(y_scale_k, axis=0))
            s_scale.append(jnp.concatenate(s_scale_k, axis=0))
        y_diag, st = [], []
        for k, (g, d, c) in enumerate(items):
            y_diag.append(jnp.concatenate(
                [jnp.dot(xdt_b[k][hh], lm_b[k][hh], preferred_element_type=F32)
                 for hh in range(HEADS_PER_GROUP)], axis=0))
            st.append(jnp.dot(x_dec[k], b_sc[c, g], preferred_element_type=F32))
        states = list(states)
        for k, (g, d, c) in enumerate(items):
            si = 2 * g + d
            rows = slice(g * GROUP_W, (g + 1) * GROUP_W)
            y_off = jnp.dot(states[si].astype(BF16), ct_sc[c, g], preferred_element_type=F32)
            yt_sc[c, rows, :] = yt_sc[c, rows, :] + y_diag[k] + y_off * y_scale[k]
            states[si] = states[si] * s_scale[k] + st[k]
        return tuple(states)

    if has_h0:
        s_init = tuple(h0_ref[d, g] for g in range(n_grp) for d in range(2))
    else:
        s_init = (jnp.zeros((GROUP_W, D_STATE), F32),) * (2 * n_grp)

    unroll = math.gcd(max(SCAN_UNROLL // n_grp, 1), nc)
    n_it = nc // unroll

    def items_of(i):
        items = []
        for u in range(unroll):
            for g in range(n_grp):
                items += [(g, 0, i * unroll + u), (g, 1, nc - 1 - (i * unroll + u))]
        return items

    def scan_step(i, carry):
        states, prep = carry
        prep_next = scan_prep(items_of(jnp.minimum(i + 1, n_it - 1)))
        return scan_items(items_of(i), prep, states), prep_next

    s_fin, _ = lax.fori_loop(0, n_it, scan_step, (s_init, scan_prep(items_of(0))))
    for g in range(n_grp):
        for d in range(2):
            hfin_ref[d, g] = s_fin[2 * g + d]

    def emit_chunk(c, carry):
        y_ref[pl.ds(pl.multiple_of(c * CHUNK, CHUNK), CHUNK), :] = yt_sc[c].T
        return carry

    lax.fori_loop(0, nc, emit_chunk, 0, unroll=math.gcd(CONV_UNROLL, nc))


def _ssd(xbc, dtt, conv_w, conv_b, dtb_g, alog_g, dsk, h0, n_grp):
    B, L, _ = xbc.shape
    nc = L // CHUNK
    xw, bw = n_grp * GROUP_W, n_grp * LANES
    bc_blk = D_SSD // bw
    dir_heads = 2 * HEADS_PER_GROUP
    state_spec = pl.BlockSpec((None, 2, n_grp, GROUP_W, D_STATE), lambda b, g: (b, 0, g, 0, 0))
    in_specs = [
        pl.BlockSpec((None, L, xw), lambda b, g: (b, 0, g)),
        pl.BlockSpec((None, L, bw), lambda b, g: (b, 0, bc_blk + g)),
        pl.BlockSpec((None, n_grp * dir_heads, L), lambda b, g: (b, g, 0)),
        pl.BlockSpec((3, xw), lambda b, g: (0, g)),
        pl.BlockSpec((3, bw), lambda b, g: (0, bc_blk + g)),
        pl.BlockSpec((1, xw), lambda b, g: (0, g)),
        pl.BlockSpec((1, bw), lambda b, g: (0, bc_blk + g)),
        pl.BlockSpec((n_grp, dir_heads, LANES), lambda b, g: (g, 0, 0)),
        pl.BlockSpec((n_grp, dir_heads, LANES), lambda b, g: (g, 0, 0)),
        pl.BlockSpec((xw, LANES), lambda b, g: (g, 0)),
    ]
    args = [xbc, xbc, dtt, conv_w, conv_w, conv_b, conv_b, dtb_g, alog_g, dsk]
    if h0 is not None:
        in_specs.append(state_spec)
        args.append(h0)
    return pl.pallas_call(
        functools.partial(_ssd_kernel, seq_len=L, has_h0=h0 is not None),
        grid=(B, SSD_GROUPS // n_grp),
        in_specs=in_specs,
        out_specs=[pl.BlockSpec((None, L, xw), lambda b, g: (b, 0, g)), state_spec],
        out_shape=[jax.ShapeDtypeStruct((B, L, D_SSD), F32),
                   jax.ShapeDtypeStruct((B, 2, SSD_GROUPS, GROUP_W, D_STATE), F32)],
        scratch_shapes=[pltpu.VMEM((nc, xw, CHUNK), F32), pltpu.VMEM((nc, xw, CHUNK), F32),
                        pltpu.VMEM((nc, n_grp, CHUNK, D_STATE), BF16),
                        pltpu.VMEM((nc, n_grp, D_STATE, CHUNK), BF16)],
        compiler_params=_cparams(("parallel", "parallel")),
        name="ssd_scan",
    )(*args)


def _out_kernel(x_ref, att_ref, y_ref, z_ref, sng_ref, w_ref, gate_ref, fg_ref, o_ref):
    n_sub = x_ref.shape[0] // SUB_ROWS

    def prologue(r):
        rows = slice(r * SUB_ROWS, (r + 1) * SUB_ROWS)
        yz = y_ref[rows, :] * _silu(z_ref[rows, :])
        ms = jnp.mean(yz * yz, axis=-1, keepdims=True)
        return (yz * lax.rsqrt(ms + EPS) * sng_ref[...]).astype(BF16)

    ssd_y = prologue(0)
    for r in range(n_sub):
        rows = slice(r * SUB_ROWS, (r + 1) * SUB_ROWS)
        ssd_next = prologue(r + 1) if r + 1 < n_sub else None
        out = jnp.dot(att_ref[rows, :], w_ref[0:D_ATT, :], preferred_element_type=F32)
        out = out + jnp.dot(ssd_y, w_ref[D_ATT:D_ATT + D_SSD, :], preferred_element_type=F32)
        res = x_ref[rows, :] + gate_ref[...] * out
        ms2 = jnp.mean(res * res, axis=-1, keepdims=True)
        o_ref[rows, :] = res * lax.rsqrt(ms2 + EPS) * fg_ref[...]
        ssd_y = ssd_next


def _out_proj(x, att, y, z, ssd_norm_g, w_out, mod, mod_row, final_g, tm):
    B, L, _ = x.shape
    row = mod_row
    tok = lambda b, i: (b, i, 0)
    in_specs = [
        pl.BlockSpec((None, tm, D_MODEL), tok),
        pl.BlockSpec((None, tm, D_ATT), tok),
        pl.BlockSpec((None, tm, D_SSD), tok),
        pl.BlockSpec((None, tm, D_SSD), tok),
        pl.BlockSpec((1, D_SSD), lambda b, i: (0, 0)),
        pl.BlockSpec((D_ATT + D_SSD, D_MODEL), lambda b, i: (0, 0), pipeline_mode=pl.Buffered(1)),
        pl.BlockSpec((None, 1, D_MODEL), lambda b, i: (row(b), 0, 2)),
        pl.BlockSpec((1, D_MODEL), lambda b, i: (0, 0)),
    ]
    return pl.pallas_call(
        _out_kernel, grid=(B, L // tm), in_specs=in_specs,
        out_specs=pl.BlockSpec((None, tm, D_MODEL), tok),
        out_shape=jax.ShapeDtypeStruct((B, L, D_MODEL), F32),
        compiler_params=_cparams(("parallel", "parallel")),
        name="out_proj",
    )(x, att, y, z, ssd_norm_g, w_out, mod, final_g)


def _rope_tables(L):
    rows = L // GRID_W
    row_ids = jnp.repeat(jnp.arange(rows), GRID_W).astype(F32)
    col_ids = jnp.tile(jnp.arange(GRID_W), rows).astype(F32)
    inv = ROPE_BASE ** (-jnp.arange(ROPE_FREQS, dtype=F32) / ROPE_FREQS)
    ang_r = row_ids[:, None] * inv
    ang_c = col_ids[:, None] * inv
    cr, sr, cc, sc = jnp.cos(ang_r), jnp.sin(ang_r), jnp.cos(ang_c), jnp.sin(ang_c)
    cos64 = jnp.concatenate([cr, cr, cc, cc], axis=1)
    sin64 = jnp.concatenate([-sr, sr, -sc, sc], axis=1)
    return jnp.tile(cos64, (1, 2)), jnp.tile(sin64, (1, 2))


def _bc_perm():
    idx = list(range(D_SSD))
    for g in range(SSD_GROUPS):
        idx += [D_SSD + g * D_STATE + n for n in range(D_STATE)]
        idx += [D_SSD + SSD_GROUPS * D_STATE + g * D_STATE + n for n in range(D_STATE)]
    return jnp.asarray(idx, dtype=jnp.int32)


def _dt_perm():
    idx = []
    for g in range(SSD_GROUPS):
        for d in range(2):
            for hh in range(HEADS_PER_GROUP):
                idx.append(d * H_SSD + g * HEADS_PER_GROUP + hh)
    return jnp.asarray(idx, dtype=jnp.int32)


def _group_rows(v2h):
    t = v2h.reshape(2, SSD_GROUPS, HEADS_PER_GROUP).transpose(1, 0, 2).reshape(SSD_GROUPS, 2 * HEADS_PER_GROUP)
    return jnp.broadcast_to(t[:, :, None], (SSD_GROUPS, 2 * HEADS_PER_GROUP, LANES))


def _layer(x, mod, mod_row, params, rope_tabs, past_kv, h0, emit_f32, lam_init, final_g, tm, tq,
           heads_per_step, groups_per_step, tm_out):
    (norm_g, w_bf, w_xbc, w_dt, lams, subln_g, conv_w, conv_b, dtb_g, alog_g, dsk, ssd_norm_g,
     w_out) = params
    proj = _qkv_proj(x, mod, mod_row, norm_g, w_bf, rope_tabs, emit_f32, tm)
    qt, k, vt = proj[:3]
    g, z, xbc, dtt = _gzx_proj(x, mod, mod_row, norm_g, w_bf, w_xbc, w_dt, tm)
    kvs = [(k, vt)]
    if past_kv is not None:
        kvs.append(past_kv)
    att = _attention(qt, kvs, g, subln_g, lams, lam_init, tq, heads_per_step)
    y, hfin = _ssd(xbc, dtt, conv_w, conv_b, dtb_g, alog_g, dsk, h0, groups_per_step)
    out = _out_proj(x, att, y, z, ssd_norm_g, w_out, mod, mod_row, final_g, tm_out)
    return out, proj[3:], hfin


def kernel(x_prompt, x_sample, cache_k, cache_v, state_ssd, c, c_ctx, w_mod, b_mod, norm_g, w_in,
           lambda_q1, lambda_k1, lambda_q2, lambda_k2, subln_g, conv_w, conv_b, dt_bias, A_log,
           D_skip, ssd_norm_g, w_out, final_g):
    b_ctx, l_ctx, _ = x_prompt.shape
    b_dec, l_dec, _ = x_sample.shape
    l_past = cache_k.shape[2]
    depth = w_mod.shape[0]
    assert depth == 1
    lam_init = 0.8 - 0.6 * math.exp(-0.3 * 0)

    w_bf = w_in[0].astype(BF16)
    o2 = 4 * D_ATT + D_SSD
    bc_perm = _bc_perm()
    w_xbc = w_bf[:, o2:o2 + CONV_DIM][:, bc_perm]
    w_dt = jnp.pad(w_bf[:, o2 + CONV_DIM:][:, _dt_perm()], ((0, 0), (0, DT_PAD - 2 * H_SSD)))
    conv_w_p = conv_w[0][:, bc_perm]
    conv_b_p = conv_b[0][bc_perm].reshape(1, CONV_DIM)
    dtb_g = _group_rows(dt_bias[0])
    alog_g = _group_rows(A_log[0])
    dsk = jnp.broadcast_to(jnp.repeat(D_skip[0], SSD_HEADDIM)[:, None], (D_SSD, LANES))
    lams = [a[0].reshape(1, HD_QK) for a in (lambda_q1, lambda_k1, lambda_q2, lambda_k2)]
    params = (norm_g[0].reshape(1, D_MODEL), w_bf, w_xbc, w_dt, lams, subln_g[0].reshape(1, HD_V),
              conv_w_p, conv_b_p, dtb_g, alog_g, dsk, ssd_norm_g[0].reshape(1, D_SSD),
              w_out[0].astype(BF16))
    fg = final_g.reshape(1, D_MODEL)

    assert b_dec < SUBLANES
    cond = jnp.zeros((SUBLANES, D_MODEL), F32).at[:b_dec].set(c).at[b_dec].set(c_ctx)
    mod = _adaln(cond, w_mod[0], b_mod[0]).reshape(SUBLANES, 1, 3 * D_MODEL)

    y_prompt, (k_ctx, v_ctx), h_ctx = _layer(
        x_prompt, mod, lambda b: b_dec, params, None, None, None, True, lam_init, fg,
        tm=256, tq=256, heads_per_step=H_ATT, groups_per_step=SSD_GROUPS, tm_out=256)

    rope_tabs = _rope_tables(l_dec)
    k_past = cache_k[:, 0].reshape(b_dec, l_past, D_ATT).astype(BF16)
    vt_past = cache_v[:, 0].reshape(b_dec, l_past, D_ATT).transpose(0, 2, 1).astype(BF16)
    h0 = state_ssd[:, 0].reshape(b_dec, 2, SSD_GROUPS, GROUP_W, D_STATE)
    y_sample, _, _ = _layer(
        x_sample, mod, lambda b: b, params, rope_tabs, (k_past, vt_past), h0, False, lam_init, fg,
        tm=512, tq=512, heads_per_step=1, groups_per_step=1, tm_out=1024)

    new_cache_k = k_ctx.reshape(b_ctx, 1, l_ctx, H_ATT, 2 * HD_QK)
    new_cache_v = v_ctx.reshape(b_ctx, 1, l_ctx, H_ATT, HD_V)
    new_state = h_ctx.reshape(b_ctx, 1, 2, H_SSD, SSD_HEADDIM, D_STATE)
    return (y_prompt, y_sample, new_cache_k, new_cache_v, new_state)
```

```python
import functools
import math

import jax
import jax.numpy as jnp
from jax import lax
from jax.experimental import pallas as pl
from jax.experimental.pallas import tpu as pltpu

F32 = jnp.float32
BF16 = jnp.bfloat16

D_MODEL = 1024
GRID_W = 64
HD_QK = 64
HD_V = 128
H_ATT = 8
D_ATT = 1024
D_SSD = 1024
SSD_HEADDIM = 64
H_SSD = 16
SSD_GROUPS = 4
HEADS_PER_GROUP = 4
D_STATE = 64
CONV_DIM = D_SSD + 2 * SSD_GROUPS * D_STATE
CHUNK = 128
ROPE_BASE = 10000.0
ROPE_FREQS = 16
EPS = 1e-6

LANES = 128
SUBLANES = 8
DT_PAD = LANES
GROUP_W = HEADS_PER_GROUP * SSD_HEADDIM
VMEM_LIMIT = 56 * 1024 * 1024
ADALN_BLK = 1024
ATT_TK = 512
ATT_TQ = 256
SUB_ROWS = 256
LOG2E = 1.4426950408889634
CONV_UNROLL = 4
SCAN_UNROLL = 8


def _cparams(sem):
    return pltpu.CompilerParams(dimension_semantics=sem, vmem_limit_bytes=VMEM_LIMIT)


def _silu(x):
    hx = 0.5 * x
    return hx + hx * jnp.tanh(hx)


def _softplus(x):
    return jnp.maximum(x, 0.0) + jnp.log1p(jnp.exp(-jnp.abs(x)))


def _adaln_kernel(cond_ref, w_ref, b_ref, o_ref):
    cond = cond_ref[...]
    s = _silu(cond).astype(BF16)
    o_ref[...] = jnp.dot(s, w_ref[...].astype(BF16), preferred_element_type=F32) + b_ref[...]


def _adaln(cond, w_mod, b_mod):
    rows, n = cond.shape[0], w_mod.shape[1]
    return pl.pallas_call(
        _adaln_kernel,
        grid=(n // ADALN_BLK,),
        in_specs=[pl.BlockSpec((rows, D_MODEL), lambda j: (0, 0)),
                  pl.BlockSpec((D_MODEL, ADALN_BLK), lambda j: (0, j)),
                  pl.BlockSpec((1, ADALN_BLK), lambda j: (0, j))],
        out_specs=pl.BlockSpec((rows, ADALN_BLK), lambda j: (0, j)),
        out_shape=jax.ShapeDtypeStruct((rows, n), F32),
        compiler_params=_cparams(("parallel",)),
        name="adaln",
    )(cond, w_mod, b_mod.reshape(1, n))


def _mod_norm(x, shift, scale, g):
    ms = jnp.mean(x * x, axis=-1, keepdims=True)
    y = x * lax.rsqrt(ms + EPS) * g
    return y * (1.0 + scale) + shift


def _rope(t, cos, sin_signed, first):
    outs = []
    for h in range(H_ATT):
        th = t[:, h * LANES:(h + 1) * LANES]
        partner = jnp.where(first, pltpu.roll(th, LANES - 16, 1), pltpu.roll(th, 16, 1))
        outs.append(th * cos + partner * sin_signed)
    return jnp.concatenate(outs, axis=1)


def _qkv_kernel(*refs, rope, emit_f32):
    x_ref, shift_ref, scale_ref, ng_ref, w_ref = refs[:5]
    pos = 5
    if rope:
        cos_ref, sin_ref = refs[pos:pos + 2]
        pos += 2
    qt_ref, k_ref, vt_ref = refs[pos:pos + 3]
    pos += 3
    if emit_f32:
        k32_ref, v32_ref = refs[pos:pos + 2]

    hb = _mod_norm(x_ref[...], shift_ref[...], scale_ref[...], ng_ref[...]).astype(BF16)
    q = jnp.dot(hb, w_ref[:, 0:D_ATT], preferred_element_type=F32)
    k = jnp.dot(hb, w_ref[:, D_ATT:2 * D_ATT], preferred_element_type=F32)
    v = jnp.dot(hb, w_ref[:, 2 * D_ATT:3 * D_ATT], preferred_element_type=F32)
    if rope:
        cos = cos_ref[...]
        sin_signed = sin_ref[...]
        lane = lax.broadcasted_iota(jnp.int32, cos.shape, 1)
        first = (lane % 32) < 16
        q = _rope(q, cos, sin_signed, first)
        k = _rope(k, cos, sin_signed, first)
    qt_ref[...] = (q * (LOG2E / math.sqrt(HD_QK))).T.astype(BF16)
    k_ref[...] = k.astype(BF16)
    vt_ref[...] = v.T.astype(BF16)
    if emit_f32:
        k32_ref[...] = k
        v32_ref[...] = v


def _qkv_proj(x, mod, mod_row, norm_g, w_bf, rope_tabs, emit_f32, tm):
    B, L, _ = x.shape
    grid = (B, L // tm)
    row = mod_row

    in_specs = [
        pl.BlockSpec((None, tm, D_MODEL), lambda b, i: (b, i, 0)),
        pl.BlockSpec((None, 1, D_MODEL), lambda b, i: (row(b), 0, 0)),
        pl.BlockSpec((None, 1, D_MODEL), lambda b, i: (row(b), 0, 1)),
        pl.BlockSpec((1, D_MODEL), lambda b, i: (0, 0)),
        pl.BlockSpec((D_MODEL, 3 * D_ATT), lambda b, i: (0, 0), pipeline_mode=pl.Buffered(1)),
    ]
    args = [x, mod, mod, norm_g, w_bf]
    if rope_tabs is not None:
        in_specs += [pl.BlockSpec((tm, LANES), lambda b, i: (i, 0)),
                     pl.BlockSpec((tm, LANES), lambda b, i: (i, 0))]
        args += list(rope_tabs)
    out_specs = [
        pl.BlockSpec((None, D_ATT, tm), lambda b, i: (b, 0, i)),
        pl.BlockSpec((None, tm, D_ATT), lambda b, i: (b, i, 0)),
        pl.BlockSpec((None, D_ATT, tm), lambda b, i: (b, 0, i)),
    ]
    out_shape = [
        jax.ShapeDtypeStruct((B, D_ATT, L), BF16),
        jax.ShapeDtypeStruct((B, L, D_ATT), BF16),
        jax.ShapeDtypeStruct((B, D_ATT, L), BF16),
    ]
    if emit_f32:
        out_specs += [pl.BlockSpec((None, tm, D_ATT), lambda b, i: (b, i, 0))] * 2
        out_shape += [jax.ShapeDtypeStruct((B, L, D_ATT), F32)] * 2
    return pl.pallas_call(
        functools.partial(_qkv_kernel, rope=rope_tabs is not None, emit_f32=emit_f32),
        grid=grid, in_specs=in_specs, out_specs=out_specs, out_shape=out_shape,
        compiler_params=_cparams(("parallel", "parallel")),
        name="qkv_proj",
    )(*args)


def _gzx_kernel(x_ref, shift_ref, scale_ref, ng_ref, wg_ref, wz_ref, wx_ref, wbc_ref, wdt_ref,
                g_ref, z_ref, xbc_ref, dtt_ref):
    n_sub = x_ref.shape[0] // SUB_ROWS

    def prologue(r):
        rows = slice(r * SUB_ROWS, (r + 1) * SUB_ROWS)
        return _mod_norm(x_ref[rows, :], shift_ref[...], scale_ref[...], ng_ref[...]).astype(BF16)

    hb = prologue(0)
    for r in range(n_sub):
        rows = slice(r * SUB_ROWS, (r + 1) * SUB_ROWS)
        hb_next = prologue(r + 1) if r + 1 < n_sub else None
        g_ref[rows, :] = jnp.dot(hb, wg_ref[...], preferred_element_type=F32)
        z_ref[rows, :] = jnp.dot(hb, wz_ref[...], preferred_element_type=F32)
        xbc_ref[rows, 0:D_SSD] = jnp.dot(hb, wx_ref[...], preferred_element_type=F32)
        bc = jnp.dot(hb, wbc_ref[...], preferred_element_type=F32)
        nb = SSD_GROUPS * D_STATE
        pieces = []
        for g in range(SSD_GROUPS):
            pieces += [bc[:, g * D_STATE:(g + 1) * D_STATE], bc[:, nb + g * D_STATE:nb + (g + 1) * D_STATE]]
        xbc_ref[rows, D_SSD:CONV_DIM] = jnp.concatenate(pieces, axis=1)
        dt = jnp.dot(hb, wdt_ref[...], preferred_element_type=F32)
        dtt_ref[:, rows] = dt.T[:2 * H_SSD, :]
        hb = hb_next


def _gzx_proj(x, mod, mod_row, norm_g, w_bf, w_dt, tm):
    bc_w = 2 * SSD_GROUPS * D_STATE
    bc_blk = (4 * D_ATT + 2 * D_SSD) // bc_w
    B, L, _ = x.shape
    row = mod_row
    once = pl.Buffered(1)
    in_specs = [
        pl.BlockSpec((None, tm, D_MODEL), lambda b, i: (b, i, 0)),
        pl.BlockSpec((None, 1, D_MODEL), lambda b, i: (row(b), 0, 0)),
        pl.BlockSpec((None, 1, D_MODEL), lambda b, i: (row(b), 0, 1)),
        pl.BlockSpec((1, D_MODEL), lambda b, i: (0, 0)),
        pl.BlockSpec((D_MODEL, D_ATT), lambda b, i: (0, 3), pipeline_mode=once),
        pl.BlockSpec((D_MODEL, D_SSD), lambda b, i: (0, 4), pipeline_mode=once),
        pl.BlockSpec((D_MODEL, D_SSD), lambda b, i: (0, 5), pipeline_mode=once),
        pl.BlockSpec((D_MODEL, bc_w), lambda b, i: (0, bc_blk), pipeline_mode=once),
        pl.BlockSpec((D_MODEL, DT_PAD), lambda b, i: (0, 0), pipeline_mode=once),
    ]
    widths = (D_ATT, D_SSD, CONV_DIM)
    out_specs = [pl.BlockSpec((None, tm, w), lambda b, i: (b, i, 0)) for w in widths]
    out_shape = [jax.ShapeDtypeStruct((B, L, w), F32) for w in widths]
    out_specs.append(pl.BlockSpec((None, 2 * H_SSD, tm), lambda b, i: (b, 0, i)))
    out_shape.append(jax.ShapeDtypeStruct((B, 2 * H_SSD, L), F32))
    return pl.pallas_call(
        _gzx_kernel, grid=(B, L // tm), in_specs=in_specs, out_specs=out_specs, out_shape=out_shape,
        compiler_params=_cparams(("parallel", "parallel")),
        name="gzx_proj",
    )(x, mod, mod, norm_g, w_bf, w_bf, w_bf, w_bf, w_dt)


def _attn_kernel(*refs, n_src, lam_init, carry):
    qt_ref = refs[0]
    pos = 1
    if carry:
        qt_next_ref = refs[pos]
        k_next_refs = refs[pos + 1:pos + 1 + n_src]
        pos += 1 + n_src
    srcs = [(refs[pos + 2 * i], refs[pos + 1 + 2 * i]) for i in range(n_src)]
    pos += 2 * n_src
    g_ref, sg_ref, lq1_ref, lk1_ref, lq2_ref, lk2_ref, o_ref = refs[pos:pos + 7]
    if carry:
        s_sc, m_sc = refs[pos + 7:pos + 9]

    tq = ATT_TQ
    n_heads = qt_ref.shape[0] // HD_V
    chains = [(hh, t) for hh in range(n_heads) for t in range(qt_ref.shape[1] // tq)]
    n_ch = len(chains)
    row = lax.broadcasted_iota(jnp.int32, (HD_V, tq), 0)
    zero = jnp.zeros((HD_V, tq), BF16)

    def stacked_q(ref, hh, t):
        qt = ref[hh * HD_V:(hh + 1) * HD_V, t * tq:(t + 1) * tq]
        return jnp.concatenate([jnp.where(row < HD_QK, qt, zero), jnp.where(row >= HD_QK, qt, zero)], axis=1)

    q2t = [stacked_q(qt_ref, hh, t) for hh, t in chains]

    blocks = []
    for si, (k_ref, vt_ref) in enumerate(srcs):
        n_keys = k_ref.shape[0]
        tk = min(ATT_TK, n_keys)
        blocks += [(k_ref, vt_ref, j * tk, tk, si) for j in range(n_keys // tk)]

    lam = (jnp.exp(jnp.sum(lq1_ref[...] * lk1_ref[...], axis=1, keepdims=True))
           - jnp.exp(jnp.sum(lq2_ref[...] * lk2_ref[...], axis=1, keepdims=True)) + lam_init)

    def score_piece(c, j, handover=None):
        k_ref, _, off, tk, si = blocks[j]
        hh = chains[c][0]
        if handover is not None:
            k_ref = k_next_refs[si]
        s = jnp.dot(k_ref[off:off + tk, hh * HD_V:(hh + 1) * HD_V],
                    q2t[c] if handover is None else handover, preferred_element_type=F32)
        return s, jnp.max(s, axis=0, keepdims=True)

    def prob_piece(c, j, s, m, l, acc):
        _, vt_ref, off, tk, _ = blocks[j]
        hh = chains[c][0]
        p = jnp.exp2(s - m)
        l = l + jnp.sum(p, axis=0, keepdims=True)
        acc = acc + jnp.dot(vt_ref[hh * HD_V:(hh + 1) * HD_V, off:off + tk], p.astype(BF16),
                            preferred_element_type=F32)
        return l, acc

    def epilogue(c, l, acc):
        hh, t = chains[c]
        o_n = acc / l
        o = (o_n[:, :tq] - lam * o_n[:, tq:]).T
        ms = jnp.mean(o * o, axis=-1, keepdims=True)
        att = o * lax.rsqrt(ms + EPS) * sg_ref[...] * (1.0 - lam_init)
        gate = _silu(g_ref[t * tq:(t + 1) * tq, hh * HD_V:(hh + 1) * HD_V])
        o_ref[t * tq:(t + 1) * tq, hh * HD_V:(hh + 1) * HD_V] = (att * gate).astype(BF16)

    n_pc = len(blocks)
    neg_inf = jnp.full((1, 2 * tq), -jnp.inf, F32)
    if not carry:
        prev = None
        for c in range(n_ch + 1):
            cur_s, cur_m = [], neg_inf
            if prev is not None:
                l = jnp.zeros((1, 2 * tq), F32)
                acc = jnp.zeros((HD_V, 2 * tq), F32)
            for j in range(n_pc):
                if c < n_ch:
                    s, s_max = score_piece(c, j)
                    cur_s.append(s)
                    cur_m = jnp.maximum(cur_m, s_max)
                if prev is not None:
                    l, acc = prob_piece(prev[0], j, prev[1][j], prev[2], l, acc)
            if prev is not None:
                epilogue(prev[0], l, acc)
            prev = (c, cur_s, cur_m) if c < n_ch else None
        return

    dst = [sum(b[3] for b in blocks[:j]) for j in range(n_pc)]
    step = ((pl.program_id(0) * pl.num_programs(1) + pl.program_id(1)) * pl.num_programs(2)
            + pl.program_id(2))
    cur, nxt = step % 2, (step + 1) % 2

    @pl.when(step == 0)
    def _():
        m0 = neg_inf
        for j in range(n_pc):
            s, s_max = score_piece(0, j)
            s_sc[0, dst[j]:dst[j] + blocks[j][3], :] = s
            m0 = jnp.maximum(m0, s_max)
        m_sc[0] = jnp.broadcast_to(m0, m_sc.shape[1:])

    hh0, t0 = chains[0]
    q_next = stacked_q(qt_next_ref, hh0, t0)
    prev_s, prev_m = None, m_sc[cur][0:1, :]
    for u in range(n_ch):
        l = jnp.zeros((1, 2 * tq), F32)
        acc = jnp.zeros((HD_V, 2 * tq), F32)
        nxt_s, nxt_m = [], neg_inf
        for j in range(n_pc):
            rows = slice(dst[j], dst[j] + blocks[j][3])
            if u + 1 < n_ch:
                s, s_max = score_piece(u + 1, j)
                nxt_s.append(s)
            else:
                s, s_max = score_piece(0, j, q_next)
                s_sc[nxt, rows, :] = s
            nxt_m = jnp.maximum(nxt_m, s_max)
            s_u = s_sc[cur, rows, :] if u == 0 else prev_s[j]
            l, acc = prob_piece(u, j, s_u, prev_m, l, acc)
        if u + 1 == n_ch:
            m_sc[nxt] = jnp.broadcast_to(nxt_m, m_sc.shape[1:])
        epilogue(u, l, acc)
        prev_s, prev_m = nxt_s, nxt_m


def _attention(qt, kvs, g, subln_g, lams, lam_init, tq, heads_per_step):
    B, _, L = qt.shape
    hw = heads_per_step * HD_V
    nq = L // tq
    nh = H_ATT // heads_per_step
    carry = nq > 1
    in_specs = [pl.BlockSpec((None, hw, tq), lambda b, h, i: (b, h, i))]
    args = [qt]
    scratch = []
    if carry:
        def succ(b, h, i):
            end_i, end_h = i == nq - 1, h == nh - 1
            last = end_i & end_h & (b == B - 1)
            b1 = jnp.where(end_i & end_h & ~last, b + 1, b)
            h1 = jnp.where(last, h, jnp.where(end_i, jnp.where(end_h, 0, h + 1), h))
            i1 = jnp.where(last, i, jnp.where(end_i, 0, i + 1))
            return b1, h1, i1

        def q_next(b, h, i):
            b1, h1, i1 = succ(b, h, i)
            return b1, h1, i1

        def k_next(b, h, i):
            b1, h1, _ = succ(b, h, i)
            return b1, 0, h1

        in_specs.append(pl.BlockSpec((None, hw, tq), q_next))
        args.append(qt)
        for k, _ in kvs:
            in_specs.append(pl.BlockSpec((None, k.shape[1], hw), k_next))
            args.append(k)
        n_all = sum(k.shape[1] for k, _ in kvs)
        scratch = [pltpu.VMEM((2, n_all, 2 * ATT_TQ), F32), pltpu.VMEM((2, SUBLANES, 2 * ATT_TQ), F32)]
    for k, vt in kvs:
        n_keys = k.shape[1]
        in_specs += [pl.BlockSpec((None, n_keys, hw), lambda b, h, i: (b, 0, h)),
                     pl.BlockSpec((None, hw, n_keys), lambda b, h, i: (b, h, 0))]
        args += [k, vt]
    in_specs += [pl.BlockSpec((None, tq, hw), lambda b, h, i: (b, i, h)),
                 pl.BlockSpec((1, HD_V), lambda b, h, i: (0, 0))]
    in_specs += [pl.BlockSpec((1, HD_QK), lambda b, h, i: (0, 0))] * 4
    args += [g, subln_g] + list(lams)
    return pl.pallas_call(
        functools.partial(_attn_kernel, n_src=len(kvs), lam_init=lam_init, carry=carry),
        grid=(B, nh, nq),
        in_specs=in_specs,
        out_specs=pl.BlockSpec((None, tq, hw), lambda b, h, i: (b, i, h)),
        out_shape=jax.ShapeDtypeStruct((B, L, D_ATT), BF16),
        scratch_shapes=scratch,
        compiler_params=_cparams(("arbitrary",) * 3 if carry else ("parallel", "parallel", "arbitrary")),
        name="diff_attention",
    )(*args)


def _ssd_kernel(*refs, seq_len, has_h0):
    (x_ref, bc_ref, dtt_ref, cwx_ref, cwbc_ref, cbx_ref, cbbc_ref,
     dtb_ref, alog_ref, dsk_ref) = refs[:10]
    pos = 10
    if has_h0:
        h0_ref = refs[pos]
        pos += 1
    y_ref, hfin_ref, xt_sc, yt_sc, b_sc, ct_sc = refs[pos:pos + 6]

    nc = seq_len // CHUNK
    n_grp = x_ref.shape[1] // GROUP_W
    row_i = lax.broadcasted_iota(jnp.int32, (CHUNK, CHUNK), 0)
    col_i = lax.broadcasted_iota(jnp.int32, (CHUNK, CHUNK), 1)
    row1 = lax.broadcasted_iota(jnp.int32, (CHUNK, 1), 0)

    def conv_silu(src, w_ref, b_ref, c):
        r0 = pl.multiple_of(c * CHUNK, CHUNK)
        rp = pl.multiple_of(jnp.maximum(r0 - SUBLANES, 0), SUBLANES)
        rn = pl.multiple_of(jnp.minimum(r0 + CHUNK, seq_len - SUBLANES), SUBLANES)
        u = src[pl.ds(r0, CHUNK), :]
        pr = jnp.where(c > 0, src[pl.ds(rp, SUBLANES), :][SUBLANES - 1:SUBLANES, :], 0.0)
        nx = jnp.where(c < nc - 1, src[pl.ds(rn, SUBLANES), :][0:1, :], 0.0)
        prev = jnp.where(row1 == 0, pr, pltpu.roll(u, 1, 0))
        nxt = jnp.where(row1 == CHUNK - 1, nx, pltpu.roll(u, CHUNK - 1, 0))
        w = 0.5 * w_ref[...]
        hx = 0.5 * b_ref[...] + prev * w[0:1, :] + u * w[1:2, :] + nxt * w[2:3, :]
        return hx + hx * jnp.tanh(hx)

    def conv_chunk(c, carry):
        xt = conv_silu(x_ref, cwx_ref, cbx_ref, c).T
        xt_sc[c] = xt
        yt_sc[c] = dsk_ref[...] * xt
        bc = conv_silu(bc_ref, cwbc_ref, cbbc_ref, c)
        for g in range(n_grp):
            bc_g = bc[:, g * LANES:(g + 1) * LANES]
            b_sc[c, g] = bc_g[:, :D_STATE].astype(BF16)
            ct_sc[c, g] = bc_g.T[D_STATE:, :].astype(BF16)
        return carry

    lax.fori_loop(0, nc, conv_chunk, 0, unroll=math.gcd(CONV_UNROLL, nc))

    dir_heads = 2 * HEADS_PER_GROUP
    dtb = [dtb_ref[g] for g in range(n_grp)]
    a_neg = [-jnp.exp(alog_ref[g]) * LOG2E for g in range(n_grp)]

    masks = (row_i <= col_i, row_i >= col_i)
    tri = tuple(jnp.where(m, 1.0, 0.0).astype(BF16) for m in masks)
    ends = (CHUNK - 1, 0)

    def scan_prep(items):
        out = []
        for d in range(2):
            sp_d = []
            for g, dd, c in items:
                if dd == d:
                    t0 = pl.multiple_of(c * CHUNK, CHUNK)
                    dt_g = dtt_ref[g * dir_heads:(g + 1) * dir_heads, pl.ds(t0, CHUNK)]
                    sp_d.append(_softplus(dt_g + dtb[g]))
            x = jnp.concatenate([s * a_neg[it[0]] for s, it in
                                 zip(sp_d, [it for it in items if it[1] == d])], axis=0)
            hi = x.astype(BF16)
            r1 = x - hi.astype(F32)
            mid = r1.astype(BF16)
            lo = (r1 - mid.astype(F32)).astype(BF16)
            parts = jnp.dot(jnp.concatenate([hi, mid, lo], axis=0), tri[d], preferred_element_type=F32)
            nr = x.shape[0]
            cum_d = parts[:nr] + parts[nr:2 * nr] + parts[2 * nr:]
            out.append((jnp.concatenate(sp_d, axis=0), cum_d, cum_d.T))
        return tuple(out)

    def scan_items(items, prep, states):
        n = len(items)
        sp, cum, cum_col, cb_t = [None] * n, [None] * n, [None] * n, [None] * n
        for d in range(2):
            sp_d, cum_d, cum_dt = prep[d]
            for i, k in enumerate([k for k, it in enumerate(items) if it[1] == d]):
                rows = slice(i * dir_heads, (i + 1) * dir_heads)
                sp[k], cum[k], cum_col[k] = sp_d[rows], cum_d[rows], cum_dt[:, rows]
        for k, (g, d, c) in enumerate(items):
            cb_t[k] = jnp.dot(b_sc[c, g], ct_sc[c, g], preferred_element_type=F32)
        xdt_b, lm_b, x_dec, y_scale, s_scale = [], [], [], [], []
        for k, (g, d, c) in enumerate(items):
            xt = xt_sc[c, g * GROUP_W:(g + 1) * GROUP_W, :]
            xdt_k, lm_k, x_dec_k, y_scale_k, s_scale_k = [], [], [], [], []
            for hh in range(HEADS_PER_GROUP):
                r = d * HEADS_PER_GROUP + hh
                a_row = cum[k][r:r + 1, :]
                seg = a_row - cum_col[k][:, r:r + 1]
                lmat = jnp.exp2(jnp.where(masks[d], seg, -jnp.inf))
                xdt = xt[hh * SSD_HEADDIM:(hh + 1) * SSD_HEADDIM, :] * sp[k][r:r + 1, :]
                a_end = a_row[:, ends[d]:ends[d] + 1]
                xdt_k.append(xdt.astype(BF16))
                lm_k.append((cb_t[k] * lmat).astype(BF16))
                x_dec_k.append((xdt * jnp.exp2(a_end - a_row)).astype(BF16))
                y_scale_k.append(jnp.broadcast_to(jnp.exp2(a_row), (SSD_HEADDIM, CHUNK)))
                s_scale_k.append(jnp.broadcast_to(jnp.exp2(a_end), (SSD_HEADDIM, D_STATE)))
            xdt_b.append(xdt_k)
            lm_b.append(lm_k)
            x_dec.append(jnp.concatenate(x_dec_k, axis=0))
            y_scale.append(jnp.concatenate(y_scale_k, axis=0))
            s_scale.append(jnp.concatenate(s_scale_k, axis=0))
        y_diag, st = [], []
        for k, (g, d, c) in enumerate(items):
            y_diag.append(jnp.concatenate(
                [jnp.dot(xdt_b[k][hh], lm_b[k][hh], preferred_element_type=F32)
                 for hh in range(HEADS_PER_GROUP)], axis=0))
            st.append(jnp.dot(x_dec[k], b_sc[c, g], preferred_element_type=F32))
        states = list(states)
        for k, (g, d, c) in enumerate(items):
            si = 2 * g + d
            rows = slice(g * GROUP_W, (g + 1) * GROUP_W)
            y_off = jnp.dot(states[si].astype(BF16), ct_sc[c, g], preferred_element_type=F32)
            yt_sc[c, rows, :] = yt_sc[c, rows, :] + y_diag[k] + y_off * y_scale[k]
            states[si] = states[si] * s_scale[k] + st[k]
        return tuple(states)

    if has_h0:
        s_init = tuple(h0_ref[d, g] for g in range(n_grp) for d in range(2))
    else:
        s_init = (jnp.zeros((GROUP_W, D_STATE), F32),) * (2 * n_grp)

    unroll = math.gcd(max(SCAN_UNROLL // n_grp, 1), nc)
    n_it = nc // unroll

    def items_of(i):
        items = []
        for u in range(unroll):
            for g in range(n_grp):
                items += [(g, 0, i * unroll + u), (g, 1, nc - 1 - (i * unroll + u))]
        return items

    def scan_step(i, carry):
        states, prep = carry
        prep_next = scan_prep(items_of(jnp.minimum(i + 1, n_it - 1)))
        return scan_items(items_of(i), prep, states), prep_next

    s_fin, _ = lax.fori_loop(0, n_it, scan_step, (s_init, scan_prep(items_of(0))))
    for g in range(n_grp):
        for d in range(2):
            hfin_ref[d, g] = s_fin[2 * g + d]

    def emit_chunk(c, carry):
        y_ref[pl.ds(pl.multiple_of(c * CHUNK, CHUNK), CHUNK), :] = yt_sc[c].T
        return carry

    lax.fori_loop(0, nc, emit_chunk, 0, unroll=math.gcd(CONV_UNROLL, nc))


def _ssd(xbc, dtt, conv_w, conv_b, dtb_g, alog_g, dsk, h0, n_grp):
    B, L, _ = xbc.shape
    nc = L // CHUNK
    xw, bw = n_grp * GROUP_W, n_grp * LANES
    bc_blk = D_SSD // bw
    dir_heads = 2 * HEADS_PER_GROUP
    state_spec = pl.BlockSpec((None, 2, n_grp, GROUP_W, D_STATE), lambda b, g: (b, 0, g, 0, 0))
    in_specs = [
        pl.BlockSpec((None, L, xw), lambda b, g: (b, 0, g)),
        pl.BlockSpec((None, L, bw), lambda b, g: (b, 0, bc_blk + g)),
        pl.BlockSpec((None, n_grp * dir_heads, L), lambda b, g: (b, g, 0)),
        pl.BlockSpec((3, xw), lambda b, g: (0, g)),
        pl.BlockSpec((3, bw), lambda b, g: (0, bc_blk + g)),
        pl.BlockSpec((1, xw), lambda b, g: (0, g)),
        pl.BlockSpec((1, bw), lambda b, g: (0, bc_blk + g)),
        pl.BlockSpec((n_grp, dir_heads, LANES), lambda b, g: (g, 0, 0)),
        pl.BlockSpec((n_grp, dir_heads, LANES), lambda b, g: (g, 0, 0)),
        pl.BlockSpec((xw, LANES), lambda b, g: (g, 0)),
    ]
    args = [xbc, xbc, dtt, conv_w, conv_w, conv_b, conv_b, dtb_g, alog_g, dsk]
    if h0 is not None:
        in_specs.append(state_spec)
        args.append(h0)
    return pl.pallas_call(
        functools.partial(_ssd_kernel, seq_len=L, has_h0=h0 is not None),
        grid=(B, SSD_GROUPS // n_grp),
        in_specs=in_specs,
        out_specs=[pl.BlockSpec((None, L, xw), lambda b, g: (b, 0, g)), state_spec],
        out_shape=[jax.ShapeDtypeStruct((B, L, D_SSD), F32),
                   jax.ShapeDtypeStruct((B, 2, SSD_GROUPS, GROUP_W, D_STATE), F32)],
        scratch_shapes=[pltpu.VMEM((nc, xw, CHUNK), F32), pltpu.VMEM((nc, xw, CHUNK), F32),
                        pltpu.VMEM((nc, n_grp, CHUNK, D_STATE), BF16),
                        pltpu.VMEM((nc, n_grp, D_STATE, CHUNK), BF16)],
        compiler_params=_cparams(("parallel", "parallel")),
        name="ssd_scan",
    )(*args)


def _out_kernel(x_ref, att_ref, y_ref, z_ref, sng_ref, w_ref, gate_ref, fg_ref, o_ref):
    n_sub = x_ref.shape[0] // SUB_ROWS

    def prologue(r):
        rows = slice(r * SUB_ROWS, (r + 1) * SUB_ROWS)
        yz = y_ref[rows, :] * _silu(z_ref[rows, :])
        ms = jnp.mean(yz * yz, axis=-1, keepdims=True)
        return (yz * lax.rsqrt(ms + EPS) * sng_ref[...]).astype(BF16)

    ssd_y = prologue(0)
    for r in range(n_sub):
        rows = slice(r * SUB_ROWS, (r + 1) * SUB_ROWS)
        ssd_next = prologue(r + 1) if r + 1 < n_sub else None
        out = jnp.dot(att_ref[rows, :], w_ref[0:D_ATT, :], preferred_element_type=F32)
        out = out + jnp.dot(ssd_y, w_ref[D_ATT:D_ATT + D_SSD, :], preferred_element_type=F32)
        res = x_ref[rows, :] + gate_ref[...] * out
        ms2 = jnp.mean(res * res, axis=-1, keepdims=True)
        o_ref[rows, :] = res * lax.rsqrt(ms2 + EPS) * fg_ref[...]
        ssd_y = ssd_next


def _out_proj(x, att, y, z, ssd_norm_g, w_out, mod, mod_row, final_g, tm):
    B, L, _ = x.shape
    row = mod_row
    tok = lambda b, i: (b, i, 0)
    in_specs = [
        pl.BlockSpec((None, tm, D_MODEL), tok),
        pl.BlockSpec((None, tm, D_ATT), tok),
        pl.BlockSpec((None, tm, D_SSD), tok),
        pl.BlockSpec((None, tm, D_SSD), tok),
        pl.BlockSpec((1, D_SSD), lambda b, i: (0, 0)),
        pl.BlockSpec((D_ATT + D_SSD, D_MODEL), lambda b, i: (0, 0), pipeline_mode=pl.Buffered(1)),
        pl.BlockSpec((None, 1, D_MODEL), lambda b, i: (row(b), 0, 2)),
        pl.BlockSpec((1, D_MODEL), lambda b, i: (0, 0)),
    ]
    return pl.pallas_call(
        _out_kernel, grid=(B, L // tm), in_specs=in_specs,
        out_specs=pl.BlockSpec((None, tm, D_MODEL), tok),
        out_shape=jax.ShapeDtypeStruct((B, L, D_MODEL), F32),
        compiler_params=_cparams(("parallel", "parallel")),
        name="out_proj",
    )(x, att, y, z, ssd_norm_g, w_out, mod, final_g)


def _rope_tables(L):
    rows = L // GRID_W
    row_ids = jnp.repeat(jnp.arange(rows), GRID_W).astype(F32)
    col_ids = jnp.tile(jnp.arange(GRID_W), rows).astype(F32)
    inv = ROPE_BASE ** (-jnp.arange(ROPE_FREQS, dtype=F32) / ROPE_FREQS)
    ang_r = row_ids[:, None] * inv
    ang_c = col_ids[:, None] * inv
    cr, sr, cc, sc = jnp.cos(ang_r), jnp.sin(ang_r), jnp.cos(ang_c), jnp.sin(ang_c)
    cos64 = jnp.concatenate([cr, cr, cc, cc], axis=1)
    sin64 = jnp.concatenate([-sr, sr, -sc, sc], axis=1)
    return jnp.tile(cos64, (1, 2)), jnp.tile(sin64, (1, 2))


def _bc_perm():
    idx = list(range(D_SSD))
    for g in range(SSD_GROUPS):
        idx += [D_SSD + g * D_STATE + n for n in range(D_STATE)]
        idx += [D_SSD + SSD_GROUPS * D_STATE + g * D_STATE + n for n in range(D_STATE)]
    return jnp.asarray(idx, dtype=jnp.int32)


def _dt_perm():
    idx = []
    for g in range(SSD_GROUPS):
        for d in range(2):
            for hh in range(HEADS_PER_GROUP):
                idx.append(d * H_SSD + g * HEADS_PER_GROUP + hh)
    return jnp.asarray(idx, dtype=jnp.int32)


def _group_rows(v2h):
    t = v2h.reshape(2, SSD_GROUPS, HEADS_PER_GROUP).transpose(1, 0, 2).reshape(SSD_GROUPS, 2 * HEADS_PER_GROUP)
    return jnp.broadcast_to(t[:, :, None], (SSD_GROUPS, 2 * HEADS_PER_GROUP, LANES))


def _layer(x, mod, mod_row, params, rope_tabs, past_kv, h0, emit_f32, lam_init, final_g, tm, tq,
           heads_per_step, groups_per_step, tm_out):
    (norm_g, w_bf, w_dt, lams, subln_g, conv_w, conv_b, dtb_g, alog_g, dsk, ssd_norm_g,
     w_out) = params
    proj = _qkv_proj(x, mod, mod_row, norm_g, w_bf, rope_tabs, emit_f32, tm)
    qt, k, vt = proj[:3]
    g, z, xbc, dtt = _gzx_proj(x, mod, mod_row, norm_g, w_bf, w_dt, tm)
    kvs = [(k, vt)]
    if past_kv is not None:
        kvs.append(past_kv)
    att = _attention(qt, kvs, g, subln_g, lams, lam_init, tq, heads_per_step)
    y, hfin = _ssd(xbc, dtt, conv_w, conv_b, dtb_g, alog_g, dsk, h0, groups_per_step)
    out = _out_proj(x, att, y, z, ssd_norm_g, w_out, mod, mod_row, final_g, tm_out)
    return out, proj[3:], hfin


def kernel(x_prompt, x_sample, cache_k, cache_v, state_ssd, c, c_ctx, w_mod, b_mod, norm_g, w_in,
           lambda_q1, lambda_k1, lambda_q2, lambda_k2, subln_g, conv_w, conv_b, dt_bias, A_log,
           D_skip, ssd_norm_g, w_out, final_g):
    b_ctx, l_ctx, _ = x_prompt.shape
    b_dec, l_dec, _ = x_sample.shape
    l_past = cache_k.shape[2]
    depth = w_mod.shape[0]
    assert depth == 1
    lam_init = 0.8 - 0.6 * math.exp(-0.3 * 0)

    w_bf = w_in[0].astype(BF16)
    o2 = 4 * D_ATT + D_SSD
    bc_perm = _bc_perm()
    w_dt = jnp.pad(w_bf[:, o2 + CONV_DIM:][:, _dt_perm()], ((0, 0), (0, DT_PAD - 2 * H_SSD)))
    conv_w_p = conv_w[0][:, bc_perm]
    conv_b_p = conv_b[0][bc_perm].reshape(1, CONV_DIM)
    dtb_g = _group_rows(dt_bias[0])
    alog_g = _group_rows(A_log[0])
    dsk = jnp.broadcast_to(jnp.repeat(D_skip[0], SSD_HEADDIM)[:, None], (D_SSD, LANES))
    lams = [a[0].reshape(1, HD_QK) for a in (lambda_q1, lambda_k1, lambda_q2, lambda_k2)]
    params = (norm_g[0].reshape(1, D_MODEL), w_bf, w_dt, lams, subln_g[0].reshape(1, HD_V),
              conv_w_p, conv_b_p, dtb_g, alog_g, dsk, ssd_norm_g[0].reshape(1, D_SSD),
              w_out[0].astype(BF16))
    fg = final_g.reshape(1, D_MODEL)

    assert b_dec < SUBLANES
    cond = jnp.zeros((SUBLANES, D_MODEL), F32).at[:b_dec].set(c).at[b_dec].set(c_ctx)
    mod = _adaln(cond, w_mod[0], b_mod[0]).reshape(SUBLANES, 1, 3 * D_MODEL)

    y_prompt, (k_ctx, v_ctx), h_ctx = _layer(
        x_prompt, mod, lambda b: b_dec, params, None, None, None, True, lam_init, fg,
        tm=256, tq=256, heads_per_step=H_ATT, groups_per_step=SSD_GROUPS, tm_out=256)

    rope_tabs = _rope_tables(l_dec)
    k_past = cache_k[:, 0].reshape(b_dec, l_past, D_ATT).astype(BF16)
    vt_past = cache_v[:, 0].reshape(b_dec, l_past, D_ATT).transpose(0, 2, 1).astype(BF16)
    h0 = state_ssd[:, 0].reshape(b_dec, 2, SSD_GROUPS, GROUP_W, D_STATE)
    y_sample, _, _ = _layer(
        x_sample, mod, lambda b: b, params, rope_tabs, (k_past, vt_past), h0, False, lam_init, fg,
        tm=512, tq=512, heads_per_step=1, groups_per_step=1, tm_out=1024)

    new_cache_k = k_ctx.reshape(b_ctx, 1, l_ctx, H_ATT, 2 * HD_QK)
    new_cache_v = v_ctx.reshape(b_ctx, 1, l_ctx, H_ATT, HD_V)
    new_state = h_ctx.reshape(b_ctx, 1, 2, H_SSD, SSD_HEADDIM, D_STATE)
    return (y_prompt, y_sample, new_cache_k, new_cache_v, new_state)
```

```python
import functools
import math

import jax
import jax.numpy as jnp
from jax import lax
from jax.experimental import pallas as pl
from jax.experimental.pallas import tpu as pltpu

F32 = jnp.float32
BF16 = jnp.bfloat16

D_MODEL = 1024
GRID_W = 64
HD_QK = 64
HD_V = 128
H_ATT = 8
D_ATT = 1024
D_SSD = 1024
SSD_HEADDIM = 64
H_SSD = 16
SSD_GROUPS = 4
HEADS_PER_GROUP = 4
D_STATE = 64
CONV_DIM = D_SSD + 2 * SSD_GROUPS * D_STATE
CHUNK = 128
ROPE_BASE = 10000.0
ROPE_FREQS = 16
EPS = 1e-6

LANES = 128
SUBLANES = 8
DT_PAD = LANES
GROUP_W = HEADS_PER_GROUP * SSD_HEADDIM
VMEM_LIMIT = 56 * 1024 * 1024
ADALN_BLK = 1024
ATT_TK = 512
ATT_TQ = 256
SUB_ROWS = 256
LOG2E = 1.4426950408889634
CONV_UNROLL = 8
SCAN_UNROLL = 8


def _cparams(sem):
    return pltpu.CompilerParams(dimension_semantics=sem, vmem_limit_bytes=VMEM_LIMIT)


def _silu(x):
    hx = 0.5 * x
    return hx + hx * jnp.tanh(hx)


def _softplus(x):
    return jnp.maximum(x, 0.0) + jnp.log1p(jnp.exp(-jnp.abs(x)))


def _adaln_kernel(cond_ref, w_ref, b_ref, o_ref):
    cond = cond_ref[...]
    s = _silu(cond).astype(BF16)
    o_ref[...] = jnp.dot(s, w_ref[...].astype(BF16), preferred_element_type=F32) + b_ref[...]


def _adaln(cond, w_mod, b_mod):
    rows, n = cond.shape[0], w_mod.shape[1]
    return pl.pallas_call(
        _adaln_kernel,
        grid=(n // ADALN_BLK,),
        in_specs=[pl.BlockSpec((rows, D_MODEL), lambda j: (0, 0)),
                  pl.BlockSpec((D_MODEL, ADALN_BLK), lambda j: (0, j)),
                  pl.BlockSpec((1, ADALN_BLK), lambda j: (0, j))],
        out_specs=pl.BlockSpec((rows, ADALN_BLK), lambda j: (0, j)),
        out_shape=jax.ShapeDtypeStruct((rows, n), F32),
        compiler_params=_cparams(("parallel",)),
        name="adaln",
    )(cond, w_mod, b_mod.reshape(1, n))


def _mod_norm(x, shift, scale, g):
    ms = jnp.mean(x * x, axis=-1, keepdims=True)
    y = x * lax.rsqrt(ms + EPS) * g
    return y * (1.0 + scale) + shift


def _rope(t, cos, sin_signed, first):
    outs = []
    for h in range(H_ATT):
        th = t[:, h * LANES:(h + 1) * LANES]
        partner = jnp.where(first, pltpu.roll(th, LANES - ROPE_FREQS, 1), pltpu.roll(th, ROPE_FREQS, 1))
        outs.append(th * cos + partner * sin_signed)
    return jnp.concatenate(outs, axis=1)


def _qkv_kernel(*refs, rope, emit_f32):
    x_ref, shift_ref, scale_ref, ng_ref, w_ref = refs[:5]
    pos = 5
    if rope:
        cos_ref, sin_ref = refs[pos:pos + 2]
        pos += 2
    qt_ref, k_ref, vt_ref = refs[pos:pos + 3]
    pos += 3
    if emit_f32:
        k32_ref, v32_ref = refs[pos:pos + 2]

    hb = _mod_norm(x_ref[...], shift_ref[...], scale_ref[...], ng_ref[...]).astype(BF16)
    q = jnp.dot(hb, w_ref[:, 0:D_ATT], preferred_element_type=F32)
    k = jnp.dot(hb, w_ref[:, D_ATT:2 * D_ATT], preferred_element_type=F32)
    v = jnp.dot(hb, w_ref[:, 2 * D_ATT:3 * D_ATT], preferred_element_type=F32)
    if rope:
        cos = cos_ref[...]
        sin_signed = sin_ref[...]
        lane = lax.broadcasted_iota(jnp.int32, cos.shape, 1)
        first = (lane % (2 * ROPE_FREQS)) < ROPE_FREQS
        q = _rope(q, cos, sin_signed, first)
        k = _rope(k, cos, sin_signed, first)
    qt_ref[...] = (q * (LOG2E / math.sqrt(HD_QK))).T.astype(BF16)
    k_ref[...] = k.astype(BF16)
    vt_ref[...] = v.T.astype(BF16)
    if emit_f32:
        k32_ref[...] = k
        v32_ref[...] = v


def _qkv_proj(x, mod, mod_row, norm_g, w_bf, rope_tabs, emit_f32, tm):
    B, L, _ = x.shape
    grid = (B, L // tm)
    row = mod_row

    in_specs = [
        pl.BlockSpec((None, tm, D_MODEL), lambda b, i: (b, i, 0)),
        pl.BlockSpec((None, 1, D_MODEL), lambda b, i: (row(b), 0, 0)),
        pl.BlockSpec((None, 1, D_MODEL), lambda b, i: (row(b), 0, 1)),
        pl.BlockSpec((1, D_MODEL), lambda b, i: (0, 0)),
        pl.BlockSpec((D_MODEL, 3 * D_ATT), lambda b, i: (0, 0), pipeline_mode=pl.Buffered(1)),
    ]
    args = [x, mod, mod, norm_g, w_bf]
    if rope_tabs is not None:
        in_specs += [pl.BlockSpec((tm, LANES), lambda b, i: (i, 0)),
                     pl.BlockSpec((tm, LANES), lambda b, i: (i, 0))]
        args += list(rope_tabs)
    out_specs = [
        pl.BlockSpec((None, D_ATT, tm), lambda b, i: (b, 0, i)),
        pl.BlockSpec((None, tm, D_ATT), lambda b, i: (b, i, 0)),
        pl.BlockSpec((None, D_ATT, tm), lambda b, i: (b, 0, i)),
    ]
    out_shape = [
        jax.ShapeDtypeStruct((B, D_ATT, L), BF16),
        jax.ShapeDtypeStruct((B, L, D_ATT), BF16),
        jax.ShapeDtypeStruct((B, D_ATT, L), BF16),
    ]
    if emit_f32:
        out_specs += [pl.BlockSpec((None, tm, D_ATT), lambda b, i: (b, i, 0))] * 2
        out_shape += [jax.ShapeDtypeStruct((B, L, D_ATT), F32)] * 2
    return pl.pallas_call(
        functools.partial(_qkv_kernel, rope=rope_tabs is not None, emit_f32=emit_f32),
        grid=grid, in_specs=in_specs, out_specs=out_specs, out_shape=out_shape,
        compiler_params=_cparams(("parallel", "parallel")),
        name="qkv_proj",
    )(*args)


def _gzx_kernel(x_ref, shift_ref, scale_ref, ng_ref, wg_ref, wz_ref, wx_ref, wbc_ref, wdt_ref,
                g_ref, z_ref, xbc_ref, dtt_ref):
    n_sub = x_ref.shape[0] // SUB_ROWS

    def prologue(r):
        rows = slice(r * SUB_ROWS, (r + 1) * SUB_ROWS)
        return _mod_norm(x_ref[rows, :], shift_ref[...], scale_ref[...], ng_ref[...]).astype(BF16)

    hb = prologue(0)
    for r in range(n_sub):
        rows = slice(r * SUB_ROWS, (r + 1) * SUB_ROWS)
        hb_next = prologue(r + 1) if r + 1 < n_sub else None
        g_ref[rows, :] = jnp.dot(hb, wg_ref[...], preferred_element_type=F32)
        z_ref[rows, :] = jnp.dot(hb, wz_ref[...], preferred_element_type=F32)
        xbc_ref[rows, 0:D_SSD] = jnp.dot(hb, wx_ref[...], preferred_element_type=F32)
        bc = jnp.dot(hb, wbc_ref[...], preferred_element_type=F32)
        nb = SSD_GROUPS * D_STATE
        pieces = []
        for g in range(SSD_GROUPS):
            pieces += [bc[:, g * D_STATE:(g + 1) * D_STATE], bc[:, nb + g * D_STATE:nb + (g + 1) * D_STATE]]
        xbc_ref[rows, D_SSD:CONV_DIM] = jnp.concatenate(pieces, axis=1)
        dt = jnp.dot(hb, wdt_ref[...], preferred_element_type=F32)
        dtt_ref[:, rows] = dt.T[:2 * H_SSD, :]
        hb = hb_next


def _gzx_proj(x, mod, mod_row, norm_g, w_bf, w_dt, tm):
    bc_w = 2 * SSD_GROUPS * D_STATE
    bc_blk = (4 * D_ATT + 2 * D_SSD) // bc_w
    B, L, _ = x.shape
    row = mod_row
    once = pl.Buffered(1)
    in_specs = [
        pl.BlockSpec((None, tm, D_MODEL), lambda b, i: (b, i, 0)),
        pl.BlockSpec((None, 1, D_MODEL), lambda b, i: (row(b), 0, 0)),
        pl.BlockSpec((None, 1, D_MODEL), lambda b, i: (row(b), 0, 1)),
        pl.BlockSpec((1, D_MODEL), lambda b, i: (0, 0)),
        pl.BlockSpec((D_MODEL, D_ATT), lambda b, i: (0, 3), pipeline_mode=once),
        pl.BlockSpec((D_MODEL, D_SSD), lambda b, i: (0, 4), pipeline_mode=once),
        pl.BlockSpec((D_MODEL, D_SSD), lambda b, i: (0, 5), pipeline_mode=once),
        pl.BlockSpec((D_MODEL, bc_w), lambda b, i: (0, bc_blk), pipeline_mode=once),
        pl.BlockSpec((D_MODEL, DT_PAD), lambda b, i: (0, 0), pipeline_mode=once),
    ]
    widths = (D_ATT, D_SSD, CONV_DIM)
    out_specs = [pl.BlockSpec((None, tm, w), lambda b, i: (b, i, 0)) for w in widths]
    out_shape = [jax.ShapeDtypeStruct((B, L, w), F32) for w in widths]
    out_specs.append(pl.BlockSpec((None, 2 * H_SSD, tm), lambda b, i: (b, 0, i)))
    out_shape.append(jax.ShapeDtypeStruct((B, 2 * H_SSD, L), F32))
    return pl.pallas_call(
        _gzx_kernel, grid=(B, L // tm), in_specs=in_specs, out_specs=out_specs, out_shape=out_shape,
        compiler_params=_cparams(("parallel", "parallel")),
        name="gzx_proj",
    )(x, mod, mod, norm_g, w_bf, w_bf, w_bf, w_bf, w_dt)


def _attn_kernel(*refs, n_src, lam_init, carry):
    qt_ref = refs[0]
    pos = 1
    if carry:
        qt_next_ref = refs[pos]
        k_next_refs = refs[pos + 1:pos + 1 + n_src]
        pos += 1 + n_src
    srcs = [(refs[pos + 2 * i], refs[pos + 1 + 2 * i]) for i in range(n_src)]
    pos += 2 * n_src
    g_ref, sg_ref, lq1_ref, lk1_ref, lq2_ref, lk2_ref, o_ref = refs[pos:pos + 7]
    if carry:
        s_sc, m_sc = refs[pos + 7:pos + 9]

    tq = ATT_TQ
    n_heads = qt_ref.shape[0] // HD_V
    chains = [(hh, t) for hh in range(n_heads) for t in range(qt_ref.shape[1] // tq)]
    n_ch = len(chains)
    row = lax.broadcasted_iota(jnp.int32, (HD_V, tq), 0)
    zero = jnp.zeros((HD_V, tq), BF16)

    def stacked_q(ref, hh, t):
        qt = ref[hh * HD_V:(hh + 1) * HD_V, t * tq:(t + 1) * tq]
        return jnp.concatenate([jnp.where(row < HD_QK, qt, zero), jnp.where(row >= HD_QK, qt, zero)], axis=1)

    q2t = [stacked_q(qt_ref, hh, t) for hh, t in chains]

    blocks = []
    for si, (k_ref, vt_ref) in enumerate(srcs):
        n_keys = k_ref.shape[0]
        tk = min(ATT_TK, n_keys)
        blocks += [(k_ref, vt_ref, j * tk, tk, si) for j in range(n_keys // tk)]

    lam = (jnp.exp(jnp.sum(lq1_ref[...] * lk1_ref[...], axis=1, keepdims=True))
           - jnp.exp(jnp.sum(lq2_ref[...] * lk2_ref[...], axis=1, keepdims=True)) + lam_init)

    def score_piece(c, j, handover=None):
        k_ref, _, off, tk, si = blocks[j]
        hh = chains[c][0]
        if handover is not None:
            k_ref = k_next_refs[si]
        s = jnp.dot(k_ref[off:off + tk, hh * HD_V:(hh + 1) * HD_V],
                    q2t[c] if handover is None else handover, preferred_element_type=F32)
        return s, jnp.max(s, axis=0, keepdims=True)

    def prob_piece(c, j, s, m, l, acc):
        _, vt_ref, off, tk, _ = blocks[j]
        hh = chains[c][0]
        p = jnp.exp2(s - m)
        l = l + jnp.sum(p, axis=0, keepdims=True)
        acc = acc + jnp.dot(vt_ref[hh * HD_V:(hh + 1) * HD_V, off:off + tk], p.astype(BF16),
                            preferred_element_type=F32)
        return l, acc

    def epilogue(c, l, acc):
        hh, t = chains[c]
        o_n = acc / l
        o = (o_n[:, :tq] - lam * o_n[:, tq:]).T
        ms = jnp.mean(o * o, axis=-1, keepdims=True)
        att = o * lax.rsqrt(ms + EPS) * sg_ref[...] * (1.0 - lam_init)
        gate = _silu(g_ref[t * tq:(t + 1) * tq, hh * HD_V:(hh + 1) * HD_V])
        o_ref[t * tq:(t + 1) * tq, hh * HD_V:(hh + 1) * HD_V] = (att * gate).astype(BF16)

    n_pc = len(blocks)
    neg_inf = jnp.full((1, 2 * tq), -jnp.inf, F32)
    if not carry:
        prev = None
        for c in range(n_ch + 1):
            cur_s, cur_m = [], neg_inf
            if prev is not None:
                l = jnp.zeros((1, 2 * tq), F32)
                acc = jnp.zeros((HD_V, 2 * tq), F32)
            for j in range(n_pc):
                if c < n_ch:
                    s, s_max = score_piece(c, j)
                    cur_s.append(s)
                    cur_m = jnp.maximum(cur_m, s_max)
                if prev is not None:
                    l, acc = prob_piece(prev[0], j, prev[1][j], prev[2], l, acc)
            if prev is not None:
                epilogue(prev[0], l, acc)
            prev = (c, cur_s, cur_m) if c < n_ch else None
        return

    dst = [sum(b[3] for b in blocks[:j]) for j in range(n_pc)]
    step = ((pl.program_id(0) * pl.num_programs(1) + pl.program_id(1)) * pl.num_programs(2)
            + pl.program_id(2))
    cur, nxt = step % 2, (step + 1) % 2

    @pl.when(step == 0)
    def _():
        m0 = neg_inf
        for j in range(n_pc):
            s, s_max = score_piece(0, j)
            s_sc[0, dst[j]:dst[j] + blocks[j][3], :] = s
            m0 = jnp.maximum(m0, s_max)
        m_sc[0] = jnp.broadcast_to(m0, m_sc.shape[1:])

    hh0, t0 = chains[0]
    q_next = stacked_q(qt_next_ref, hh0, t0)
    prev_s, prev_m = None, m_sc[cur][0:1, :]
    for u in range(n_ch):
        l = jnp.zeros((1, 2 * tq), F32)
        acc = jnp.zeros((HD_V, 2 * tq), F32)
        nxt_s, nxt_m = [], neg_inf
        for j in range(n_pc):
            rows = slice(dst[j], dst[j] + blocks[j][3])
            if u + 1 < n_ch:
                s, s_max = score_piece(u + 1, j)
                nxt_s.append(s)
            else:
                s, s_max = score_piece(0, j, q_next)
                s_sc[nxt, rows, :] = s
            nxt_m = jnp.maximum(nxt_m, s_max)
            s_u = s_sc[cur, rows, :] if u == 0 else prev_s[j]
            l, acc = prob_piece(u, j, s_u, prev_m, l, acc)
        if u + 1 == n_ch:
            m_sc[nxt] = jnp.broadcast_to(nxt_m, m_sc.shape[1:])
        epilogue(u, l, acc)
        prev_s, prev_m = nxt_s, nxt_m


def _attention(qt, kvs, g, subln_g, lams, lam_init, tq, heads_per_step):
    B, _, L = qt.shape
    hw = heads_per_step * HD_V
    nq = L // tq
    nh = H_ATT // heads_per_step
    carry = nq > 1
    in_specs = [pl.BlockSpec((None, hw, tq), lambda b, h, i: (b, h, i))]
    args = [qt]
    scratch = []
    if carry:
        def succ(b, h, i):
            end_i, end_h = i == nq - 1, h == nh - 1
            last = end_i & end_h & (b == B - 1)
            b1 = jnp.where(end_i & end_h & ~last, b + 1, b)
            h1 = jnp.where(last, h, jnp.where(end_i, jnp.where(end_h, 0, h + 1), h))
            i1 = jnp.where(last, i, jnp.where(end_i, 0, i + 1))
            return b1, h1, i1

        def q_next(b, h, i):
            b1, h1, i1 = succ(b, h, i)
            return b1, h1, i1

        def k_next(b, h, i):
            b1, h1, _ = succ(b, h, i)
            return b1, 0, h1

        in_specs.append(pl.BlockSpec((None, hw, tq), q_next))
        args.append(qt)
        for k, _ in kvs:
            in_specs.append(pl.BlockSpec((None, k.shape[1], hw), k_next))
            args.append(k)
        n_all = sum(k.shape[1] for k, _ in kvs)
        scratch = [pltpu.VMEM((2, n_all, 2 * ATT_TQ), F32), pltpu.VMEM((2, SUBLANES, 2 * ATT_TQ), F32)]
    for k, vt in kvs:
        n_keys = k.shape[1]
        in_specs += [pl.BlockSpec((None, n_keys, hw), lambda b, h, i: (b, 0, h)),
                     pl.BlockSpec((None, hw, n_keys), lambda b, h, i: (b, h, 0))]
        args += [k, vt]
    in_specs += [pl.BlockSpec((None, tq, hw), lambda b, h, i: (b, i, h)),
                 pl.BlockSpec((1, HD_V), lambda b, h, i: (0, 0))]
    in_specs += [pl.BlockSpec((1, HD_QK), lambda b, h, i: (0, 0))] * 4
    args += [g, subln_g] + list(lams)
    return pl.pallas_call(
        functools.partial(_attn_kernel, n_src=len(kvs), lam_init=lam_init, carry=carry),
        grid=(B, nh, nq),
        in_specs=in_specs,
        out_specs=pl.BlockSpec((None, tq, hw), lambda b, h, i: (b, i, h)),
        out_shape=jax.ShapeDtypeStruct((B, L, D_ATT), BF16),
        scratch_shapes=scratch,
        compiler_params=_cparams(("arbitrary",) * 3 if carry else ("parallel", "parallel", "arbitrary")),
        name="diff_attention",
    )(*args)


def _ssd_kernel(*refs, seq_len, has_h0):
    (x_ref, bc_ref, dtt_ref, cwx_ref, cwbc_ref, cbx_ref, cbbc_ref,
     dtb_ref, alog_ref, dsk_ref) = refs[:10]
    pos = 10
    if has_h0:
        h0_ref = refs[pos]
        pos += 1
    y_ref, hfin_ref, xt_sc, yt_sc, b_sc, ct_sc = refs[pos:pos + 6]

    nc = seq_len // CHUNK
    n_grp = x_ref.shape[1] // GROUP_W
    row_i = lax.broadcasted_iota(jnp.int32, (CHUNK, CHUNK), 0)
    col_i = lax.broadcasted_iota(jnp.int32, (CHUNK, CHUNK), 1)
    row1 = lax.broadcasted_iota(jnp.int32, (CHUNK, 1), 0)

    def conv_silu(src, w_ref, b_ref, c):
        r0 = pl.multiple_of(c * CHUNK, CHUNK)
        rp = pl.multiple_of(jnp.maximum(r0 - SUBLANES, 0), SUBLANES)
        rn = pl.multiple_of(jnp.minimum(r0 + CHUNK, seq_len - SUBLANES), SUBLANES)
        u = src[pl.ds(r0, CHUNK), :]
        pr = jnp.where(c > 0, src[pl.ds(rp, SUBLANES), :][SUBLANES - 1:SUBLANES, :], 0.0)
        nx = jnp.where(c < nc - 1, src[pl.ds(rn, SUBLANES), :][0:1, :], 0.0)
        prev = jnp.where(row1 == 0, pr, pltpu.roll(u, 1, 0))
        nxt = jnp.where(row1 == CHUNK - 1, nx, pltpu.roll(u, CHUNK - 1, 0))
        w = 0.5 * w_ref[...]
        hx = 0.5 * b_ref[...] + prev * w[0:1, :] + u * w[1:2, :] + nxt * w[2:3, :]
        return hx + hx * jnp.tanh(hx)

    def conv_chunk(c, carry):
        xt = conv_silu(x_ref, cwx_ref, cbx_ref, c).T
        xt_sc[c] = xt
        yt_sc[c] = dsk_ref[...] * xt
        bc = conv_silu(bc_ref, cwbc_ref, cbbc_ref, c)
        for g in range(n_grp):
            bc_g = bc[:, g * LANES:(g + 1) * LANES]
            b_sc[c, g] = bc_g[:, :D_STATE].astype(BF16)
            ct_sc[c, g] = bc_g.T[D_STATE:, :].astype(BF16)
        return carry

    lax.fori_loop(0, nc, conv_chunk, 0, unroll=math.gcd(CONV_UNROLL, nc))

    dir_heads = 2 * HEADS_PER_GROUP
    dtb = [dtb_ref[g] for g in range(n_grp)]
    a_neg = [-jnp.exp(alog_ref[g]) * LOG2E for g in range(n_grp)]

    masks = (row_i <= col_i, row_i >= col_i)
    tri = tuple(jnp.where(m, 1.0, 0.0).astype(BF16) for m in masks)
    ends = (CHUNK - 1, 0)

    def scan_prep(items):
        out = []
        for d in range(2):
            sp_d = []
            for g, dd, c in items:
                if dd == d:
                    t0 = pl.multiple_of(c * CHUNK, CHUNK)
                    dt_g = dtt_ref[g * dir_heads:(g + 1) * dir_heads, pl.ds(t0, CHUNK)]
                    sp_d.append(_softplus(dt_g + dtb[g]))
            x = jnp.concatenate([s * a_neg[it[0]] for s, it in
                                 zip(sp_d, [it for it in items if it[1] == d])], axis=0)
            hi = x.astype(BF16)
            r1 = x - hi.astype(F32)
            mid = r1.astype(BF16)
            lo = (r1 - mid.astype(F32)).astype(BF16)
            parts = jnp.dot(jnp.concatenate([hi, mid, lo], axis=0), tri[d], preferred_element_type=F32)
            nr = x.shape[0]
            cum_d = parts[:nr] + parts[nr:2 * nr] + parts[2 * nr:]
            out.append((jnp.concatenate(sp_d, axis=0), cum_d, cum_d.T))
        return tuple(out)

    def scan_items(items, prep, states):
        n = len(items)
        sp, cum, cum_col, cb_t = [None] * n, [None] * n, [None] * n, [None] * n
        for d in range(2):
            sp_d, cum_d, cum_dt = prep[d]
            for i, k in enumerate([k for k, it in enumerate(items) if it[1] == d]):
                rows = slice(i * dir_heads, (i + 1) * dir_heads)
                sp[k], cum[k], cum_col[k] = sp_d[rows], cum_d[rows], cum_dt[:, rows]
        for k, (g, d, c) in enumerate(items):
            cb_t[k] = jnp.dot(b_sc[c, g], ct_sc[c, g], preferred_element_type=F32)
        xdt_b, lm_b, x_dec, y_scale, s_scale = [], [], [], [], []
        for k, (g, d, c) in enumerate(items):
            xt = xt_sc[c, g * GROUP_W:(g + 1) * GROUP_W, :]
            xdt_k, lm_k, x_dec_k, y_scale_k, s_scale_k = [], [], [], [], []
            for hh in range(HEADS_PER_GROUP):
                r = d * HEADS_PER_GROUP + hh
                a_row = cum[k][r:r + 1, :]
                seg = a_row - cum_col[k][:, r:r + 1]
                lmat = jnp.exp2(jnp.where(masks[d], seg, -jnp.inf))
                xdt = xt[hh * SSD_HEADDIM:(hh + 1) * SSD_HEADDIM, :] * sp[k][r:r + 1, :]
                a_end = a_row[:, ends[d]:ends[d] + 1]
                xdt_k.append(xdt.astype(BF16))
                lm_k.append((cb_t[k] * lmat).astype(BF16))
                x_dec_k.append((xdt * jnp.exp2(a_end - a_row)).astype(BF16))
                y_scale_k.append(jnp.broadcast_to(jnp.exp2(a_row), (SSD_HEADDIM, CHUNK)))
                s_scale_k.append(jnp.broadcast_to(jnp.exp2(a_end), (SSD_HEADDIM, D_STATE)))
            xdt_b.append(xdt_k)
            lm_b.append(lm_k)
            x_dec.append(jnp.concatenate(x_dec_k, axis=0))
            y_scale.append(jnp.concatenate(y_scale_k, axis=0))
            s_scale.append(jnp.concatenate(s_scale_k, axis=0))
        y_diag, st = [], []
        for k, (g, d, c) in enumerate(items):
            y_diag.append(jnp.concatenate(
                [jnp.dot(xdt_b[k][hh], lm_b[k][hh], preferred_element_type=F32)
                 for hh in range(HEADS_PER_GROUP)], axis=0))
            st.append(jnp.dot(x_dec[k], b_sc[c, g], preferred_element_type=F32))
        states = list(states)
        for k, (g, d, c) in enumerate(items):
            si = 2 * g + d
            rows = slice(g * GROUP_W, (g + 1) * GROUP_W)
            y_off = jnp.dot(states[si].astype(BF16), ct_sc[c, g], preferred_element_type=F32)
            yt_sc[c, rows, :] = yt_sc[c, rows, :] + y_diag[k] + y_off * y_scale[k]
            states[si] = states[si] * s_scale[k] + st[k]
        return tuple(states)

    if has_h0:
        s_init = tuple(h0_ref[d, g] for g in range(n_grp) for d in range(2))
    else:
        s_init = (jnp.zeros((GROUP_W, D_STATE), F32),) * (2 * n_grp)

    unroll = math.gcd(max(SCAN_UNROLL // n_grp, 1), nc)
    n_it = nc // unroll

    def items_of(i):
        items = []
        for u in range(unroll):
            for g in range(n_grp):
                items += [(g, 0, i * unroll + u), (g, 1, nc - 1 - (i * unroll + u))]
        return items

    def scan_step(i, carry):
        states, prep = carry
        prep_next = scan_prep(items_of(jnp.minimum(i + 1, n_it - 1)))
        return scan_items(items_of(i), prep, states), prep_next

    s_fin, _ = lax.fori_loop(0, n_it, scan_step, (s_init, scan_prep(items_of(0))))
    for g in range(n_grp):
        for d in range(2):
            hfin_ref[d, g] = s_fin[2 * g + d]

    def emit_chunk(c, carry):
        y_ref[pl.ds(pl.multiple_of(c * CHUNK, CHUNK), CHUNK), :] = yt_sc[c].T
        return carry

    lax.fori_loop(0, nc, emit_chunk, 0, unroll=math.gcd(CONV_UNROLL, nc))


def _ssd(xbc, dtt, conv_w, conv_b, dtb_g, alog_g, dsk, h0, n_grp):
    B, L, _ = xbc.shape
    nc = L // CHUNK
    xw, bw = n_grp * GROUP_W, n_grp * LANES
    bc_blk = D_SSD // bw
    dir_heads = 2 * HEADS_PER_GROUP
    state_spec = pl.BlockSpec((None, 2, n_grp, GROUP_W, D_STATE), lambda b, g: (b, 0, g, 0, 0))
    in_specs = [
        pl.BlockSpec((None, L, xw), lambda b, g: (b, 0, g)),
        pl.BlockSpec((None, L, bw), lambda b, g: (b, 0, bc_blk + g)),
        pl.BlockSpec((None, n_grp * dir_heads, L), lambda b, g: (b, g, 0)),
        pl.BlockSpec((3, xw), lambda b, g: (0, g)),
        pl.BlockSpec((3, bw), lambda b, g: (0, bc_blk + g)),
        pl.BlockSpec((1, xw), lambda b, g: (0, g)),
        pl.BlockSpec((1, bw), lambda b, g: (0, bc_blk + g)),
        pl.BlockSpec((n_grp, dir_heads, LANES), lambda b, g: (g, 0, 0)),
        pl.BlockSpec((n_grp, dir_heads, LANES), lambda b, g: (g, 0, 0)),
        pl.BlockSpec((xw, LANES), lambda b, g: (g, 0)),
    ]
    args = [xbc, xbc, dtt, conv_w, conv_w, conv_b, conv_b, dtb_g, alog_g, dsk]
    if h0 is not None:
        in_specs.append(state_spec)
        args.append(h0)
    return pl.pallas_call(
        functools.partial(_ssd_kernel, seq_len=L, has_h0=h0 is not None),
        grid=(B, SSD_GROUPS // n_grp),
        in_specs=in_specs,
        out_specs=[pl.BlockSpec((None, L, xw), lambda b, g: (b, 0, g)), state_spec],
        out_shape=[jax.ShapeDtypeStruct((B, L, D_SSD), F32),
                   jax.ShapeDtypeStruct((B, 2, SSD_GROUPS, GROUP_W, D_STATE), F32)],
        scratch_shapes=[pltpu.VMEM((nc, xw, CHUNK), F32), pltpu.VMEM((nc, xw, CHUNK), F32),
                        pltpu.VMEM((nc, n_grp, CHUNK, D_STATE), BF16),
                        pltpu.VMEM((nc, n_grp, D_STATE, CHUNK), BF16)],
        compiler_params=_cparams(("parallel", "parallel")),
        name="ssd_scan",
    )(*args)


def _out_kernel(x_ref, att_ref, y_ref, z_ref, sng_ref, w_ref, gate_ref, fg_ref, o_ref):
    n_sub = x_ref.shape[0] // SUB_ROWS

    def prologue(r):
        rows = slice(r * SUB_ROWS, (r + 1) * SUB_ROWS)
        yz = y_ref[rows, :] * _silu(z_ref[rows, :])
        ms = jnp.mean(yz * yz, axis=-1, keepdims=True)
        return (yz * lax.rsqrt(ms + EPS) * sng_ref[...]).astype(BF16)

    ssd_y = prologue(0)
    for r in range(n_sub):
        rows = slice(r * SUB_ROWS, (r + 1) * SUB_ROWS)
        ssd_next = prologue(r + 1) if r + 1 < n_sub else None
        out = jnp.dot(att_ref[rows, :], w_ref[0:D_ATT, :], preferred_element_type=F32)
        out = out + jnp.dot(ssd_y, w_ref[D_ATT:D_ATT + D_SSD, :], preferred_element_type=F32)
        res = x_ref[rows, :] + gate_ref[...] * out
        ms2 = jnp.mean(res * res, axis=-1, keepdims=True)
        o_ref[rows, :] = res * lax.rsqrt(ms2 + EPS) * fg_ref[...]
        ssd_y = ssd_next


def _out_proj(x, att, y, z, ssd_norm_g, w_out, mod, mod_row, final_g, tm):
    B, L, _ = x.shape
    row = mod_row
    tok = lambda b, i: (b, i, 0)
    in_specs = [
        pl.BlockSpec((None, tm, D_MODEL), tok),
        pl.BlockSpec((None, tm, D_ATT), tok),
        pl.BlockSpec((None, tm, D_SSD), tok),
        pl.BlockSpec((None, tm, D_SSD), tok),
        pl.BlockSpec((1, D_SSD), lambda b, i: (0, 0)),
        pl.BlockSpec((D_ATT + D_SSD, D_MODEL), lambda b, i: (0, 0), pipeline_mode=pl.Buffered(1)),
        pl.BlockSpec((None, 1, D_MODEL), lambda b, i: (row(b), 0, 2)),
        pl.BlockSpec((1, D_MODEL), lambda b, i: (0, 0)),
    ]
    return pl.pallas_call(
        _out_kernel, grid=(B, L // tm), in_specs=in_specs,
        out_specs=pl.BlockSpec((None, tm, D_MODEL), tok),
        out_shape=jax.ShapeDtypeStruct((B, L, D_MODEL), F32),
        compiler_params=_cparams(("parallel", "parallel")),
        name="out_proj",
    )(x, att, y, z, ssd_norm_g, w_out, mod, final_g)


def _rope_tables(L):
    rows = L // GRID_W
    row_ids = jnp.repeat(jnp.arange(rows), GRID_W).astype(F32)
    col_ids = jnp.tile(jnp.arange(GRID_W), rows).astype(F32)
    inv = ROPE_BASE ** (-jnp.arange(ROPE_FREQS, dtype=F32) / ROPE_FREQS)
    ang_r = row_ids[:, None] * inv
    ang_c = col_ids[:, None] * inv
    cr, sr, cc, sc = jnp.cos(ang_r), jnp.sin(ang_r), jnp.cos(ang_c), jnp.sin(ang_c)
    cos64 = jnp.concatenate([cr, cr, cc, cc], axis=1)
    sin64 = jnp.concatenate([-sr, sr, -sc, sc], axis=1)
    return jnp.tile(cos64, (1, 2)), jnp.tile(sin64, (1, 2))


def _bc_perm():
    idx = list(range(D_SSD))
    for g in range(SSD_GROUPS):
        idx += [D_SSD + g * D_STATE + n for n in range(D_STATE)]
        idx += [D_SSD + SSD_GROUPS * D_STATE + g * D_STATE + n for n in range(D_STATE)]
    return jnp.asarray(idx, dtype=jnp.int32)


def _dt_perm():
    idx = []
    for g in range(SSD_GROUPS):
        for d in range(2):
            for hh in range(HEADS_PER_GROUP):
                idx.append(d * H_SSD + g * HEADS_PER_GROUP + hh)
    return jnp.asarray(idx, dtype=jnp.int32)


def _group_rows(v2h):
    t = v2h.reshape(2, SSD_GROUPS, HEADS_PER_GROUP).transpose(1, 0, 2).reshape(SSD_GROUPS, 2 * HEADS_PER_GROUP)
    return jnp.broadcast_to(t[:, :, None], (SSD_GROUPS, 2 * HEADS_PER_GROUP, LANES))


def _layer(x, mod, mod_row, params, rope_tabs, past_kv, h0, emit_f32, lam_init, final_g, tm, tq,
           heads_per_step, groups_per_step, tm_out):
    (norm_g, w_bf, w_dt, lams, subln_g, conv_w, conv_b, dtb_g, alog_g, dsk, ssd_norm_g,
     w_out) = params
    proj = _qkv_proj(x, mod, mod_row, norm_g, w_bf, rope_tabs, emit_f32, tm)
    qt, k, vt = proj[:3]
    g, z, xbc, dtt = _gzx_proj(x, mod, mod_row, norm_g, w_bf, w_dt, tm)
    kvs = [(k, vt)]
    if past_kv is not None:
        kvs.append(past_kv)
    att = _attention(qt, kvs, g, subln_g, lams, lam_init, tq, heads_per_step)
    y, hfin = _ssd(xbc, dtt, conv_w, conv_b, dtb_g, alog_g, dsk, h0, groups_per_step)
    out = _out_proj(x, att, y, z, ssd_norm_g, w_out, mod, mod_row, final_g, tm_out)
    return out, proj[3:], hfin


def kernel(x_prompt, x_sample, cache_k, cache_v, state_ssd, c, c_ctx, w_mod, b_mod, norm_g, w_in,
           lambda_q1, lambda_k1, lambda_q2, lambda_k2, subln_g, conv_w, conv_b, dt_bias, A_log,
           D_skip, ssd_norm_g, w_out, final_g):
    b_ctx, l_ctx, _ = x_prompt.shape
    b_dec, l_dec, _ = x_sample.shape
    l_past = cache_k.shape[2]
    depth = w_mod.shape[0]
    assert depth == 1
    lam_init = 0.8 - 0.6 * math.exp(-0.3 * 0)

    w_bf = w_in[0].astype(BF16)
    o2 = 4 * D_ATT + D_SSD
    bc_perm = _bc_perm()
    w_dt = jnp.pad(w_bf[:, o2 + CONV_DIM:][:, _dt_perm()], ((0, 0), (0, DT_PAD - 2 * H_SSD)))
    conv_w_p = conv_w[0][:, bc_perm]
    conv_b_p = conv_b[0][bc_perm].reshape(1, CONV_DIM)
    dtb_g = _group_rows(dt_bias[0])
    alog_g = _group_rows(A_log[0])
    dsk = jnp.broadcast_to(jnp.repeat(D_skip[0], SSD_HEADDIM)[:, None], (D_SSD, LANES))
    lams = [a[0].reshape(1, HD_QK) for a in (lambda_q1, lambda_k1, lambda_q2, lambda_k2)]
    params = (norm_g[0].reshape(1, D_MODEL), w_bf, w_dt, lams, subln_g[0].reshape(1, HD_V),
              conv_w_p, conv_b_p, dtb_g, alog_g, dsk, ssd_norm_g[0].reshape(1, D_SSD),
              w_out[0].astype(BF16))
    fg = final_g.reshape(1, D_MODEL)

    assert b_dec < SUBLANES
    cond = jnp.zeros((SUBLANES, D_MODEL), F32).at[:b_dec].set(c).at[b_dec].set(c_ctx)
    mod = _adaln(cond, w_mod[0], b_mod[0]).reshape(SUBLANES, 1, 3 * D_MODEL)

    y_prompt, (k_ctx, v_ctx), h_ctx = _layer(
        x_prompt, mod, lambda b: b_dec, params, None, None, None, True, lam_init, fg,
        tm=256, tq=256, heads_per_step=H_ATT, groups_per_step=SSD_GROUPS, tm_out=256)

    rope_tabs = _rope_tables(l_dec)
    k_past = cache_k[:, 0].reshape(b_dec, l_past, D_ATT).astype(BF16)
    vt_past = cache_v[:, 0].reshape(b_dec, l_past, D_ATT).transpose(0, 2, 1).astype(BF16)
    h0 = state_ssd[:, 0].reshape(b_dec, 2, SSD_GROUPS, GROUP_W, D_STATE)
    y_sample, _, _ = _layer(
        x_sample, mod, lambda b: b, params, rope_tabs, (k_past, vt_past), h0, False, lam_init, fg,
        tm=512, tq=512, heads_per_step=1, groups_per_step=1, tm_out=1024)

    new_cache_k = k_ctx.reshape(b_ctx, 1, l_ctx, H_ATT, 2 * HD_QK)
    new_cache_v = v_ctx.reshape(b_ctx, 1, l_ctx, H_ATT, HD_V)
    new_state = h_ctx.reshape(b_ctx, 1, 2, H_SSD, SSD_HEADDIM, D_STATE)
    return (y_prompt, y_sample, new_cache_k, new_cache_v, new_state)
```

```python
import functools
import math

import jax
import jax.numpy as jnp
from jax import lax
from jax.experimental import pallas as pl
from jax.experimental.pallas import tpu as pltpu

F32 = jnp.float32
BF16 = jnp.bfloat16

D_MODEL = 1024
GRID_W = 64
HD_QK = 64
HD_V = 128
H_ATT = 8
D_ATT = 1024
D_SSD = 1024
SSD_HEADDIM = 64
H_SSD = 16
SSD_GROUPS = 4
HEADS_PER_GROUP = 4
D_STATE = 64
CONV_DIM = D_SSD + 2 * SSD_GROUPS * D_STATE
CHUNK = 128
ROPE_BASE = 10000.0
ROPE_FREQS = 16
EPS = 1e-6

LANES = 128
SUBLANES = 8
DT_PAD = LANES
GROUP_W = HEADS_PER_GROUP * SSD_HEADDIM
VMEM_LIMIT = 56 * 1024 * 1024
ADALN_BLK = 1024
ATT_TK = 512
ATT_TQ = 256
SUB_ROWS = 256
LOG2E = 1.4426950408889634
CONV_UNROLL = 8
SCAN_UNROLL = 8


def _cparams(sem):
    return pltpu.CompilerParams(dimension_semantics=sem, vmem_limit_bytes=VMEM_LIMIT)


def _silu(x):
    hx = 0.5 * x
    return hx + hx * jnp.tanh(hx)


def _softplus(x):
    return jnp.maximum(x, 0.0) + jnp.log1p(jnp.exp(-jnp.abs(x)))


def _adaln_kernel(cond_ref, w_ref, b_ref, o_ref):
    cond = cond_ref[...]
    s = _silu(cond).astype(BF16)
    o_ref[...] = jnp.dot(s, w_ref[...].astype(BF16), preferred_element_type=F32) + b_ref[...]


def _adaln(cond, w_mod, b_mod):
    rows, n = cond.shape[0], w_mod.shape[1]
    return pl.pallas_call(
        _adaln_kernel,
        grid=(n // ADALN_BLK,),
        in_specs=[pl.BlockSpec((rows, D_MODEL), lambda j: (0, 0)),
                  pl.BlockSpec((D_MODEL, ADALN_BLK), lambda j: (0, j)),
                  pl.BlockSpec((1, ADALN_BLK), lambda j: (0, j))],
        out_specs=pl.BlockSpec((rows, ADALN_BLK), lambda j: (0, j)),
        out_shape=jax.ShapeDtypeStruct((rows, n), F32),
        compiler_params=_cparams(("parallel",)),
        name="adaln",
    )(cond, w_mod, b_mod.reshape(1, n))


def _mod_norm(x, shift, scale, g):
    ms = jnp.mean(x * x, axis=-1, keepdims=True)
    y = x * lax.rsqrt(ms + EPS) * g
    return y * (1.0 + scale) + shift


def _rope(t, cos, sin_signed, first):
    outs = []
    for h in range(H_ATT):
        th = t[:, h * LANES:(h + 1) * LANES]
        partner = jnp.where(first, pltpu.roll(th, LANES - ROPE_FREQS, 1), pltpu.roll(th, ROPE_FREQS, 1))
        outs.append(th * cos + partner * sin_signed)
    return jnp.concatenate(outs, axis=1)


def _qkv_kernel(*refs, rope, emit_f32):
    x_ref, shift_ref, scale_ref, ng_ref, w_ref = refs[:5]
    pos = 5
    if rope:
        cos_ref, sin_ref = refs[pos:pos + 2]
        pos += 2
    qt_ref, k_ref, vt_ref = refs[pos:pos + 3]
    pos += 3
    if emit_f32:
        k32_ref, v32_ref = refs[pos:pos + 2]

    hb = _mod_norm(x_ref[...], shift_ref[...], scale_ref[...], ng_ref[...]).astype(BF16)
    q = jnp.dot(hb, w_ref[:, 0:D_ATT], preferred_element_type=F32)
    k = jnp.dot(hb, w_ref[:, D_ATT:2 * D_ATT], preferred_element_type=F32)
    v = jnp.dot(hb, w_ref[:, 2 * D_ATT:3 * D_ATT], preferred_element_type=F32)
    if rope:
        cos = cos_ref[...]
        sin_signed = sin_ref[...]
        lane = lax.broadcasted_iota(jnp.int32, cos.shape, 1)
        first = (lane % (2 * ROPE_FREQS)) < ROPE_FREQS
        q = _rope(q, cos, sin_signed, first)
        k = _rope(k, cos, sin_signed, first)
    qt_ref[...] = (q * (LOG2E / math.sqrt(HD_QK))).T.astype(BF16)
    k_ref[...] = k.astype(BF16)
    vt_ref[...] = v.T.astype(BF16)
    if emit_f32:
        k32_ref[...] = k
        v32_ref[...] = v


def _qkv_proj(x, mod, mod_row, norm_g, w_bf, rope_tabs, emit_f32, tm):
    B, L, _ = x.shape
    grid = (B, L // tm)
    row = mod_row

    in_specs = [
        pl.BlockSpec((None, tm, D_MODEL), lambda b, i: (b, i, 0)),
        pl.BlockSpec((None, 1, D_MODEL), lambda b, i: (row(b), 0, 0)),
        pl.BlockSpec((None, 1, D_MODEL), lambda b, i: (row(b), 0, 1)),
        pl.BlockSpec((1, D_MODEL), lambda b, i: (0, 0)),
        pl.BlockSpec((D_MODEL, 3 * D_ATT), lambda b, i: (0, 0), pipeline_mode=pl.Buffered(1)),
    ]
    args = [x, mod, mod, norm_g, w_bf]
    if rope_tabs is not None:
        in_specs += [pl.BlockSpec((tm, LANES), lambda b, i: (i, 0)),
                     pl.BlockSpec((tm, LANES), lambda b, i: (i, 0))]
        args += list(rope_tabs)
    out_specs = [
        pl.BlockSpec((None, D_ATT, tm), lambda b, i: (b, 0, i)),
        pl.BlockSpec((None, tm, D_ATT), lambda b, i: (b, i, 0)),
        pl.BlockSpec((None, D_ATT, tm), lambda b, i: (b, 0, i)),
    ]
    out_shape = [
        jax.ShapeDtypeStruct((B, D_ATT, L), BF16),
        jax.ShapeDtypeStruct((B, L, D_ATT), BF16),
        jax.ShapeDtypeStruct((B, D_ATT, L), BF16),
    ]
    if emit_f32:
        out_specs += [pl.BlockSpec((None, tm, D_ATT), lambda b, i: (b, i, 0))] * 2
        out_shape += [jax.ShapeDtypeStruct((B, L, D_ATT), F32)] * 2
    return pl.pallas_call(
        functools.partial(_qkv_kernel, rope=rope_tabs is not None, emit_f32=emit_f32),
        grid=grid, in_specs=in_specs, out_specs=out_specs, out_shape=out_shape,
        compiler_params=_cparams(("parallel", "parallel")),
        name="qkv_proj",
    )(*args)


def _gzx_kernel(x_ref, shift_ref, scale_ref, ng_ref, wg_ref, wz_ref, wx_ref, wbc_ref, wdt_ref,
                g_ref, z_ref, xbc_ref, dtt_ref):
    n_sub = x_ref.shape[0] // SUB_ROWS

    def prologue(r):
        rows = slice(r * SUB_ROWS, (r + 1) * SUB_ROWS)
        return _mod_norm(x_ref[rows, :], shift_ref[...], scale_ref[...], ng_ref[...]).astype(BF16)

    hb = prologue(0)
    for r in range(n_sub):
        rows = slice(r * SUB_ROWS, (r + 1) * SUB_ROWS)
        hb_next = prologue(r + 1) if r + 1 < n_sub else None
        g_ref[rows, :] = jnp.dot(hb, wg_ref[...], preferred_element_type=F32)
        z_ref[rows, :] = jnp.dot(hb, wz_ref[...], preferred_element_type=F32)
        xbc_ref[rows, 0:D_SSD] = jnp.dot(hb, wx_ref[...], preferred_element_type=F32)
        bc = jnp.dot(hb, wbc_ref[...], preferred_element_type=F32)
        nb = SSD_GROUPS * D_STATE
        pieces = []
        for g in range(SSD_GROUPS):
            pieces += [bc[:, g * D_STATE:(g + 1) * D_STATE], bc[:, nb + g * D_STATE:nb + (g + 1) * D_STATE]]
        xbc_ref[rows, D_SSD:CONV_DIM] = jnp.concatenate(pieces, axis=1)
        dt = jnp.dot(hb, wdt_ref[...], preferred_element_type=F32)
        dtt_ref[:, rows] = dt.T[:2 * H_SSD, :]
        hb = hb_next


def _gzx_proj(x, mod, mod_row, norm_g, w_bf, w_dt, tm):
    bc_w = 2 * SSD_GROUPS * D_STATE
    bc_blk = (4 * D_ATT + 2 * D_SSD) // bc_w
    B, L, _ = x.shape
    row = mod_row
    once = pl.Buffered(1)
    in_specs = [
        pl.BlockSpec((None, tm, D_MODEL), lambda b, i: (b, i, 0)),
        pl.BlockSpec((None, 1, D_MODEL), lambda b, i: (row(b), 0, 0)),
        pl.BlockSpec((None, 1, D_MODEL), lambda b, i: (row(b), 0, 1)),
        pl.BlockSpec((1, D_MODEL), lambda b, i: (0, 0)),
        pl.BlockSpec((D_MODEL, D_ATT), lambda b, i: (0, 3), pipeline_mode=once),
        pl.BlockSpec((D_MODEL, D_SSD), lambda b, i: (0, 4), pipeline_mode=once),
        pl.BlockSpec((D_MODEL, D_SSD), lambda b, i: (0, 5), pipeline_mode=once),
        pl.BlockSpec((D_MODEL, bc_w), lambda b, i: (0, bc_blk), pipeline_mode=once),
        pl.BlockSpec((D_MODEL, DT_PAD), lambda b, i: (0, 0), pipeline_mode=once),
    ]
    widths = (D_ATT, D_SSD, CONV_DIM)
    out_specs = [pl.BlockSpec((None, tm, w), lambda b, i: (b, i, 0)) for w in widths]
    out_shape = [jax.ShapeDtypeStruct((B, L, w), F32) for w in widths]
    out_specs.append(pl.BlockSpec((None, 2 * H_SSD, tm), lambda b, i: (b, 0, i)))
    out_shape.append(jax.ShapeDtypeStruct((B, 2 * H_SSD, L), F32))
    return pl.pallas_call(
        _gzx_kernel, grid=(B, L // tm), in_specs=in_specs, out_specs=out_specs, out_shape=out_shape,
        compiler_params=_cparams(("parallel", "parallel")),
        name="gzx_proj",
    )(x, mod, mod, norm_g, w_bf, w_bf, w_bf, w_bf, w_dt)


def _attn_kernel(*refs, n_src, lam_init, carry):
    qt_ref = refs[0]
    pos = 1
    if carry:
        qt_next_ref = refs[pos]
        k_next_refs = refs[pos + 1:pos + 1 + n_src]
        pos += 1 + n_src
    srcs = [(refs[pos + 2 * i], refs[pos + 1 + 2 * i]) for i in range(n_src)]
    pos += 2 * n_src
    g_ref, sg_ref, lq1_ref, lk1_ref, lq2_ref, lk2_ref, o_ref = refs[pos:pos + 7]
    if carry:
        s_sc, m_sc = refs[pos + 7:pos + 9]

    tq = ATT_TQ
    n_heads = qt_ref.shape[0] // HD_V
    chains = [(hh, t) for hh in range(n_heads) for t in range(qt_ref.shape[1] // tq)]
    n_ch = len(chains)
    row = lax.broadcasted_iota(jnp.int32, (HD_V, tq), 0)
    zero = jnp.zeros((HD_V, tq), BF16)

    def stacked_q(ref, hh, t):
        qt = ref[hh * HD_V:(hh + 1) * HD_V, t * tq:(t + 1) * tq]
        return jnp.concatenate([jnp.where(row < HD_QK, qt, zero), jnp.where(row >= HD_QK, qt, zero)], axis=1)

    q2t = [stacked_q(qt_ref, hh, t) for hh, t in chains]

    blocks = []
    for si, (k_ref, vt_ref) in enumerate(srcs):
        n_keys = k_ref.shape[0]
        tk = min(ATT_TK, n_keys)
        blocks += [(k_ref, vt_ref, j * tk, tk, si) for j in range(n_keys // tk)]

    lam = (jnp.exp(jnp.sum(lq1_ref[...] * lk1_ref[...], axis=1, keepdims=True))
           - jnp.exp(jnp.sum(lq2_ref[...] * lk2_ref[...], axis=1, keepdims=True)) + lam_init)

    def score_piece(c, j, handover=None):
        k_ref, _, off, tk, si = blocks[j]
        hh = chains[c][0]
        if handover is not None:
            k_ref = k_next_refs[si]
        s = jnp.dot(k_ref[off:off + tk, hh * HD_V:(hh + 1) * HD_V],
                    q2t[c] if handover is None else handover, preferred_element_type=F32)
        return s, jnp.max(s, axis=0, keepdims=True)

    def prob_piece(c, j, s, m, l, acc):
        _, vt_ref, off, tk, _ = blocks[j]
        hh = chains[c][0]
        p = jnp.exp2(s - m)
        l = l + jnp.sum(p, axis=0, keepdims=True)
        acc = acc + jnp.dot(vt_ref[hh * HD_V:(hh + 1) * HD_V, off:off + tk], p.astype(BF16),
                            preferred_element_type=F32)
        return l, acc

    def epilogue(c, l, acc):
        hh, t = chains[c]
        o_n = acc / l
        o = (o_n[:, :tq] - lam * o_n[:, tq:]).T
        ms = jnp.mean(o * o, axis=-1, keepdims=True)
        att = o * lax.rsqrt(ms + EPS) * sg_ref[...] * (1.0 - lam_init)
        gate = _silu(g_ref[t * tq:(t + 1) * tq, hh * HD_V:(hh + 1) * HD_V])
        o_ref[t * tq:(t + 1) * tq, hh * HD_V:(hh + 1) * HD_V] = (att * gate).astype(BF16)

    n_pc = len(blocks)
    neg_inf = jnp.full((1, 2 * tq), -jnp.inf, F32)
    if not carry:
        prev = None
        for c in range(n_ch + 1):
            cur_s, cur_m = [], neg_inf
            if prev is not None:
                l = jnp.zeros((1, 2 * tq), F32)
                acc = jnp.zeros((HD_V, 2 * tq), F32)
            for j in range(n_pc):
                if c < n_ch:
                    s, s_max = score_piece(c, j)
                    cur_s.append(s)
                    cur_m = jnp.maximum(cur_m, s_max)
                if prev is not None:
                    l, acc = prob_piece(prev[0], j, prev[1][j], prev[2], l, acc)
            if prev is not None:
                epilogue(prev[0], l, acc)
            prev = (c, cur_s, cur_m) if c < n_ch else None
        return

    dst = [sum(b[3] for b in blocks[:j]) for j in range(n_pc)]
    step = ((pl.program_id(0) * pl.num_programs(1) + pl.program_id(1)) * pl.num_programs(2)
            + pl.program_id(2))
    cur, nxt = step % 2, (step + 1) % 2

    @pl.when(step == 0)
    def _():
        m0 = neg_inf
        for j in range(n_pc):
            s, s_max = score_piece(0, j)
            s_sc[0, dst[j]:dst[j] + blocks[j][3], :] = s
            m0 = jnp.maximum(m0, s_max)
        m_sc[0] = jnp.broadcast_to(m0, m_sc.shape[1:])

    hh0, t0 = chains[0]
    q_next = stacked_q(qt_next_ref, hh0, t0)
    prev_s, prev_m = None, m_sc[cur][0:1, :]
    for u in range(n_ch):
        l = jnp.zeros((1, 2 * tq), F32)
        acc = jnp.zeros((HD_V, 2 * tq), F32)
        nxt_s, nxt_m = [], neg_inf
        for j in range(n_pc):
            rows = slice(dst[j], dst[j] + blocks[j][3])
            if u + 1 < n_ch:
                s, s_max = score_piece(u + 1, j)
                nxt_s.append(s)
            else:
                s, s_max = score_piece(0, j, q_next)
                s_sc[nxt, rows, :] = s
            nxt_m = jnp.maximum(nxt_m, s_max)
            s_u = s_sc[cur, rows, :] if u == 0 else prev_s[j]
            l, acc = prob_piece(u, j, s_u, prev_m, l, acc)
        if u + 1 == n_ch:
            m_sc[nxt] = jnp.broadcast_to(nxt_m, m_sc.shape[1:])
        epilogue(u, l, acc)
        prev_s, prev_m = nxt_s, nxt_m


def _attention(qt, kvs, g, subln_g, lams, lam_init, tq, heads_per_step):
    B, _, L = qt.shape
    hw = heads_per_step * HD_V
    nq = L // tq
    nh = H_ATT // heads_per_step
    carry = nq > 1
    in_specs = [pl.BlockSpec((None, hw, tq), lambda b, h, i: (b, h, i))]
    args = [qt]
    scratch = []
    if carry:
        def succ(b, h, i):
            end_i, end_h = i == nq - 1, h == nh - 1
            last = end_i & end_h & (b == B - 1)
            b1 = jnp.where(end_i & end_h & ~last, b + 1, b)
            h1 = jnp.where(last, h, jnp.where(end_i, jnp.where(end_h, 0, h + 1), h))
            i1 = jnp.where(last, i, jnp.where(end_i, 0, i + 1))
            return b1, h1, i1

        def q_next(b, h, i):
            b1, h1, i1 = succ(b, h, i)
            return b1, h1, i1

        def k_next(b, h, i):
            b1, h1, _ = succ(b, h, i)
            return b1, 0, h1

        in_specs.append(pl.BlockSpec((None, hw, tq), q_next))
        args.append(qt)
        for k, _ in kvs:
            in_specs.append(pl.BlockSpec((None, k.shape[1], hw), k_next))
            args.append(k)
        n_all = sum(k.shape[1] for k, _ in kvs)
        scratch = [pltpu.VMEM((2, n_all, 2 * ATT_TQ), F32), pltpu.VMEM((2, SUBLANES, 2 * ATT_TQ), F32)]
    for k, vt in kvs:
        n_keys = k.shape[1]
        in_specs += [pl.BlockSpec((None, n_keys, hw), lambda b, h, i: (b, 0, h)),
                     pl.BlockSpec((None, hw, n_keys), lambda b, h, i: (b, h, 0))]
        args += [k, vt]
    in_specs += [pl.BlockSpec((None, tq, hw), lambda b, h, i: (b, i, h)),
                 pl.BlockSpec((1, HD_V), lambda b, h, i: (0, 0))]
    in_specs += [pl.BlockSpec((1, HD_QK), lambda b, h, i: (0, 0))] * 4
    args += [g, subln_g] + list(lams)
    return pl.pallas_call(
        functools.partial(_attn_kernel, n_src=len(kvs), lam_init=lam_init, carry=carry),
        grid=(B, nh, nq),
        in_specs=in_specs,
        out_specs=pl.BlockSpec((None, tq, hw), lambda b, h, i: (b, i, h)),
        out_shape=jax.ShapeDtypeStruct((B, L, D_ATT), BF16),
        scratch_shapes=scratch,
        compiler_params=_cparams(("arbitrary",) * 3 if carry else ("parallel", "parallel", "arbitrary")),
        name="diff_attention",
    )(*args)


def _ssd_kernel(*refs, seq_len, has_h0):
    (x_ref, bc_ref, dtt_ref, cwx_ref, cwbc_ref, cbx_ref, cbbc_ref,
     dtb_ref, alog_ref, dsk_ref) = refs[:10]
    pos = 10
    if has_h0:
        h0_ref = refs[pos]
        pos += 1
    y_ref, hfin_ref, xt_sc, yt_sc, b_sc, ct_sc = refs[pos:pos + 6]

    nc = seq_len // CHUNK
    n_grp = x_ref.shape[1] // GROUP_W
    row_i = lax.broadcasted_iota(jnp.int32, (CHUNK, CHUNK), 0)
    col_i = lax.broadcasted_iota(jnp.int32, (CHUNK, CHUNK), 1)
    row1 = lax.broadcasted_iota(jnp.int32, (CHUNK, 1), 0)

    def conv_silu(src, w_ref, b_ref, c):
        r0 = pl.multiple_of(c * CHUNK, CHUNK)
        rp = pl.multiple_of(jnp.maximum(r0 - SUBLANES, 0), SUBLANES)
        rn = pl.multiple_of(jnp.minimum(r0 + CHUNK, seq_len - SUBLANES), SUBLANES)
        u = src[pl.ds(r0, CHUNK), :]
        pr = jnp.where(c > 0, src[pl.ds(rp, SUBLANES), :][SUBLANES - 1:SUBLANES, :], 0.0)
        nx = jnp.where(c < nc - 1, src[pl.ds(rn, SUBLANES), :][0:1, :], 0.0)
        prev = jnp.where(row1 == 0, pr, pltpu.roll(u, 1, 0))
        nxt = jnp.where(row1 == CHUNK - 1, nx, pltpu.roll(u, CHUNK - 1, 0))
        w = 0.5 * w_ref[...]
        hx = 0.5 * b_ref[...] + prev * w[0:1, :] + u * w[1:2, :] + nxt * w[2:3, :]
        return hx + hx * jnp.tanh(hx)

    def conv_chunk(c, carry):
        xt = conv_silu(x_ref, cwx_ref, cbx_ref, c).T
        xt_sc[c] = xt
        yt_sc[c] = dsk_ref[...] * xt
        bc = conv_silu(bc_ref, cwbc_ref, cbbc_ref, c)
        for g in range(n_grp):
            bc_g = bc[:, g * LANES:(g + 1) * LANES]
            b_sc[c, g] = bc_g[:, :D_STATE].astype(BF16)
            ct_sc[c, g] = bc_g.T[D_STATE:, :].astype(BF16)
        return carry

    lax.fori_loop(0, nc, conv_chunk, 0, unroll=math.gcd(CONV_UNROLL, nc))

    dir_heads = 2 * HEADS_PER_GROUP
    dtb = [dtb_ref[g] for g in range(n_grp)]
    a_neg = [-jnp.exp(alog_ref[g]) * LOG2E for g in range(n_grp)]

    masks = (row_i <= col_i, row_i >= col_i)
    tri = tuple(jnp.where(m, 1.0, 0.0).astype(BF16) for m in masks)
    ends = (CHUNK - 1, 0)

    def scan_prep(items):
        out = []
        for d in range(2):
            sp_d = []
            for g, dd, c in items:
                if dd == d:
                    t0 = pl.multiple_of(c * CHUNK, CHUNK)
                    dt_g = dtt_ref[g * dir_heads:(g + 1) * dir_heads, pl.ds(t0, CHUNK)]
                    sp_d.append(_softplus(dt_g + dtb[g]))
            x = jnp.concatenate([s * a_neg[it[0]] for s, it in
                                 zip(sp_d, [it for it in items if it[1] == d])], axis=0)
            hi = x.astype(BF16)
            r1 = x - hi.astype(F32)
            mid = r1.astype(BF16)
            lo = (r1 - mid.astype(F32)).astype(BF16)
            parts = jnp.dot(jnp.concatenate([hi, mid, lo], axis=0), tri[d], preferred_element_type=F32)
            nr = x.shape[0]
            cum_d = parts[:nr] + parts[nr:2 * nr] + parts[2 * nr:]
            out.append((jnp.concatenate(sp_d, axis=0), cum_d, cum_d.T))
        return tuple(out)

    def scan_items(items, prep, states):
        n = len(items)
        sp, cum, cum_col, cb_t = [None] * n, [None] * n, [None] * n, [None] * n
        for d in range(2):
            sp_d, cum_d, cum_dt = prep[d]
            for i, k in enumerate([k for k, it in enumerate(items) if it[1] == d]):
                rows = slice(i * dir_heads, (i + 1) * dir_heads)
                sp[k], cum[k], cum_col[k] = sp_d[rows], cum_d[rows], cum_dt[:, rows]
        for k, (g, d, c) in enumerate(items):
            cb_t[k] = jnp.dot(b_sc[c, g], ct_sc[c, g], preferred_element_type=F32)
        xdt_b, lm_b, x_dec, y_scale, s_scale = [], [], [], [], []
        for k, (g, d, c) in enumerate(items):
            xt = xt_sc[c, g * GROUP_W:(g + 1) * GROUP_W, :]
            xdt_k, lm_k, x_dec_k, y_scale_k, s_scale_k = [], [], [], [], []
            for hh in range(HEADS_PER_GROUP):
                r = d * HEADS_PER_GROUP + hh
                a_row = cum[k][r:r + 1, :]
                seg = a_row - cum_col[k][:, r:r + 1]
                lmat = jnp.exp2(jnp.where(masks[d], seg, -jnp.inf))
                xdt = xt[hh * SSD_HEADDIM:(hh + 1) * SSD_HEADDIM, :] * sp[k][r:r + 1, :]
                a_end = a_row[:, ends[d]:ends[d] + 1]
                xdt_k.append(xdt.astype(BF16))
                lm_k.append((cb_t[k] * lmat).astype(BF16))
                x_dec_k.append((xdt * jnp.exp2(a_end - a_row)).astype(BF16))
                y_scale_k.append(jnp.broadcast_to(jnp.exp2(a_row), (SSD_HEADDIM, CHUNK)))
                s_scale_k.append(jnp.broadcast_to(jnp.exp2(a_end), (SSD_HEADDIM, D_STATE)))
            xdt_b.append(xdt_k)
            lm_b.append(lm_k)
            x_dec.append(jnp.concatenate(x_dec_k, axis=0))
            y_scale.append(jnp.concatenate(y_scale_k, axis=0))
            s_scale.append(jnp.concatenate(s_scale_k, axis=0))
        y_diag, st = [], []
        for k, (g, d, c) in enumerate(items):
            y_diag.append(jnp.concatenate(
                [jnp.dot(xdt_b[k][hh], lm_b[k][hh], preferred_element_type=F32)
                 for hh in range(HEADS_PER_GROUP)], axis=0))
            st.append(jnp.dot(x_dec[k], b_sc[c, g], preferred_element_type=F32))
        states = list(states)
        for k, (g, d, c) in enumerate(items):
            si = 2 * g + d
            rows = slice(g * GROUP_W, (g + 1) * GROUP_W)
            y_off = jnp.dot(states[si].astype(BF16), ct_sc[c, g], preferred_element_type=F32)
            yt_sc[c, rows, :] = yt_sc[c, rows, :] + y_diag[k] + y_off * y_scale[k]
            states[si] = states[si] * s_scale[k] + st[k]
        return tuple(states)

    if has_h0:
        s_init = tuple(h0_ref[d, g] for g in range(n_grp) for d in range(2))
    else:
        s_init = (jnp.zeros((GROUP_W, D_STATE), F32),) * (2 * n_grp)

    unroll = math.gcd(max(SCAN_UNROLL // n_grp, 1), nc)
    n_it = nc // unroll

    def items_of(i):
        items = []
        for u in range(unroll):
            for g in range(n_grp):
                items += [(g, 0, i * unroll + u), (g, 1, nc - 1 - (i * unroll + u))]
        return items

    def scan_step(i, carry):
        states, prep = carry
        prep_next = scan_prep(items_of(jnp.minimum(i + 1, n_it - 1)))
        return scan_items(items_of(i), prep, states), prep_next

    s_fin, _ = lax.fori_loop(0, n_it, scan_step, (s_init, scan_prep(items_of(0))))
    for g in range(n_grp):
        for d in range(2):
            hfin_ref[d, g] = s_fin[2 * g + d]

    def emit_chunk(c, carry):
        y_ref[pl.ds(pl.multiple_of(c * CHUNK, CHUNK), CHUNK), :] = yt_sc[c].T
        return carry

    lax.fori_loop(0, nc, emit_chunk, 0, unroll=math.gcd(CONV_UNROLL, nc))


def _ssd(xbc, dtt, conv_w, conv_b, dtb_g, alog_g, dsk, h0, n_grp, dtt_merge):
    B, L, _ = xbc.shape
    nc = L // CHUNK
    xw, bw = n_grp * GROUP_W, n_grp * LANES
    bc_blk = D_SSD // bw
    dir_heads = 2 * HEADS_PER_GROUP
    state_spec = pl.BlockSpec((None, 2, n_grp, GROUP_W, D_STATE), lambda b, g: (b, 0, g, 0, 0))
    in_specs = [
        pl.BlockSpec((None, L, xw), lambda b, g: (b, 0, g)),
        pl.BlockSpec((None, L, bw), lambda b, g: (b, 0, bc_blk + g)),
        pl.BlockSpec((None, n_grp * dir_heads, L), lambda b, g: (b // dtt_merge, g, b % dtt_merge)),
        pl.BlockSpec((3, xw), lambda b, g: (0, g)),
        pl.BlockSpec((3, bw), lambda b, g: (0, bc_blk + g)),
        pl.BlockSpec((1, xw), lambda b, g: (0, g)),
        pl.BlockSpec((1, bw), lambda b, g: (0, bc_blk + g)),
        pl.BlockSpec((n_grp, dir_heads, LANES), lambda b, g: (g, 0, 0)),
        pl.BlockSpec((n_grp, dir_heads, LANES), lambda b, g: (g, 0, 0)),
        pl.BlockSpec((xw, LANES), lambda b, g: (g, 0)),
    ]
    args = [xbc, xbc, dtt, conv_w, conv_w, conv_b, conv_b, dtb_g, alog_g, dsk]
    if h0 is not None:
        in_specs.append(state_spec)
        args.append(h0)
    return pl.pallas_call(
        functools.partial(_ssd_kernel, seq_len=L, has_h0=h0 is not None),
        grid=(B, SSD_GROUPS // n_grp),
        in_specs=in_specs,
        out_specs=[pl.BlockSpec((None, L, xw), lambda b, g: (b, 0, g)), state_spec],
        out_shape=[jax.ShapeDtypeStruct((B, L, D_SSD), F32),
                   jax.ShapeDtypeStruct((B, 2, SSD_GROUPS, GROUP_W, D_STATE), F32)],
        scratch_shapes=[pltpu.VMEM((nc, xw, CHUNK), F32), pltpu.VMEM((nc, xw, CHUNK), F32),
                        pltpu.VMEM((nc, n_grp, CHUNK, D_STATE), BF16),
                        pltpu.VMEM((nc, n_grp, D_STATE, CHUNK), BF16)],
        compiler_params=_cparams(("parallel", "parallel")),
        name="ssd_scan",
    )(*args)


def _out_kernel(x_ref, att_ref, y_ref, z_ref, sng_ref, w_ref, gate_ref, fg_ref, o_ref):
    n_sub = x_ref.shape[0] // SUB_ROWS

    def prologue(r):
        rows = slice(r * SUB_ROWS, (r + 1) * SUB_ROWS)
        yz = y_ref[rows, :] * _silu(z_ref[rows, :])
        ms = jnp.mean(yz * yz, axis=-1, keepdims=True)
        return (yz * lax.rsqrt(ms + EPS) * sng_ref[...]).astype(BF16)

    ssd_y = prologue(0)
    for r in range(n_sub):
        rows = slice(r * SUB_ROWS, (r + 1) * SUB_ROWS)
        ssd_next = prologue(r + 1) if r + 1 < n_sub else None
        out = jnp.dot(att_ref[rows, :], w_ref[0:D_ATT, :], preferred_element_type=F32)
        out = out + jnp.dot(ssd_y, w_ref[D_ATT:D_ATT + D_SSD, :], preferred_element_type=F32)
        res = x_ref[rows, :] + gate_ref[...] * out
        ms2 = jnp.mean(res * res, axis=-1, keepdims=True)
        o_ref[rows, :] = res * lax.rsqrt(ms2 + EPS) * fg_ref[...]
        ssd_y = ssd_next


def _out_proj(x, att, y, z, ssd_norm_g, w_out, mod, mod_row, final_g, tm):
    B, L, _ = x.shape
    row = mod_row
    tok = lambda b, i: (b, i, 0)
    in_specs = [
        pl.BlockSpec((None, tm, D_MODEL), tok),
        pl.BlockSpec((None, tm, D_ATT), tok),
        pl.BlockSpec((None, tm, D_SSD), tok),
        pl.BlockSpec((None, tm, D_SSD), tok),
        pl.BlockSpec((1, D_SSD), lambda b, i: (0, 0)),
        pl.BlockSpec((D_ATT + D_SSD, D_MODEL), lambda b, i: (0, 0), pipeline_mode=pl.Buffered(1)),
        pl.BlockSpec((None, 1, D_MODEL), lambda b, i: (row(b), 0, 2)),
        pl.BlockSpec((1, D_MODEL), lambda b, i: (0, 0)),
    ]
    return pl.pallas_call(
        _out_kernel, grid=(B, L // tm), in_specs=in_specs,
        out_specs=pl.BlockSpec((None, tm, D_MODEL), tok),
        out_shape=jax.ShapeDtypeStruct((B, L, D_MODEL), F32),
        compiler_params=_cparams(("parallel", "parallel")),
        name="out_proj",
    )(x, att, y, z, ssd_norm_g, w_out, mod, final_g)


def _rope_tables(L):
    rows = L // GRID_W
    row_ids = jnp.repeat(jnp.arange(rows), GRID_W).astype(F32)
    col_ids = jnp.tile(jnp.arange(GRID_W), rows).astype(F32)
    inv = ROPE_BASE ** (-jnp.arange(ROPE_FREQS, dtype=F32) / ROPE_FREQS)
    ang_r = row_ids[:, None] * inv
    ang_c = col_ids[:, None] * inv
    cr, sr, cc, sc = jnp.cos(ang_r), jnp.sin(ang_r), jnp.cos(ang_c), jnp.sin(ang_c)
    cos64 = jnp.concatenate([cr, cr, cc, cc], axis=1)
    sin64 = jnp.concatenate([-sr, sr, -sc, sc], axis=1)
    return jnp.tile(cos64, (1, 2)), jnp.tile(sin64, (1, 2))


def _bc_perm():
    idx = list(range(D_SSD))
    for g in range(SSD_GROUPS):
        idx += [D_SSD + g * D_STATE + n for n in range(D_STATE)]
        idx += [D_SSD + SSD_GROUPS * D_STATE + g * D_STATE + n for n in range(D_STATE)]
    return jnp.asarray(idx, dtype=jnp.int32)


def _dt_perm():
    idx = []
    for g in range(SSD_GROUPS):
        for d in range(2):
            for hh in range(HEADS_PER_GROUP):
                idx.append(d * H_SSD + g * HEADS_PER_GROUP + hh)
    return jnp.asarray(idx, dtype=jnp.int32)


def _group_rows(v2h):
    t = v2h.reshape(2, SSD_GROUPS, HEADS_PER_GROUP).transpose(1, 0, 2).reshape(SSD_GROUPS, 2 * HEADS_PER_GROUP)
    return jnp.broadcast_to(t[:, :, None], (SSD_GROUPS, 2 * HEADS_PER_GROUP, LANES))


def _layer(x, mod, mod_row, params, rope_tabs, past_kv, h0, emit_f32, lam_init, final_g, tm, tq,
           heads_per_step, groups_per_step, tm_out, row_merge):
    (norm_g, w_bf, w_dt, lams, subln_g, conv_w, conv_b, dtb_g, alog_g, dsk, ssd_norm_g,
     w_out) = params
    B, L, _ = x.shape
    merged = lambda a: a.reshape(B // row_merge, L * row_merge, a.shape[-1])
    split = lambda a: a.reshape(B, L, a.shape[-1])
    proj = _qkv_proj(x, mod, mod_row, norm_g, w_bf, rope_tabs, emit_f32, tm)
    qt, k, vt = proj[:3]
    g, z, xbc, dtt = _gzx_proj(merged(x), mod, mod_row, norm_g, w_bf, w_dt, tm * row_merge)
    g, z, xbc = split(g), split(z), split(xbc)
    kvs = [(k, vt)]
    if past_kv is not None:
        kvs.append(past_kv)
    att = _attention(qt, kvs, g, subln_g, lams, lam_init, tq, heads_per_step)
    y, hfin = _ssd(xbc, dtt, conv_w, conv_b, dtb_g, alog_g, dsk, h0, groups_per_step, row_merge)
    out = _out_proj(merged(x), merged(att), merged(y), merged(z), ssd_norm_g, w_out, mod, mod_row,
                    final_g, tm_out)
    return split(out), proj[3:], hfin


def kernel(x_prompt, x_sample, cache_k, cache_v, state_ssd, c, c_ctx, w_mod, b_mod, norm_g, w_in,
           lambda_q1, lambda_k1, lambda_q2, lambda_k2, subln_g, conv_w, conv_b, dt_bias, A_log,
           D_skip, ssd_norm_g, w_out, final_g):
    b_ctx, l_ctx, _ = x_prompt.shape
    b_dec, l_dec, _ = x_sample.shape
    l_past = cache_k.shape[2]
    depth = w_mod.shape[0]
    assert depth == 1
    lam_init = 0.8 - 0.6 * math.exp(-0.3 * 0)

    w_bf = w_in[0].astype(BF16)
    o2 = 4 * D_ATT + D_SSD
    bc_perm = _bc_perm()
    w_dt = jnp.pad(w_bf[:, o2 + CONV_DIM:][:, _dt_perm()], ((0, 0), (0, DT_PAD - 2 * H_SSD)))
    conv_w_p = conv_w[0][:, bc_perm]
    conv_b_p = conv_b[0][bc_perm].reshape(1, CONV_DIM)
    dtb_g = _group_rows(dt_bias[0])
    alog_g = _group_rows(A_log[0])
    dsk = jnp.broadcast_to(jnp.repeat(D_skip[0], SSD_HEADDIM)[:, None], (D_SSD, LANES))
    lams = [a[0].reshape(1, HD_QK) for a in (lambda_q1, lambda_k1, lambda_q2, lambda_k2)]
    params = (norm_g[0].reshape(1, D_MODEL), w_bf, w_dt, lams, subln_g[0].reshape(1, HD_V),
              conv_w_p, conv_b_p, dtb_g, alog_g, dsk, ssd_norm_g[0].reshape(1, D_SSD),
              w_out[0].astype(BF16))
    fg = final_g.reshape(1, D_MODEL)

    assert b_dec < SUBLANES
    cond = jnp.zeros((SUBLANES, D_MODEL), F32).at[:b_dec].set(c).at[b_dec].set(c_ctx)
    mod = _adaln(cond, w_mod[0], b_mod[0]).reshape(SUBLANES, 1, 3 * D_MODEL)

    y_prompt, (k_ctx, v_ctx), h_ctx = _layer(
        x_prompt, mod, lambda b: b_dec, params, None, None, None, True, lam_init, fg,
        tm=256, tq=256, heads_per_step=H_ATT, groups_per_step=SSD_GROUPS, tm_out=512, row_merge=2)

    rope_tabs = _rope_tables(l_dec)
    k_past = cache_k[:, 0].reshape(b_dec, l_past, D_ATT).astype(BF16)
    vt_past = cache_v[:, 0].reshape(b_dec, l_past, D_ATT).transpose(0, 2, 1).astype(BF16)
    h0 = state_ssd[:, 0].reshape(b_dec, 2, SSD_GROUPS, GROUP_W, D_STATE)
    y_sample, _, _ = _layer(
        x_sample, mod, lambda b: b, params, rope_tabs, (k_past, vt_past), h0, False, lam_init, fg,
        tm=512, tq=512, heads_per_step=1, groups_per_step=1, tm_out=1024, row_merge=1)

    new_cache_k = k_ctx.reshape(b_ctx, 1, l_ctx, H_ATT, 2 * HD_QK)
    new_cache_v = v_ctx.reshape(b_ctx, 1, l_ctx, H_ATT, HD_V)
    new_state = h_ctx.reshape(b_ctx, 1, 2, H_SSD, SSD_HEADDIM, D_STATE)
    return (y_prompt, y_sample, new_cache_k, new_cache_v, new_state)
```

```python
import functools
import math

import jax
import jax.numpy as jnp
import numpy as np
from jax import lax
from jax.experimental import pallas as pl
from jax.experimental.pallas import tpu as pltpu

F32 = jnp.float32
BF16 = jnp.bfloat16

D_MODEL = 1024
GRID_W = 64
HD_QK = 64
HD_V = 128
H_ATT = 8
D_ATT = 1024
D_SSD = 1024
SSD_HEADDIM = 64
H_SSD = 16
SSD_GROUPS = 4
HEADS_PER_GROUP = 4
D_STATE = 64
CONV_DIM = D_SSD + 2 * SSD_GROUPS * D_STATE
CHUNK = 128
ROPE_BASE = 10000.0
ROPE_FREQS = 16
EPS = 1e-6

LANES = 128
SUBLANES = 8
DT_PAD = LANES
GROUP_W = HEADS_PER_GROUP * SSD_HEADDIM
VMEM_LIMIT = 56 * 1024 * 1024
ADALN_BLK = 1024
ATT_TK = 512
ATT_TQ = 256
SUB_ROWS = 256
LOG2E = 1.4426950408889634
CONV_UNROLL = 8
SCAN_UNROLL = 8


def _cparams(sem):
    return pltpu.CompilerParams(dimension_semantics=sem, vmem_limit_bytes=VMEM_LIMIT)


def _silu(x):
    hx = 0.5 * x
    return hx + hx * jnp.tanh(hx)


def _softplus(x):
    return jnp.maximum(x, 0.0) + jnp.log1p(jnp.exp(-jnp.abs(x)))


def _adaln_kernel(cond_ref, w_ref, b_ref, o_ref):
    cond = cond_ref[...]
    s = _silu(cond).astype(BF16)
    o_ref[...] = jnp.dot(s, w_ref[...].astype(BF16), preferred_element_type=F32) + b_ref[...]


def _adaln(cond, w_mod, b_mod):
    rows, n = cond.shape[0], w_mod.shape[1]
    return pl.pallas_call(
        _adaln_kernel,
        grid=(n // ADALN_BLK,),
        in_specs=[pl.BlockSpec((rows, D_MODEL), lambda j: (0, 0)),
                  pl.BlockSpec((D_MODEL, ADALN_BLK), lambda j: (0, j)),
                  pl.BlockSpec((1, ADALN_BLK), lambda j: (0, j))],
        out_specs=pl.BlockSpec((rows, ADALN_BLK), lambda j: (0, j)),
        out_shape=jax.ShapeDtypeStruct((rows, n), F32),
        compiler_params=_cparams(("parallel",)),
        name="adaln",
    )(cond, w_mod, b_mod.reshape(1, n))


def _mod_norm(x, shift, scale, g):
    ms = jnp.mean(x * x, axis=-1, keepdims=True)
    y = x * lax.rsqrt(ms + EPS) * g
    return y * (1.0 + scale) + shift


def _rope(t, cos, sin_signed, first):
    outs = []
    for h in range(H_ATT):
        th = t[:, h * LANES:(h + 1) * LANES]
        partner = jnp.where(first, pltpu.roll(th, LANES - ROPE_FREQS, 1), pltpu.roll(th, ROPE_FREQS, 1))
        outs.append(th * cos + partner * sin_signed)
    return jnp.concatenate(outs, axis=1)


def _qkv_kernel(*refs, rope, emit_f32):
    x_ref, shift_ref, scale_ref, ng_ref, w_ref = refs[:5]
    pos = 5
    if rope:
        cos_ref, sin_ref = refs[pos:pos + 2]
        pos += 2
    qt_ref, k_ref, vt_ref = refs[pos:pos + 3]
    pos += 3
    if emit_f32:
        k32_ref, v32_ref = refs[pos:pos + 2]

    hb = _mod_norm(x_ref[...], shift_ref[...], scale_ref[...], ng_ref[...]).astype(BF16)
    q = jnp.dot(hb, w_ref[:, 0:D_ATT], preferred_element_type=F32)
    k = jnp.dot(hb, w_ref[:, D_ATT:2 * D_ATT], preferred_element_type=F32)
    v = jnp.dot(hb, w_ref[:, 2 * D_ATT:3 * D_ATT], preferred_element_type=F32)
    if rope:
        cos = cos_ref[...]
        sin_signed = sin_ref[...]
        lane = lax.broadcasted_iota(jnp.int32, cos.shape, 1)
        first = (lane % (2 * ROPE_FREQS)) < ROPE_FREQS
        q = _rope(q, cos, sin_signed, first)
        k = _rope(k, cos, sin_signed, first)
    qt_ref[...] = (q * (LOG2E / math.sqrt(HD_QK))).T.astype(BF16)
    k_ref[...] = k.astype(BF16)
    vt_ref[...] = v.T.astype(BF16)
    if emit_f32:
        k32_ref[...] = k
        v32_ref[...] = v


def _qkv_proj(x, mod, mod_row, norm_g, w_bf, rope_tabs, emit_f32, tm):
    B, L, _ = x.shape
    grid = (B, L // tm)
    row = mod_row

    in_specs = [
        pl.BlockSpec((None, tm, D_MODEL), lambda b, i: (b, i, 0)),
        pl.BlockSpec((None, 1, D_MODEL), lambda b, i: (row(b), 0, 0)),
        pl.BlockSpec((None, 1, D_MODEL), lambda b, i: (row(b), 0, 1)),
        pl.BlockSpec((1, D_MODEL), lambda b, i: (0, 0)),
        pl.BlockSpec((D_MODEL, 3 * D_ATT), lambda b, i: (0, 0), pipeline_mode=pl.Buffered(1)),
    ]
    args = [x, mod, mod, norm_g, w_bf]
    if rope_tabs is not None:
        in_specs += [pl.BlockSpec((tm, LANES), lambda b, i: (i, 0)),
                     pl.BlockSpec((tm, LANES), lambda b, i: (i, 0))]
        args += list(rope_tabs)
    out_specs = [
        pl.BlockSpec((None, D_ATT, tm), lambda b, i: (b, 0, i)),
        pl.BlockSpec((None, tm, D_ATT), lambda b, i: (b, i, 0)),
        pl.BlockSpec((None, D_ATT, tm), lambda b, i: (b, 0, i)),
    ]
    out_shape = [
        jax.ShapeDtypeStruct((B, D_ATT, L), BF16),
        jax.ShapeDtypeStruct((B, L, D_ATT), BF16),
        jax.ShapeDtypeStruct((B, D_ATT, L), BF16),
    ]
    if emit_f32:
        out_specs += [pl.BlockSpec((None, tm, D_ATT), lambda b, i: (b, i, 0))] * 2
        out_shape += [jax.ShapeDtypeStruct((B, L, D_ATT), F32)] * 2
    return pl.pallas_call(
        functools.partial(_qkv_kernel, rope=rope_tabs is not None, emit_f32=emit_f32),
        grid=grid, in_specs=in_specs, out_specs=out_specs, out_shape=out_shape,
        compiler_params=_cparams(("parallel", "parallel")),
        name="qkv_proj",
    )(*args)


def _gzx_kernel(x_ref, shift_ref, scale_ref, ng_ref, wg_ref, wz_ref, wx_ref, wbc_ref, wdt_ref,
                g_ref, z_ref, xbc_ref, dtt_ref):
    n_sub = x_ref.shape[0] // SUB_ROWS

    def prologue(r):
        rows = slice(r * SUB_ROWS, (r + 1) * SUB_ROWS)
        return _mod_norm(x_ref[rows, :], shift_ref[...], scale_ref[...], ng_ref[...]).astype(BF16)

    hb = prologue(0)
    for r in range(n_sub):
        rows = slice(r * SUB_ROWS, (r + 1) * SUB_ROWS)
        hb_next = prologue(r + 1) if r + 1 < n_sub else None
        g_ref[rows, :] = jnp.dot(hb, wg_ref[...], preferred_element_type=F32)
        z_ref[rows, :] = jnp.dot(hb, wz_ref[...], preferred_element_type=F32)
        xbc_ref[rows, 0:D_SSD] = jnp.dot(hb, wx_ref[...], preferred_element_type=F32)
        bc = jnp.dot(hb, wbc_ref[...], preferred_element_type=F32)
        nb = SSD_GROUPS * D_STATE
        pieces = []
        for g in range(SSD_GROUPS):
            pieces += [bc[:, g * D_STATE:(g + 1) * D_STATE], bc[:, nb + g * D_STATE:nb + (g + 1) * D_STATE]]
        xbc_ref[rows, D_SSD:CONV_DIM] = jnp.concatenate(pieces, axis=1)
        dt = jnp.dot(hb, wdt_ref[...], preferred_element_type=F32)
        dtt_ref[:, rows] = dt.T[:2 * H_SSD, :]
        hb = hb_next


def _gzx_proj(x, mod, mod_row, norm_g, w_bf, w_dt, tm):
    bc_w = 2 * SSD_GROUPS * D_STATE
    bc_blk = (4 * D_ATT + 2 * D_SSD) // bc_w
    B, L, _ = x.shape
    row = mod_row
    once = pl.Buffered(1)
    in_specs = [
        pl.BlockSpec((None, tm, D_MODEL), lambda b, i: (b, i, 0)),
        pl.BlockSpec((None, 1, D_MODEL), lambda b, i: (row(b), 0, 0)),
        pl.BlockSpec((None, 1, D_MODEL), lambda b, i: (row(b), 0, 1)),
        pl.BlockSpec((1, D_MODEL), lambda b, i: (0, 0)),
        pl.BlockSpec((D_MODEL, D_ATT), lambda b, i: (0, 3), pipeline_mode=once),
        pl.BlockSpec((D_MODEL, D_SSD), lambda b, i: (0, 4), pipeline_mode=once),
        pl.BlockSpec((D_MODEL, D_SSD), lambda b, i: (0, 5), pipeline_mode=once),
        pl.BlockSpec((D_MODEL, bc_w), lambda b, i: (0, bc_blk), pipeline_mode=once),
        pl.BlockSpec((D_MODEL, DT_PAD), lambda b, i: (0, 0), pipeline_mode=once),
    ]
    widths = (D_ATT, D_SSD, CONV_DIM)
    out_specs = [pl.BlockSpec((None, tm, w), lambda b, i: (b, i, 0)) for w in widths]
    out_shape = [jax.ShapeDtypeStruct((B, L, w), F32) for w in widths]
    out_specs.append(pl.BlockSpec((None, 2 * H_SSD, tm), lambda b, i: (b, 0, i)))
    out_shape.append(jax.ShapeDtypeStruct((B, 2 * H_SSD, L), F32))
    return pl.pallas_call(
        _gzx_kernel, grid=(B, L // tm), in_specs=in_specs, out_specs=out_specs, out_shape=out_shape,
        compiler_params=_cparams(("parallel", "parallel")),
        name="gzx_proj",
    )(x, mod, mod, norm_g, w_bf, w_bf, w_bf, w_bf, w_dt)


def _attn_kernel(*refs, n_src, lam_init, carry):
    qt_ref = refs[0]
    pos = 1
    if carry:
        qt_next_ref = refs[pos]
        k_next_refs = refs[pos + 1:pos + 1 + n_src]
        pos += 1 + n_src
    srcs = [(refs[pos + 2 * i], refs[pos + 1 + 2 * i]) for i in range(n_src)]
    pos += 2 * n_src
    g_ref, sg_ref, lq1_ref, lk1_ref, lq2_ref, lk2_ref, o_ref = refs[pos:pos + 7]
    if carry:
        s_sc, m_sc = refs[pos + 7:pos + 9]

    tq = ATT_TQ
    n_heads = qt_ref.shape[0] // HD_V
    chains = [(hh, t) for hh in range(n_heads) for t in range(qt_ref.shape[1] // tq)]
    n_ch = len(chains)
    row = lax.broadcasted_iota(jnp.int32, (HD_V, tq), 0)
    zero = jnp.zeros((HD_V, tq), BF16)

    def stacked_q(ref, hh, t):
        qt = ref[hh * HD_V:(hh + 1) * HD_V, t * tq:(t + 1) * tq]
        return jnp.concatenate([jnp.where(row < HD_QK, qt, zero), jnp.where(row >= HD_QK, qt, zero)], axis=1)

    q2t = [stacked_q(qt_ref, hh, t) for hh, t in chains]

    blocks = []
    for si, (k_ref, vt_ref) in enumerate(srcs):
        n_keys = k_ref.shape[0]
        tk = min(ATT_TK, n_keys)
        blocks += [(k_ref, vt_ref, j * tk, tk, si) for j in range(n_keys // tk)]

    lam = (jnp.exp(jnp.sum(lq1_ref[...] * lk1_ref[...], axis=1, keepdims=True))
           - jnp.exp(jnp.sum(lq2_ref[...] * lk2_ref[...], axis=1, keepdims=True)) + lam_init)

    def score_piece(c, j, handover=None):
        k_ref, _, off, tk, si = blocks[j]
        hh = chains[c][0]
        if handover is not None:
            k_ref = k_next_refs[si]
        s = jnp.dot(k_ref[off:off + tk, hh * HD_V:(hh + 1) * HD_V],
                    q2t[c] if handover is None else handover, preferred_element_type=F32)
        return s, jnp.max(s, axis=0, keepdims=True)

    def prob_piece(c, j, s, m, l, acc):
        _, vt_ref, off, tk, _ = blocks[j]
        hh = chains[c][0]
        p = jnp.exp2(s - m)
        l = l + jnp.sum(p, axis=0, keepdims=True)
        acc = acc + jnp.dot(vt_ref[hh * HD_V:(hh + 1) * HD_V, off:off + tk], p.astype(BF16),
                            preferred_element_type=F32)
        return l, acc

    def epilogue(c, l, acc):
        hh, t = chains[c]
        o_n = acc / l
        o = (o_n[:, :tq] - lam * o_n[:, tq:]).T
        ms = jnp.mean(o * o, axis=-1, keepdims=True)
        att = o * lax.rsqrt(ms + EPS) * sg_ref[...] * (1.0 - lam_init)
        gate = _silu(g_ref[t * tq:(t + 1) * tq, hh * HD_V:(hh + 1) * HD_V])
        o_ref[t * tq:(t + 1) * tq, hh * HD_V:(hh + 1) * HD_V] = (att * gate).astype(BF16)

    n_pc = len(blocks)
    neg_inf = jnp.full((1, 2 * tq), -jnp.inf, F32)
    if not carry:
        prev = None
        for c in range(n_ch + 1):
            cur_s, cur_m = [], neg_inf
            if prev is not None:
                l = jnp.zeros((1, 2 * tq), F32)
                acc = jnp.zeros((HD_V, 2 * tq), F32)
            for j in range(n_pc):
                if c < n_ch:
                    s, s_max = score_piece(c, j)
                    cur_s.append(s)
                    cur_m = jnp.maximum(cur_m, s_max)
                if prev is not None:
                    l, acc = prob_piece(prev[0], j, prev[1][j], prev[2], l, acc)
            if prev is not None:
                epilogue(prev[0], l, acc)
            prev = (c, cur_s, cur_m) if c < n_ch else None
        return

    dst = [sum(b[3] for b in blocks[:j]) for j in range(n_pc)]
    step = ((pl.program_id(0) * pl.num_programs(1) + pl.program_id(1)) * pl.num_programs(2)
            + pl.program_id(2))
    cur, nxt = step % 2, (step + 1) % 2

    @pl.when(step == 0)
    def _():
        m0 = neg_inf
        for j in range(n_pc):
            s, s_max = score_piece(0, j)
            s_sc[0, dst[j]:dst[j] + blocks[j][3], :] = s
            m0 = jnp.maximum(m0, s_max)
        m_sc[0] = jnp.broadcast_to(m0, m_sc.shape[1:])

    hh0, t0 = chains[0]
    q_next = stacked_q(qt_next_ref, hh0, t0)
    prev_s, prev_m = None, m_sc[cur][0:1, :]
    for u in range(n_ch):
        l = jnp.zeros((1, 2 * tq), F32)
        acc = jnp.zeros((HD_V, 2 * tq), F32)
        nxt_s, nxt_m = [], neg_inf
        for j in range(n_pc):
            rows = slice(dst[j], dst[j] + blocks[j][3])
            if u + 1 < n_ch:
                s, s_max = score_piece(u + 1, j)
                nxt_s.append(s)
            else:
                s, s_max = score_piece(0, j, q_next)
                s_sc[nxt, rows, :] = s
            nxt_m = jnp.maximum(nxt_m, s_max)
            s_u = s_sc[cur, rows, :] if u == 0 else prev_s[j]
            l, acc = prob_piece(u, j, s_u, prev_m, l, acc)
        if u + 1 == n_ch:
            m_sc[nxt] = jnp.broadcast_to(nxt_m, m_sc.shape[1:])
        epilogue(u, l, acc)
        prev_s, prev_m = nxt_s, nxt_m


def _attention(qt, kvs, g, subln_g, lams, lam_init, tq, heads_per_step):
    B, _, L = qt.shape
    hw = heads_per_step * HD_V
    nq = L // tq
    nh = H_ATT // heads_per_step
    carry = nq > 1
    in_specs = [pl.BlockSpec((None, hw, tq), lambda b, h, i: (b, h, i))]
    args = [qt]
    scratch = []
    if carry:
        def succ(b, h, i):
            end_i, end_h = i == nq - 1, h == nh - 1
            last = end_i & end_h & (b == B - 1)
            b1 = jnp.where(end_i & end_h & ~last, b + 1, b)
            h1 = jnp.where(last, h, jnp.where(end_i, jnp.where(end_h, 0, h + 1), h))
            i1 = jnp.where(last, i, jnp.where(end_i, 0, i + 1))
            return b1, h1, i1

        def q_next(b, h, i):
            b1, h1, i1 = succ(b, h, i)
            return b1, h1, i1

        def k_next(b, h, i):
            b1, h1, _ = succ(b, h, i)
            return b1, 0, h1

        in_specs.append(pl.BlockSpec((None, hw, tq), q_next))
        args.append(qt)
        for k, _ in kvs:
            in_specs.append(pl.BlockSpec((None, k.shape[1], hw), k_next))
            args.append(k)
        n_all = sum(k.shape[1] for k, _ in kvs)
        scratch = [pltpu.VMEM((2, n_all, 2 * ATT_TQ), F32), pltpu.VMEM((2, SUBLANES, 2 * ATT_TQ), F32)]
    for k, vt in kvs:
        n_keys = k.shape[1]
        in_specs += [pl.BlockSpec((None, n_keys, hw), lambda b, h, i: (b, 0, h)),
                     pl.BlockSpec((None, hw, n_keys), lambda b, h, i: (b, h, 0))]
        args += [k, vt]
    in_specs += [pl.BlockSpec((None, tq, hw), lambda b, h, i: (b, i, h)),
                 pl.BlockSpec((1, HD_V), lambda b, h, i: (0, 0))]
    in_specs += [pl.BlockSpec((1, HD_QK), lambda b, h, i: (0, 0))] * 4
    args += [g, subln_g] + list(lams)
    return pl.pallas_call(
        functools.partial(_attn_kernel, n_src=len(kvs), lam_init=lam_init, carry=carry),
        grid=(B, nh, nq),
        in_specs=in_specs,
        out_specs=pl.BlockSpec((None, tq, hw), lambda b, h, i: (b, i, h)),
        out_shape=jax.ShapeDtypeStruct((B, L, D_ATT), BF16),
        scratch_shapes=scratch,
        compiler_params=_cparams(("arbitrary",) * 3 if carry else ("parallel", "parallel", "arbitrary")),
        name="diff_attention",
    )(*args)


def _ssd_kernel(*refs, seq_len, has_h0):
    (x_ref, bc_ref, dtt_ref, cwx_ref, cwbc_ref, cbx_ref, cbbc_ref,
     dtb_ref, alog_ref, dsk_ref) = refs[:10]
    pos = 10
    if has_h0:
        h0_ref = refs[pos]
        pos += 1
    y_ref, hfin_ref, xt_sc, yt_sc, b_sc, ct_sc = refs[pos:pos + 6]

    nc = seq_len // CHUNK
    n_grp = x_ref.shape[1] // GROUP_W
    row_i = lax.broadcasted_iota(jnp.int32, (CHUNK, CHUNK), 0)
    col_i = lax.broadcasted_iota(jnp.int32, (CHUNK, CHUNK), 1)
    row1 = lax.broadcasted_iota(jnp.int32, (CHUNK, 1), 0)

    def conv_silu(src, w_ref, b_ref, c):
        r0 = pl.multiple_of(c * CHUNK, CHUNK)
        rp = pl.multiple_of(jnp.maximum(r0 - SUBLANES, 0), SUBLANES)
        rn = pl.multiple_of(jnp.minimum(r0 + CHUNK, seq_len - SUBLANES), SUBLANES)
        u = src[pl.ds(r0, CHUNK), :]
        pr = jnp.where(c > 0, src[pl.ds(rp, SUBLANES), :][SUBLANES - 1:SUBLANES, :], 0.0)
        nx = jnp.where(c < nc - 1, src[pl.ds(rn, SUBLANES), :][0:1, :], 0.0)
        prev = jnp.where(row1 == 0, pr, pltpu.roll(u, 1, 0))
        nxt = jnp.where(row1 == CHUNK - 1, nx, pltpu.roll(u, CHUNK - 1, 0))
        w = 0.5 * w_ref[...]
        hx = 0.5 * b_ref[...] + prev * w[0:1, :] + u * w[1:2, :] + nxt * w[2:3, :]
        return hx + hx * jnp.tanh(hx)

    def conv_chunk(c, carry):
        xt = conv_silu(x_ref, cwx_ref, cbx_ref, c).T
        xt_sc[c] = xt
        yt_sc[c] = dsk_ref[...] * xt
        bc = conv_silu(bc_ref, cwbc_ref, cbbc_ref, c)
        for g in range(n_grp):
            bc_g = bc[:, g * LANES:(g + 1) * LANES]
            b_sc[c, g] = bc_g[:, :D_STATE].astype(BF16)
            ct_sc[c, g] = bc_g.T[D_STATE:, :].astype(BF16)
        return carry

    lax.fori_loop(0, nc, conv_chunk, 0, unroll=math.gcd(CONV_UNROLL, nc))

    dir_heads = 2 * HEADS_PER_GROUP
    dtb = [dtb_ref[g] for g in range(n_grp)]
    a_neg = [-jnp.exp(alog_ref[g]) * LOG2E for g in range(n_grp)]

    masks = (row_i <= col_i, row_i >= col_i)
    tri = tuple(jnp.where(m, 1.0, 0.0).astype(BF16) for m in masks)
    ends = (CHUNK - 1, 0)

    def scan_prep(items):
        out = []
        for d in range(2):
            sp_d = []
            for g, dd, c in items:
                if dd == d:
                    t0 = pl.multiple_of(c * CHUNK, CHUNK)
                    dt_g = dtt_ref[g * dir_heads:(g + 1) * dir_heads, pl.ds(t0, CHUNK)]
                    sp_d.append(_softplus(dt_g + dtb[g]))
            x = jnp.concatenate([s * a_neg[it[0]] for s, it in
                                 zip(sp_d, [it for it in items if it[1] == d])], axis=0)
            hi = x.astype(BF16)
            r1 = x - hi.astype(F32)
            mid = r1.astype(BF16)
            lo = (r1 - mid.astype(F32)).astype(BF16)
            parts = jnp.dot(jnp.concatenate([hi, mid, lo], axis=0), tri[d], preferred_element_type=F32)
            nr = x.shape[0]
            cum_d = parts[:nr] + parts[nr:2 * nr] + parts[2 * nr:]
            out.append((jnp.concatenate(sp_d, axis=0), cum_d, cum_d.T))
        return tuple(out)

    def scan_items(items, prep, states):
        n = len(items)
        sp, cum, cum_col, cb_t = [None] * n, [None] * n, [None] * n, [None] * n
        for d in range(2):
            sp_d, cum_d, cum_dt = prep[d]
            for i, k in enumerate([k for k, it in enumerate(items) if it[1] == d]):
                rows = slice(i * dir_heads, (i + 1) * dir_heads)
                sp[k], cum[k], cum_col[k] = sp_d[rows], cum_d[rows], cum_dt[:, rows]
        for k, (g, d, c) in enumerate(items):
            cb_t[k] = jnp.dot(b_sc[c, g], ct_sc[c, g], preferred_element_type=F32)
        xdt_b, lm_b, x_dec, y_scale, s_scale = [], [], [], [], []
        for k, (g, d, c) in enumerate(items):
            xt = xt_sc[c, g * GROUP_W:(g + 1) * GROUP_W, :]
            xdt_k, lm_k, x_dec_k, y_scale_k, s_scale_k = [], [], [], [], []
            for hh in range(HEADS_PER_GROUP):
                r = d * HEADS_PER_GROUP + hh
                a_row = cum[k][r:r + 1, :]
                seg = a_row - cum_col[k][:, r:r + 1]
                lmat = jnp.exp2(jnp.where(masks[d], seg, -jnp.inf))
                xdt = xt[hh * SSD_HEADDIM:(hh + 1) * SSD_HEADDIM, :] * sp[k][r:r + 1, :]
                a_end = a_row[:, ends[d]:ends[d] + 1]
                xdt_k.append(xdt.astype(BF16))
                lm_k.append((cb_t[k] * lmat).astype(BF16))
                x_dec_k.append((xdt * jnp.exp2(a_end - a_row)).astype(BF16))
                y_scale_k.append(jnp.broadcast_to(jnp.exp2(a_row), (SSD_HEADDIM, CHUNK)))
                s_scale_k.append(jnp.broadcast_to(jnp.exp2(a_end), (SSD_HEADDIM, D_STATE)))
            xdt_b.append(xdt_k)
            lm_b.append(lm_k)
            x_dec.append(jnp.concatenate(x_dec_k, axis=0))
            y_scale.append(jnp.concatenate(y_scale_k, axis=0))
            s_scale.append(jnp.concatenate(s_scale_k, axis=0))
        y_diag, st = [], []
        for k, (g, d, c) in enumerate(items):
            y_diag.append(jnp.concatenate(
                [jnp.dot(xdt_b[k][hh], lm_b[k][hh], preferred_element_type=F32)
                 for hh in range(HEADS_PER_GROUP)], axis=0))
            st.append(jnp.dot(x_dec[k], b_sc[c, g], preferred_element_type=F32))
        states = list(states)
        for k, (g, d, c) in enumerate(items):
            si = 2 * g + d
            rows = slice(g * GROUP_W, (g + 1) * GROUP_W)
            y_off = jnp.dot(states[si].astype(BF16), ct_sc[c, g], preferred_element_type=F32)
            yt_sc[c, rows, :] = yt_sc[c, rows, :] + y_diag[k] + y_off * y_scale[k]
            states[si] = states[si] * s_scale[k] + st[k]
        return tuple(states)

    if has_h0:
        s_init = tuple(h0_ref[d, g] for g in range(n_grp) for d in range(2))
    else:
        s_init = (jnp.zeros((GROUP_W, D_STATE), F32),) * (2 * n_grp)

    unroll = math.gcd(max(SCAN_UNROLL // n_grp, 1), nc)
    n_it = nc // unroll

    def items_of(i):
        items = []
        for u in range(unroll):
            for g in range(n_grp):
                items += [(g, 0, i * unroll + u), (g, 1, nc - 1 - (i * unroll + u))]
        return items

    def scan_step(i, carry):
        states, prep = carry
        prep_next = scan_prep(items_of(jnp.minimum(i + 1, n_it - 1)))
        return scan_items(items_of(i), prep, states), prep_next

    s_fin, _ = lax.fori_loop(0, n_it, scan_step, (s_init, scan_prep(items_of(0))))
    for g in range(n_grp):
        for d in range(2):
            hfin_ref[d, g] = s_fin[2 * g + d]

    def emit_chunk(c, carry):
        y_ref[pl.ds(pl.multiple_of(c * CHUNK, CHUNK), CHUNK), :] = yt_sc[c].T
        return carry

    lax.fori_loop(0, nc, emit_chunk, 0, unroll=math.gcd(CONV_UNROLL, nc))


def _ssd(xbc, dtt, conv_w, conv_b, dtb_g, alog_g, dsk, h0, n_grp, dtt_merge):
    B, L, _ = xbc.shape
    nc = L // CHUNK
    xw, bw = n_grp * GROUP_W, n_grp * LANES
    bc_blk = D_SSD // bw
    dir_heads = 2 * HEADS_PER_GROUP
    state_spec = pl.BlockSpec((None, 2, n_grp, GROUP_W, D_STATE), lambda b, g: (b, 0, g, 0, 0))
    in_specs = [
        pl.BlockSpec((None, L, xw), lambda b, g: (b, 0, g)),
        pl.BlockSpec((None, L, bw), lambda b, g: (b, 0, bc_blk + g)),
        pl.BlockSpec((None, n_grp * dir_heads, L), lambda b, g: (b // dtt_merge, g, b % dtt_merge)),
        pl.BlockSpec((3, xw), lambda b, g: (0, g)),
        pl.BlockSpec((3, bw), lambda b, g: (0, bc_blk + g)),
        pl.BlockSpec((1, xw), lambda b, g: (0, g)),
        pl.BlockSpec((1, bw), lambda b, g: (0, bc_blk + g)),
        pl.BlockSpec((n_grp, dir_heads, LANES), lambda b, g: (g, 0, 0)),
        pl.BlockSpec((n_grp, dir_heads, LANES), lambda b, g: (g, 0, 0)),
        pl.BlockSpec((xw, LANES), lambda b, g: (g, 0)),
    ]
    args = [xbc, xbc, dtt, conv_w, conv_w, conv_b, conv_b, dtb_g, alog_g, dsk]
    if h0 is not None:
        in_specs.append(state_spec)
        args.append(h0)
    return pl.pallas_call(
        functools.partial(_ssd_kernel, seq_len=L, has_h0=h0 is not None),
        grid=(B, SSD_GROUPS // n_grp),
        in_specs=in_specs,
        out_specs=[pl.BlockSpec((None, L, xw), lambda b, g: (b, 0, g)), state_spec],
        out_shape=[jax.ShapeDtypeStruct((B, L, D_SSD), F32),
                   jax.ShapeDtypeStruct((B, 2, SSD_GROUPS, GROUP_W, D_STATE), F32)],
        scratch_shapes=[pltpu.VMEM((nc, xw, CHUNK), F32), pltpu.VMEM((nc, xw, CHUNK), F32),
                        pltpu.VMEM((nc, n_grp, CHUNK, D_STATE), BF16),
                        pltpu.VMEM((nc, n_grp, D_STATE, CHUNK), BF16)],
        compiler_params=_cparams(("parallel", "parallel")),
        name="ssd_scan",
    )(*args)


def _out_kernel(x_ref, att_ref, y_ref, z_ref, sng_ref, w_ref, gate_ref, fg_ref, o_ref):
    n_sub = x_ref.shape[0] // SUB_ROWS

    def prologue(r):
        rows = slice(r * SUB_ROWS, (r + 1) * SUB_ROWS)
        yz = y_ref[rows, :] * _silu(z_ref[rows, :])
        ms = jnp.mean(yz * yz, axis=-1, keepdims=True)
        return (yz * lax.rsqrt(ms + EPS) * sng_ref[...]).astype(BF16)

    ssd_y = prologue(0)
    for r in range(n_sub):
        rows = slice(r * SUB_ROWS, (r + 1) * SUB_ROWS)
        ssd_next = prologue(r + 1) if r + 1 < n_sub else None
        out = jnp.dot(att_ref[rows, :], w_ref[0:D_ATT, :], preferred_element_type=F32)
        out = out + jnp.dot(ssd_y, w_ref[D_ATT:D_ATT + D_SSD, :], preferred_element_type=F32)
        res = x_ref[rows, :] + gate_ref[...] * out
        ms2 = jnp.mean(res * res, axis=-1, keepdims=True)
        o_ref[rows, :] = res * lax.rsqrt(ms2 + EPS) * fg_ref[...]
        ssd_y = ssd_next


def _out_proj(x, att, y, z, ssd_norm_g, w_out, mod, mod_row, final_g, tm):
    B, L, _ = x.shape
    row = mod_row
    tok = lambda b, i: (b, i, 0)
    in_specs = [
        pl.BlockSpec((None, tm, D_MODEL), tok),
        pl.BlockSpec((None, tm, D_ATT), tok),
        pl.BlockSpec((None, tm, D_SSD), tok),
        pl.BlockSpec((None, tm, D_SSD), tok),
        pl.BlockSpec((1, D_SSD), lambda b, i: (0, 0)),
        pl.BlockSpec((D_ATT + D_SSD, D_MODEL), lambda b, i: (0, 0), pipeline_mode=pl.Buffered(1)),
        pl.BlockSpec((None, 1, D_MODEL), lambda b, i: (row(b), 0, 2)),
        pl.BlockSpec((1, D_MODEL), lambda b, i: (0, 0)),
    ]
    return pl.pallas_call(
        _out_kernel, grid=(B, L // tm), in_specs=in_specs,
        out_specs=pl.BlockSpec((None, tm, D_MODEL), tok),
        out_shape=jax.ShapeDtypeStruct((B, L, D_MODEL), F32),
        compiler_params=_cparams(("parallel", "parallel")),
        name="out_proj",
    )(x, att, y, z, ssd_norm_g, w_out, mod, final_g)


def _rope_tables(L):
    rows = L // GRID_W
    f32 = np.float32
    row_ids = np.repeat(np.arange(rows), GRID_W).astype(f32)
    col_ids = np.tile(np.arange(GRID_W), rows).astype(f32)
    inv = (f32(ROPE_BASE) ** (-np.arange(ROPE_FREQS, dtype=f32) / f32(ROPE_FREQS))).astype(f32)
    ang_r = row_ids[:, None] * inv
    ang_c = col_ids[:, None] * inv
    cr, sr, cc, sc = np.cos(ang_r), np.sin(ang_r), np.cos(ang_c), np.sin(ang_c)
    cos64 = np.concatenate([cr, cr, cc, cc], axis=1)
    sin64 = np.concatenate([-sr, sr, -sc, sc], axis=1)
    return jnp.asarray(np.tile(cos64, (1, 2)), F32), jnp.asarray(np.tile(sin64, (1, 2)), F32)


def _bc_perm():
    idx = list(range(D_SSD))
    for g in range(SSD_GROUPS):
        idx += [D_SSD + g * D_STATE + n for n in range(D_STATE)]
        idx += [D_SSD + SSD_GROUPS * D_STATE + g * D_STATE + n for n in range(D_STATE)]
    return jnp.asarray(idx, dtype=jnp.int32)


def _dt_perm():
    idx = []
    for g in range(SSD_GROUPS):
        for d in range(2):
            for hh in range(HEADS_PER_GROUP):
                idx.append(d * H_SSD + g * HEADS_PER_GROUP + hh)
    return jnp.asarray(idx, dtype=jnp.int32)


def _group_rows(v2h):
    t = v2h.reshape(2, SSD_GROUPS, HEADS_PER_GROUP).transpose(1, 0, 2).reshape(SSD_GROUPS, 2 * HEADS_PER_GROUP)
    return jnp.broadcast_to(t[:, :, None], (SSD_GROUPS, 2 * HEADS_PER_GROUP, LANES))


def _layer(x, mod, mod_row, params, rope_tabs, past_kv, h0, emit_f32, lam_init, final_g, tm, tq,
           heads_per_step, groups_per_step, tm_out, row_merge):
    (norm_g, w_bf, w_dt, lams, subln_g, conv_w, conv_b, dtb_g, alog_g, dsk, ssd_norm_g,
     w_out) = params
    B, L, _ = x.shape
    merged = lambda a: a.reshape(B // row_merge, L * row_merge, a.shape[-1])
    split = lambda a: a.reshape(B, L, a.shape[-1])
    proj = _qkv_proj(x, mod, mod_row, norm_g, w_bf, rope_tabs, emit_f32, tm)
    qt, k, vt = proj[:3]
    g, z, xbc, dtt = _gzx_proj(merged(x), mod, mod_row, norm_g, w_bf, w_dt, tm * row_merge)
    g, z, xbc = split(g), split(z), split(xbc)
    kvs = [(k, vt)]
    if past_kv is not None:
        kvs.append(past_kv)
    att = _attention(qt, kvs, g, subln_g, lams, lam_init, tq, heads_per_step)
    y, hfin = _ssd(xbc, dtt, conv_w, conv_b, dtb_g, alog_g, dsk, h0, groups_per_step, row_merge)
    out = _out_proj(merged(x), merged(att), merged(y), merged(z), ssd_norm_g, w_out, mod, mod_row,
                    final_g, tm_out)
    return split(out), proj[3:], hfin


def kernel(x_prompt, x_sample, cache_k, cache_v, state_ssd, c, c_ctx, w_mod, b_mod, norm_g, w_in,
           lambda_q1, lambda_k1, lambda_q2, lambda_k2, subln_g, conv_w, conv_b, dt_bias, A_log,
           D_skip, ssd_norm_g, w_out, final_g):
    b_ctx, l_ctx, _ = x_prompt.shape
    b_dec, l_dec, _ = x_sample.shape
    l_past = cache_k.shape[2]
    depth = w_mod.shape[0]
    assert depth == 1
    lam_init = 0.8 - 0.6 * math.exp(-0.3 * 0)

    w_bf = w_in[0].astype(BF16)
    o2 = 4 * D_ATT + D_SSD
    bc_perm = _bc_perm()
    w_dt = jnp.pad(w_bf[:, o2 + CONV_DIM:][:, _dt_perm()], ((0, 0), (0, DT_PAD - 2 * H_SSD)))
    conv_w_p = conv_w[0][:, bc_perm]
    conv_b_p = conv_b[0][bc_perm].reshape(1, CONV_DIM)
    dtb_g = _group_rows(dt_bias[0])
    alog_g = _group_rows(A_log[0])
    dsk = jnp.broadcast_to(jnp.repeat(D_skip[0], SSD_HEADDIM)[:, None], (D_SSD, LANES))
    lams = [a[0].reshape(1, HD_QK) for a in (lambda_q1, lambda_k1, lambda_q2, lambda_k2)]
    params = (norm_g[0].reshape(1, D_MODEL), w_bf, w_dt, lams, subln_g[0].reshape(1, HD_V),
              conv_w_p, conv_b_p, dtb_g, alog_g, dsk, ssd_norm_g[0].reshape(1, D_SSD),
              w_out[0].astype(BF16))
    fg = final_g.reshape(1, D_MODEL)

    assert b_dec < SUBLANES
    cond = jnp.zeros((SUBLANES, D_MODEL), F32).at[:b_dec].set(c).at[b_dec].set(c_ctx)
    mod = _adaln(cond, w_mod[0], b_mod[0]).reshape(SUBLANES, 1, 3 * D_MODEL)

    y_prompt, (k_ctx, v_ctx), h_ctx = _layer(
        x_prompt, mod, lambda b: b_dec, params, None, None, None, True, lam_init, fg,
        tm=256, tq=256, heads_per_step=H_ATT, groups_per_step=SSD_GROUPS, tm_out=512, row_merge=2)

    rope_tabs = _rope_tables(l_dec)
    k_past = cache_k[:, 0].reshape(b_dec, l_past, D_ATT).astype(BF16)
    vt_past = cache_v[:, 0].reshape(b_dec, l_past, D_ATT).transpose(0, 2, 1).astype(BF16)
    h0 = state_ssd[:, 0].reshape(b_dec, 2, SSD_GROUPS, GROUP_W, D_STATE)
    y_sample, _, _ = _layer(
        x_sample, mod, lambda b: b, params, rope_tabs, (k_past, vt_past), h0, False, lam_init, fg,
        tm=512, tq=512, heads_per_step=1, groups_per_step=1, tm_out=1024, row_merge=1)

    new_cache_k = k_ctx.reshape(b_ctx, 1, l_ctx, H_ATT, 2 * HD_QK)
    new_cache_v = v_ctx.reshape(b_ctx, 1, l_ctx, H_ATT, HD_V)
    new_state = h_ctx.reshape(b_ctx, 1, 2, H_SSD, SSD_HEADDIM, D_STATE)
    return (y_prompt, y_sample, new_cache_k, new_cache_v, new_state)
```

```python
import functools
import math

import jax
import jax.numpy as jnp
from jax import lax
from jax.experimental import pallas as pl
from jax.experimental.pallas import tpu as pltpu

F32 = jnp.float32
BF16 = jnp.bfloat16

D_MODEL = 1024
GRID_W = 64
HD_QK = 64
HD_V = 128
H_ATT = 8
D_ATT = 1024
D_SSD = 1024
SSD_HEADDIM = 64
H_SSD = 16
SSD_GROUPS = 4
HEADS_PER_GROUP = 4
D_STATE = 64
CONV_DIM = D_SSD + 2 * SSD_GROUPS * D_STATE
CHUNK = 128
ROPE_BASE = 10000.0
ROPE_FREQS = 16
EPS = 1e-6

LANES = 128
SUBLANES = 8
DT_PAD = LANES
GROUP_W = HEADS_PER_GROUP * SSD_HEADDIM
VMEM_LIMIT = 56 * 1024 * 1024
ADALN_BLK = 1024
ATT_TK = 512
ATT_TQ = 256
SUB_ROWS = 256
LOG2E = 1.4426950408889634
CONV_UNROLL = 8
SCAN_UNROLL = 8


def _cparams(sem):
    return pltpu.CompilerParams(dimension_semantics=sem, vmem_limit_bytes=VMEM_LIMIT)


def _silu(x):
    hx = 0.5 * x
    return hx + hx * jnp.tanh(hx)


def _softplus(x):
    return jnp.maximum(x, 0.0) + jnp.log1p(jnp.exp(-jnp.abs(x)))


def _adaln_kernel(cond_ref, w_ref, b_ref, o_ref):
    cond = cond_ref[...]
    s = _silu(cond).astype(BF16)
    o_ref[...] = jnp.dot(s, w_ref[...].astype(BF16), preferred_element_type=F32) + b_ref[...]


def _adaln(cond, w_mod, b_mod):
    rows, n = cond.shape[0], w_mod.shape[1]
    return pl.pallas_call(
        _adaln_kernel,
        grid=(n // ADALN_BLK,),
        in_specs=[pl.BlockSpec((rows, D_MODEL), lambda j: (0, 0)),
                  pl.BlockSpec((D_MODEL, ADALN_BLK), lambda j: (0, j)),
                  pl.BlockSpec((1, ADALN_BLK), lambda j: (0, j))],
        out_specs=pl.BlockSpec((rows, ADALN_BLK), lambda j: (0, j)),
        out_shape=jax.ShapeDtypeStruct((rows, n), F32),
        compiler_params=_cparams(("parallel",)),
        name="adaln",
    )(cond, w_mod, b_mod.reshape(1, n))


def _mod_norm(x, shift, scale, g):
    ms = jnp.mean(x * x, axis=-1, keepdims=True)
    y = x * lax.rsqrt(ms + EPS) * g
    return y * (1.0 + scale) + shift


def _rope(t, cos, sin_signed, first):
    outs = []
    for h in range(H_ATT):
        th = t[:, h * LANES:(h + 1) * LANES]
        partner = jnp.where(first, pltpu.roll(th, LANES - ROPE_FREQS, 1), pltpu.roll(th, ROPE_FREQS, 1))
        outs.append(th * cos + partner * sin_signed)
    return jnp.concatenate(outs, axis=1)


def _qkv_kernel(*refs, rope, emit_f32):
    x_ref, shift_ref, scale_ref, ng_ref, w_ref = refs[:5]
    pos = 5
    if rope:
        cos_ref, sin_ref = refs[pos:pos + 2]
        pos += 2
    qt_ref, k_ref, vt_ref = refs[pos:pos + 3]
    pos += 3
    if emit_f32:
        k32_ref, v32_ref = refs[pos:pos + 2]

    hb = _mod_norm(x_ref[...], shift_ref[...], scale_ref[...], ng_ref[...]).astype(BF16)
    q = jnp.dot(hb, w_ref[:, 0:D_ATT], preferred_element_type=F32)
    k = jnp.dot(hb, w_ref[:, D_ATT:2 * D_ATT], preferred_element_type=F32)
    v = jnp.dot(hb, w_ref[:, 2 * D_ATT:3 * D_ATT], preferred_element_type=F32)
    if rope:
        cos = cos_ref[...]
        sin_signed = sin_ref[...]
        lane = lax.broadcasted_iota(jnp.int32, cos.shape, 1)
        first = (lane % (2 * ROPE_FREQS)) < ROPE_FREQS
        q = _rope(q, cos, sin_signed, first)
        k = _rope(k, cos, sin_signed, first)
    qt_ref[...] = (q * (LOG2E / math.sqrt(HD_QK))).T.astype(BF16)
    k_ref[...] = k.astype(BF16)
    vt_ref[...] = v.T.astype(BF16)
    if emit_f32:
        k32_ref[...] = k
        v32_ref[...] = v


def _qkv_proj(x, mod, mod_row, norm_g, w_bf, rope_tabs, emit_f32, tm):
    B, L, _ = x.shape
    grid = (B, L // tm)
    row = mod_row

    in_specs = [
        pl.BlockSpec((None, tm, D_MODEL), lambda b, i: (b, i, 0)),
        pl.BlockSpec((None, 1, D_MODEL), lambda b, i: (row(b), 0, 0)),
        pl.BlockSpec((None, 1, D_MODEL), lambda b, i: (row(b), 0, 1)),
        pl.BlockSpec((1, D_MODEL), lambda b, i: (0, 0)),
        pl.BlockSpec((D_MODEL, 3 * D_ATT), lambda b, i: (0, 0), pipeline_mode=pl.Buffered(1)),
    ]
    args = [x, mod, mod, norm_g, w_bf]
    if rope_tabs is not None:
        in_specs += [pl.BlockSpec((tm, LANES), lambda b, i: (i, 0)),
                     pl.BlockSpec((tm, LANES), lambda b, i: (i, 0))]
        args += list(rope_tabs)
    out_specs = [
        pl.BlockSpec((None, D_ATT, tm), lambda b, i: (b, 0, i)),
        pl.BlockSpec((None, tm, D_ATT), lambda b, i: (b, i, 0)),
        pl.BlockSpec((None, D_ATT, tm), lambda b, i: (b, 0, i)),
    ]
    out_shape = [
        jax.ShapeDtypeStruct((B, D_ATT, L), BF16),
        jax.ShapeDtypeStruct((B, L, D_ATT), BF16),
        jax.ShapeDtypeStruct((B, D_ATT, L), BF16),
    ]
    if emit_f32:
        out_specs += [pl.BlockSpec((None, tm, D_ATT), lambda b, i: (b, i, 0))] * 2
        out_shape += [jax.ShapeDtypeStruct((B, L, D_ATT), F32)] * 2
    return pl.pallas_call(
        functools.partial(_qkv_kernel, rope=rope_tabs is not None, emit_f32=emit_f32),
        grid=grid, in_specs=in_specs, out_specs=out_specs, out_shape=out_shape,
        compiler_params=_cparams(("parallel", "parallel")),
        name="qkv_proj",
    )(*args)


def _gzx_kernel(x_ref, shift_ref, scale_ref, ng_ref, wg_ref, wx_ref, wbc_ref, wdt_ref,
                g_ref, xbc_ref, dtt_ref):
    n_sub = x_ref.shape[0] // SUB_ROWS

    def prologue(r):
        rows = slice(r * SUB_ROWS, (r + 1) * SUB_ROWS)
        return _mod_norm(x_ref[rows, :], shift_ref[...], scale_ref[...], ng_ref[...]).astype(BF16)

    hb = prologue(0)
    for r in range(n_sub):
        rows = slice(r * SUB_ROWS, (r + 1) * SUB_ROWS)
        hb_next = prologue(r + 1) if r + 1 < n_sub else None
        g_ref[rows, :] = jnp.dot(hb, wg_ref[...], preferred_element_type=F32)
        xbc_ref[rows, 0:D_SSD] = jnp.dot(hb, wx_ref[...], preferred_element_type=F32)
        bc = jnp.dot(hb, wbc_ref[...], preferred_element_type=F32)
        nb = SSD_GROUPS * D_STATE
        pieces = []
        for g in range(SSD_GROUPS):
            pieces += [bc[:, g * D_STATE:(g + 1) * D_STATE], bc[:, nb + g * D_STATE:nb + (g + 1) * D_STATE]]
        xbc_ref[rows, D_SSD:CONV_DIM] = jnp.concatenate(pieces, axis=1)
        dt = jnp.dot(hb, wdt_ref[...], preferred_element_type=F32)
        dtt_ref[:, rows] = dt.T[:2 * H_SSD, :]
        hb = hb_next


def _gzx_proj(x, mod, mod_row, norm_g, w_bf, w_dt, tm):
    bc_w = 2 * SSD_GROUPS * D_STATE
    bc_blk = (4 * D_ATT + 2 * D_SSD) // bc_w
    B, L, _ = x.shape
    row = mod_row
    once = pl.Buffered(1)
    in_specs = [
        pl.BlockSpec((None, tm, D_MODEL), lambda b, i: (b, i, 0)),
        pl.BlockSpec((None, 1, D_MODEL), lambda b, i: (row(b), 0, 0)),
        pl.BlockSpec((None, 1, D_MODEL), lambda b, i: (row(b), 0, 1)),
        pl.BlockSpec((1, D_MODEL), lambda b, i: (0, 0)),
        pl.BlockSpec((D_MODEL, D_ATT), lambda b, i: (0, 3), pipeline_mode=once),
        pl.BlockSpec((D_MODEL, D_SSD), lambda b, i: (0, 5), pipeline_mode=once),
        pl.BlockSpec((D_MODEL, bc_w), lambda b, i: (0, bc_blk), pipeline_mode=once),
        pl.BlockSpec((D_MODEL, DT_PAD), lambda b, i: (0, 0), pipeline_mode=once),
    ]
    widths = (D_ATT, CONV_DIM)
    out_specs = [pl.BlockSpec((None, tm, w), lambda b, i: (b, i, 0)) for w in widths]
    out_shape = [jax.ShapeDtypeStruct((B, L, w), F32) for w in widths]
    out_specs.append(pl.BlockSpec((None, 2 * H_SSD, tm), lambda b, i: (b, 0, i)))
    out_shape.append(jax.ShapeDtypeStruct((B, 2 * H_SSD, L), F32))
    return pl.pallas_call(
        _gzx_kernel, grid=(B, L // tm), in_specs=in_specs, out_specs=out_specs, out_shape=out_shape,
        compiler_params=_cparams(("parallel", "parallel")),
        name="gzx_proj",
    )(x, mod, mod, norm_g, w_bf, w_bf, w_bf, w_dt)


def _attn_kernel(*refs, n_src, lam_init, carry):
    qt_ref = refs[0]
    pos = 1
    if carry:
        qt_next_ref = refs[pos]
        k_next_refs = refs[pos + 1:pos + 1 + n_src]
        pos += 1 + n_src
    srcs = [(refs[pos + 2 * i], refs[pos + 1 + 2 * i]) for i in range(n_src)]
    pos += 2 * n_src
    g_ref, sg_ref, lq1_ref, lk1_ref, lq2_ref, lk2_ref, o_ref = refs[pos:pos + 7]
    if carry:
        s_sc, m_sc = refs[pos + 7:pos + 9]

    tq = ATT_TQ
    n_heads = qt_ref.shape[0] // HD_V
    chains = [(hh, t) for hh in range(n_heads) for t in range(qt_ref.shape[1] // tq)]
    n_ch = len(chains)
    row = lax.broadcasted_iota(jnp.int32, (HD_V, tq), 0)
    zero = jnp.zeros((HD_V, tq), BF16)

    def stacked_q(ref, hh, t):
        qt = ref[hh * HD_V:(hh + 1) * HD_V, t * tq:(t + 1) * tq]
        return jnp.concatenate([jnp.where(row < HD_QK, qt, zero), jnp.where(row >= HD_QK, qt, zero)], axis=1)

    q2t = [stacked_q(qt_ref, hh, t) for hh, t in chains]

    blocks = []
    for si, (k_ref, vt_ref) in enumerate(srcs):
        n_keys = k_ref.shape[0]
        tk = min(ATT_TK, n_keys)
        blocks += [(k_ref, vt_ref, j * tk, tk, si) for j in range(n_keys // tk)]

    lam = (jnp.exp(jnp.sum(lq1_ref[...] * lk1_ref[...], axis=1, keepdims=True))
           - jnp.exp(jnp.sum(lq2_ref[...] * lk2_ref[...], axis=1, keepdims=True)) + lam_init)

    def score_piece(c, j, handover=None):
        k_ref, _, off, tk, si = blocks[j]
        hh = chains[c][0]
        if handover is not None:
            k_ref = k_next_refs[si]
        s = jnp.dot(k_ref[off:off + tk, hh * HD_V:(hh + 1) * HD_V],
                    q2t[c] if handover is None else handover, preferred_element_type=F32)
        return s, jnp.max(s, axis=0, keepdims=True)

    def prob_piece(c, j, s, m, l, acc):
        _, vt_ref, off, tk, _ = blocks[j]
        hh = chains[c][0]
        p = jnp.exp2(s - m)
        l = l + jnp.sum(p, axis=0, keepdims=True)
        acc = acc + jnp.dot(vt_ref[hh * HD_V:(hh + 1) * HD_V, off:off + tk], p.astype(BF16),
                            preferred_element_type=F32)
        return l, acc

    def epilogue(c, l, acc):
        hh, t = chains[c]
        o_n = acc / l
        o = (o_n[:, :tq] - lam * o_n[:, tq:]).T
        ms = jnp.mean(o * o, axis=-1, keepdims=True)
        att = o * lax.rsqrt(ms + EPS) * sg_ref[...] * (1.0 - lam_init)
        gate = _silu(g_ref[t * tq:(t + 1) * tq, hh * HD_V:(hh + 1) * HD_V])
        o_ref[t * tq:(t + 1) * tq, hh * HD_V:(hh + 1) * HD_V] = (att * gate).astype(BF16)

    n_pc = len(blocks)
    neg_inf = jnp.full((1, 2 * tq), -jnp.inf, F32)
    if not carry:
        prev = None
        for c in range(n_ch + 1):
            cur_s, cur_m = [], neg_inf
            if prev is not None:
                l = jnp.zeros((1, 2 * tq), F32)
                acc = jnp.zeros((HD_V, 2 * tq), F32)
            for j in range(n_pc):
                if c < n_ch:
                    s, s_max = score_piece(c, j)
                    cur_s.append(s)
                    cur_m = jnp.maximum(cur_m, s_max)
                if prev is not None:
                    l, acc = prob_piece(prev[0], j, prev[1][j], prev[2], l, acc)
            if prev is not None:
                epilogue(prev[0], l, acc)
            prev = (c, cur_s, cur_m) if c < n_ch else None
        return

    dst = [sum(b[3] for b in blocks[:j]) for j in range(n_pc)]
    step = ((pl.program_id(0) * pl.num_programs(1) + pl.program_id(1)) * pl.num_programs(2)
            + pl.program_id(2))
    cur, nxt = step % 2, (step + 1) % 2

    @pl.when(step == 0)
    def _():
        m0 = neg_inf
        for j in range(n_pc):
            s, s_max = score_piece(0, j)
            s_sc[0, dst[j]:dst[j] + blocks[j][3], :] = s
            m0 = jnp.maximum(m0, s_max)
        m_sc[0] = jnp.broadcast_to(m0, m_sc.shape[1:])

    hh0, t0 = chains[0]
    q_next = stacked_q(qt_next_ref, hh0, t0)
    prev_s, prev_m = None, m_sc[cur][0:1, :]
    for u in range(n_ch):
        l = jnp.zeros((1, 2 * tq), F32)
        acc = jnp.zeros((HD_V, 2 * tq), F32)
        nxt_s, nxt_m = [], neg_inf
        for j in range(n_pc):
            rows = slice(dst[j], dst[j] + blocks[j][3])
            if u + 1 < n_ch:
                s, s_max = score_piece(u + 1, j)
                nxt_s.append(s)
            else:
                s, s_max = score_piece(0, j, q_next)
                s_sc[nxt, rows, :] = s
            nxt_m = jnp.maximum(nxt_m, s_max)
            s_u = s_sc[cur, rows, :] if u == 0 else prev_s[j]
            l, acc = prob_piece(u, j, s_u, prev_m, l, acc)
        if u + 1 == n_ch:
            m_sc[nxt] = jnp.broadcast_to(nxt_m, m_sc.shape[1:])
        epilogue(u, l, acc)
        prev_s, prev_m = nxt_s, nxt_m


def _attention(qt, kvs, g, subln_g, lams, lam_init, tq, heads_per_step):
    B, _, L = qt.shape
    hw = heads_per_step * HD_V
    nq = L // tq
    nh = H_ATT // heads_per_step
    carry = nq > 1
    in_specs = [pl.BlockSpec((None, hw, tq), lambda b, h, i: (b, h, i))]
    args = [qt]
    scratch = []
    if carry:
        def succ(b, h, i):
            end_i, end_h = i == nq - 1, h == nh - 1
            last = end_i & end_h & (b == B - 1)
            b1 = jnp.where(end_i & end_h & ~last, b + 1, b)
            h1 = jnp.where(last, h, jnp.where(end_i, jnp.where(end_h, 0, h + 1), h))
            i1 = jnp.where(last, i, jnp.where(end_i, 0, i + 1))
            return b1, h1, i1

        def q_next(b, h, i):
            b1, h1, i1 = succ(b, h, i)
            return b1, h1, i1

        def k_next(b, h, i):
            b1, h1, _ = succ(b, h, i)
            return b1, 0, h1

        in_specs.append(pl.BlockSpec((None, hw, tq), q_next))
        args.append(qt)
        for k, _ in kvs:
            in_specs.append(pl.BlockSpec((None, k.shape[1], hw), k_next))
            args.append(k)
        n_all = sum(k.shape[1] for k, _ in kvs)
        scratch = [pltpu.VMEM((2, n_all, 2 * ATT_TQ), F32), pltpu.VMEM((2, SUBLANES, 2 * ATT_TQ), F32)]
    for k, vt in kvs:
        n_keys = k.shape[1]
        in_specs += [pl.BlockSpec((None, n_keys, hw), lambda b, h, i: (b, 0, h)),
                     pl.BlockSpec((None, hw, n_keys), lambda b, h, i: (b, h, 0))]
        args += [k, vt]
    in_specs += [pl.BlockSpec((None, tq, hw), lambda b, h, i: (b, i, h)),
                 pl.BlockSpec((1, HD_V), lambda b, h, i: (0, 0))]
    in_specs += [pl.BlockSpec((1, HD_QK), lambda b, h, i: (0, 0))] * 4
    args += [g, subln_g] + list(lams)
    return pl.pallas_call(
        functools.partial(_attn_kernel, n_src=len(kvs), lam_init=lam_init, carry=carry),
        grid=(B, nh, nq),
        in_specs=in_specs,
        out_specs=pl.BlockSpec((None, tq, hw), lambda b, h, i: (b, i, h)),
        out_shape=jax.ShapeDtypeStruct((B, L, D_ATT), BF16),
        scratch_shapes=scratch,
        compiler_params=_cparams(("arbitrary",) * 3 if carry else ("parallel", "parallel", "arbitrary")),
        name="diff_attention",
    )(*args)


def _ssd_kernel(*refs, seq_len, has_h0):
    (x_ref, bc_ref, dtt_ref, cwx_ref, cwbc_ref, cbx_ref, cbbc_ref,
     dtb_ref, alog_ref, dsk_ref) = refs[:10]
    pos = 10
    if has_h0:
        h0_ref = refs[pos]
        pos += 1
    y_ref, hfin_ref, xt_sc, yt_sc, b_sc, ct_sc = refs[pos:pos + 6]

    nc = seq_len // CHUNK
    n_grp = x_ref.shape[1] // GROUP_W
    row_i = lax.broadcasted_iota(jnp.int32, (CHUNK, CHUNK), 0)
    col_i = lax.broadcasted_iota(jnp.int32, (CHUNK, CHUNK), 1)
    row1 = lax.broadcasted_iota(jnp.int32, (CHUNK, 1), 0)

    def conv_silu(src, w_ref, b_ref, c):
        r0 = pl.multiple_of(c * CHUNK, CHUNK)
        rp = pl.multiple_of(jnp.maximum(r0 - SUBLANES, 0), SUBLANES)
        rn = pl.multiple_of(jnp.minimum(r0 + CHUNK, seq_len - SUBLANES), SUBLANES)
        u = src[pl.ds(r0, CHUNK), :]
        pr = jnp.where(c > 0, src[pl.ds(rp, SUBLANES), :][SUBLANES - 1:SUBLANES, :], 0.0)
        nx = jnp.where(c < nc - 1, src[pl.ds(rn, SUBLANES), :][0:1, :], 0.0)
        prev = jnp.where(row1 == 0, pr, pltpu.roll(u, 1, 0))
        nxt = jnp.where(row1 == CHUNK - 1, nx, pltpu.roll(u, CHUNK - 1, 0))
        w = 0.5 * w_ref[...]
        hx = 0.5 * b_ref[...] + prev * w[0:1, :] + u * w[1:2, :] + nxt * w[2:3, :]
        return hx + hx * jnp.tanh(hx)

    def conv_chunk(c, carry):
        xt = conv_silu(x_ref, cwx_ref, cbx_ref, c).T
        xt_sc[c] = xt
        yt_sc[c] = dsk_ref[...] * xt
        bc = conv_silu(bc_ref, cwbc_ref, cbbc_ref, c)
        for g in range(n_grp):
            bc_g = bc[:, g * LANES:(g + 1) * LANES]
            b_sc[c, g] = bc_g[:, :D_STATE].astype(BF16)
            ct_sc[c, g] = bc_g.T[D_STATE:, :].astype(BF16)
        return carry

    lax.fori_loop(0, nc, conv_chunk, 0, unroll=math.gcd(CONV_UNROLL, nc))

    dir_heads = 2 * HEADS_PER_GROUP
    dtb = [dtb_ref[g] for g in range(n_grp)]
    a_neg = [-jnp.exp(alog_ref[g]) * LOG2E for g in range(n_grp)]

    masks = (row_i <= col_i, row_i >= col_i)
    tri = tuple(jnp.where(m, 1.0, 0.0).astype(BF16) for m in masks)
    ends = (CHUNK - 1, 0)

    def scan_prep(items):
        out = []
        for d in range(2):
            sp_d = []
            for g, dd, c in items:
                if dd == d:
                    t0 = pl.multiple_of(c * CHUNK, CHUNK)
                    dt_g = dtt_ref[g * dir_heads:(g + 1) * dir_heads, pl.ds(t0, CHUNK)]
                    sp_d.append(_softplus(dt_g + dtb[g]))
            x = jnp.concatenate([s * a_neg[it[0]] for s, it in
                                 zip(sp_d, [it for it in items if it[1] == d])], axis=0)
            hi = x.astype(BF16)
            r1 = x - hi.astype(F32)
            mid = r1.astype(BF16)
            lo = (r1 - mid.astype(F32)).astype(BF16)
            parts = jnp.dot(jnp.concatenate([hi, mid, lo], axis=0), tri[d], preferred_element_type=F32)
            nr = x.shape[0]
            cum_d = parts[:nr] + parts[nr:2 * nr] + parts[2 * nr:]
            out.append((jnp.concatenate(sp_d, axis=0), cum_d, cum_d.T))
        return tuple(out)

    def scan_items(items, prep, states):
        n = len(items)
        sp, cum, cum_col, cb_t = [None] * n, [None] * n, [None] * n, [None] * n
        for d in range(2):
            sp_d, cum_d, cum_dt = prep[d]
            for i, k in enumerate([k for k, it in enumerate(items) if it[1] == d]):
                rows = slice(i * dir_heads, (i + 1) * dir_heads)
                sp[k], cum[k], cum_col[k] = sp_d[rows], cum_d[rows], cum_dt[:, rows]
        for k, (g, d, c) in enumerate(items):
            cb_t[k] = jnp.dot(b_sc[c, g], ct_sc[c, g], preferred_element_type=F32)
        xdt_b, lm_b, x_dec, y_scale, s_scale = [], [], [], [], []
        for k, (g, d, c) in enumerate(items):
            xt = xt_sc[c, g * GROUP_W:(g + 1) * GROUP_W, :]
            xdt_k, lm_k, x_dec_k, y_scale_k, s_scale_k = [], [], [], [], []
            for hh in range(HEADS_PER_GROUP):
                r = d * HEADS_PER_GROUP + hh
                a_row = cum[k][r:r + 1, :]
                seg = a_row - cum_col[k][:, r:r + 1]
                lmat = jnp.exp2(jnp.where(masks[d], seg, -jnp.inf))
                xdt = xt[hh * SSD_HEADDIM:(hh + 1) * SSD_HEADDIM, :] * sp[k][r:r + 1, :]
                a_end = a_row[:, ends[d]:ends[d] + 1]
                xdt_k.append(xdt.astype(BF16))
                lm_k.append((cb_t[k] * lmat).astype(BF16))
                x_dec_k.append((xdt * jnp.exp2(a_end - a_row)).astype(BF16))
                y_scale_k.append(jnp.broadcast_to(jnp.exp2(a_row), (SSD_HEADDIM, CHUNK)))
                s_scale_k.append(jnp.broadcast_to(jnp.exp2(a_end), (SSD_HEADDIM, D_STATE)))
            xdt_b.append(xdt_k)
            lm_b.append(lm_k)
            x_dec.append(jnp.concatenate(x_dec_k, axis=0))
            y_scale.append(jnp.concatenate(y_scale_k, axis=0))
            s_scale.append(jnp.concatenate(s_scale_k, axis=0))
        y_diag, st = [], []
        for k, (g, d, c) in enumerate(items):
            y_diag.append(jnp.concatenate(
                [jnp.dot(xdt_b[k][hh], lm_b[k][hh], preferred_element_type=F32)
                 for hh in range(HEADS_PER_GROUP)], axis=0))
            st.append(jnp.dot(x_dec[k], b_sc[c, g], preferred_element_type=F32))
        states = list(states)
        for k, (g, d, c) in enumerate(items):
            si = 2 * g + d
            rows = slice(g * GROUP_W, (g + 1) * GROUP_W)
            y_off = jnp.dot(states[si].astype(BF16), ct_sc[c, g], preferred_element_type=F32)
            yt_sc[c, rows, :] = yt_sc[c, rows, :] + y_diag[k] + y_off * y_scale[k]
            states[si] = states[si] * s_scale[k] + st[k]
        return tuple(states)

    if has_h0:
        s_init = tuple(h0_ref[d, g] for g in range(n_grp) for d in range(2))
    else:
        s_init = (jnp.zeros((GROUP_W, D_STATE), F32),) * (2 * n_grp)

    unroll = math.gcd(max(SCAN_UNROLL // n_grp, 1), nc)
    n_it = nc // unroll

    def items_of(i):
        items = []
        for u in range(unroll):
            for g in range(n_grp):
                items += [(g, 0, i * unroll + u), (g, 1, nc - 1 - (i * unroll + u))]
        return items

    def scan_step(i, carry):
        states, prep = carry
        prep_next = scan_prep(items_of(jnp.minimum(i + 1, n_it - 1)))
        return scan_items(items_of(i), prep, states), prep_next

    s_fin, _ = lax.fori_loop(0, n_it, scan_step, (s_init, scan_prep(items_of(0))))
    for g in range(n_grp):
        for d in range(2):
            hfin_ref[d, g] = s_fin[2 * g + d]

    def emit_chunk(c, carry):
        y_ref[pl.ds(pl.multiple_of(c * CHUNK, CHUNK), CHUNK), :] = yt_sc[c].T
        return carry

    lax.fori_loop(0, nc, emit_chunk, 0, unroll=math.gcd(CONV_UNROLL, nc))


def _ssd(xbc, dtt, conv_w, conv_b, dtb_g, alog_g, dsk, h0, n_grp):
    B, L, _ = xbc.shape
    nc = L // CHUNK
    xw, bw = n_grp * GROUP_W, n_grp * LANES
    bc_blk = D_SSD // bw
    dir_heads = 2 * HEADS_PER_GROUP
    state_spec = pl.BlockSpec((None, 2, n_grp, GROUP_W, D_STATE), lambda b, g: (b, 0, g, 0, 0))
    in_specs = [
        pl.BlockSpec((None, L, xw), lambda b, g: (b, 0, g)),
        pl.BlockSpec((None, L, bw), lambda b, g: (b, 0, bc_blk + g)),
        pl.BlockSpec((None, n_grp * dir_heads, L), lambda b, g: (b, g, 0)),
        pl.BlockSpec((3, xw), lambda b, g: (0, g)),
        pl.BlockSpec((3, bw), lambda b, g: (0, bc_blk + g)),
        pl.BlockSpec((1, xw), lambda b, g: (0, g)),
        pl.BlockSpec((1, bw), lambda b, g: (0, bc_blk + g)),
        pl.BlockSpec((n_grp, dir_heads, LANES), lambda b, g: (g, 0, 0)),
        pl.BlockSpec((n_grp, dir_heads, LANES), lambda b, g: (g, 0, 0)),
        pl.BlockSpec((xw, LANES), lambda b, g: (g, 0)),
    ]
    args = [xbc, xbc, dtt, conv_w, conv_w, conv_b, conv_b, dtb_g, alog_g, dsk]
    if h0 is not None:
        in_specs.append(state_spec)
        args.append(h0)
    return pl.pallas_call(
        functools.partial(_ssd_kernel, seq_len=L, has_h0=h0 is not None),
        grid=(B, SSD_GROUPS // n_grp),
        in_specs=in_specs,
        out_specs=[pl.BlockSpec((None, L, xw), lambda b, g: (b, 0, g)), state_spec],
        out_shape=[jax.ShapeDtypeStruct((B, L, D_SSD), F32),
                   jax.ShapeDtypeStruct((B, 2, SSD_GROUPS, GROUP_W, D_STATE), F32)],
        scratch_shapes=[pltpu.VMEM((nc, xw, CHUNK), F32), pltpu.VMEM((nc, xw, CHUNK), F32),
                        pltpu.VMEM((nc, n_grp, CHUNK, D_STATE), BF16),
                        pltpu.VMEM((nc, n_grp, D_STATE, CHUNK), BF16)],
        compiler_params=_cparams(("parallel", "parallel")),
        name="ssd_scan",
    )(*args)


def _out_kernel(x_ref, att_ref, y_ref, shift_ref, scale_ref, ng_ref, wz_ref, sng_ref, w_ref, gate_ref,
                fg_ref, o_ref):
    n_sub = x_ref.shape[0] // SUB_ROWS

    def prologue(r):
        rows = slice(r * SUB_ROWS, (r + 1) * SUB_ROWS)
        hb = _mod_norm(x_ref[rows, :], shift_ref[...], scale_ref[...], ng_ref[...]).astype(BF16)
        z = jnp.dot(hb, wz_ref[...], preferred_element_type=F32)
        yz = y_ref[rows, :] * _silu(z)
        ms = jnp.mean(yz * yz, axis=-1, keepdims=True)
        return (yz * lax.rsqrt(ms + EPS) * sng_ref[...]).astype(BF16)

    ssd_y = prologue(0)
    for r in range(n_sub):
        rows = slice(r * SUB_ROWS, (r + 1) * SUB_ROWS)
        ssd_next = prologue(r + 1) if r + 1 < n_sub else None
        out = jnp.dot(att_ref[rows, :], w_ref[0:D_ATT, :], preferred_element_type=F32)
        out = out + jnp.dot(ssd_y, w_ref[D_ATT:D_ATT + D_SSD, :], preferred_element_type=F32)
        res = x_ref[rows, :] + gate_ref[...] * out
        ms2 = jnp.mean(res * res, axis=-1, keepdims=True)
        o_ref[rows, :] = res * lax.rsqrt(ms2 + EPS) * fg_ref[...]
        ssd_y = ssd_next


def _out_proj(x, att, y, norm_g, w_bf, ssd_norm_g, w_out, mod, mod_row, final_g, tm):
    B, L, _ = x.shape
    row = mod_row
    tok = lambda b, i: (b, i, 0)
    in_specs = [
        pl.BlockSpec((None, tm, D_MODEL), tok),
        pl.BlockSpec((None, tm, D_ATT), tok),
        pl.BlockSpec((None, tm, D_SSD), tok),
        pl.BlockSpec((None, 1, D_MODEL), lambda b, i: (row(b), 0, 0)),
        pl.BlockSpec((None, 1, D_MODEL), lambda b, i: (row(b), 0, 1)),
        pl.BlockSpec((1, D_MODEL), lambda b, i: (0, 0)),
        pl.BlockSpec((D_MODEL, D_SSD), lambda b, i: (0, 4), pipeline_mode=pl.Buffered(1)),
        pl.BlockSpec((1, D_SSD), lambda b, i: (0, 0)),
        pl.BlockSpec((D_ATT + D_SSD, D_MODEL), lambda b, i: (0, 0), pipeline_mode=pl.Buffered(1)),
        pl.BlockSpec((None, 1, D_MODEL), lambda b, i: (row(b), 0, 2)),
        pl.BlockSpec((1, D_MODEL), lambda b, i: (0, 0)),
    ]
    return pl.pallas_call(
        _out_kernel, grid=(B, L // tm), in_specs=in_specs,
        out_specs=pl.BlockSpec((None, tm, D_MODEL), tok),
        out_shape=jax.ShapeDtypeStruct((B, L, D_MODEL), F32),
        compiler_params=_cparams(("parallel", "parallel")),
        name="out_proj",
    )(x, att, y, mod, mod, norm_g, w_bf, ssd_norm_g, w_out, mod, final_g)


def _rope_tables(L):
    rows = L // GRID_W
    row_ids = jnp.repeat(jnp.arange(rows), GRID_W).astype(F32)
    col_ids = jnp.tile(jnp.arange(GRID_W), rows).astype(F32)
    inv = ROPE_BASE ** (-jnp.arange(ROPE_FREQS, dtype=F32) / ROPE_FREQS)
    ang_r = row_ids[:, None] * inv
    ang_c = col_ids[:, None] * inv
    cr, sr, cc, sc = jnp.cos(ang_r), jnp.sin(ang_r), jnp.cos(ang_c), jnp.sin(ang_c)
    cos64 = jnp.concatenate([cr, cr, cc, cc], axis=1)
    sin64 = jnp.concatenate([-sr, sr, -sc, sc], axis=1)
    return jnp.tile(cos64, (1, 2)), jnp.tile(sin64, (1, 2))


def _bc_perm():
    idx = list(range(D_SSD))
    for g in range(SSD_GROUPS):
        idx += [D_SSD + g * D_STATE + n for n in range(D_STATE)]
        idx += [D_SSD + SSD_GROUPS * D_STATE + g * D_STATE + n for n in range(D_STATE)]
    return jnp.asarray(idx, dtype=jnp.int32)


def _dt_perm():
    idx = []
    for g in range(SSD_GROUPS):
        for d in range(2):
            for hh in range(HEADS_PER_GROUP):
                idx.append(d * H_SSD + g * HEADS_PER_GROUP + hh)
    return jnp.asarray(idx, dtype=jnp.int32)


def _group_rows(v2h):
    t = v2h.reshape(2, SSD_GROUPS, HEADS_PER_GROUP).transpose(1, 0, 2).reshape(SSD_GROUPS, 2 * HEADS_PER_GROUP)
    return jnp.broadcast_to(t[:, :, None], (SSD_GROUPS, 2 * HEADS_PER_GROUP, LANES))


def _layer(x, mod, mod_row, params, rope_tabs, past_kv, h0, emit_f32, lam_init, final_g, tm, tq,
           heads_per_step, groups_per_step, tm_out):
    (norm_g, w_bf, w_dt, lams, subln_g, conv_w, conv_b, dtb_g, alog_g, dsk, ssd_norm_g,
     w_out) = params
    proj = _qkv_proj(x, mod, mod_row, norm_g, w_bf, rope_tabs, emit_f32, tm)
    qt, k, vt = proj[:3]
    g, xbc, dtt = _gzx_proj(x, mod, mod_row, norm_g, w_bf, w_dt, tm)
    kvs = [(k, vt)]
    if past_kv is not None:
        kvs.append(past_kv)
    att = _attention(qt, kvs, g, subln_g, lams, lam_init, tq, heads_per_step)
    y, hfin = _ssd(xbc, dtt, conv_w, conv_b, dtb_g, alog_g, dsk, h0, groups_per_step)
    out = _out_proj(x, att, y, norm_g, w_bf, ssd_norm_g, w_out, mod, mod_row, final_g, tm_out)
    return out, proj[3:], hfin


def kernel(x_prompt, x_sample, cache_k, cache_v, state_ssd, c, c_ctx, w_mod, b_mod, norm_g, w_in,
           lambda_q1, lambda_k1, lambda_q2, lambda_k2, subln_g, conv_w, conv_b, dt_bias, A_log,
           D_skip, ssd_norm_g, w_out, final_g):
    b_ctx, l_ctx, _ = x_prompt.shape
    b_dec, l_dec, _ = x_sample.shape
    l_past = cache_k.shape[2]
    depth = w_mod.shape[0]
    assert depth == 1
    lam_init = 0.8 - 0.6 * math.exp(-0.3 * 0)

    w_bf = w_in[0].astype(BF16)
    o2 = 4 * D_ATT + D_SSD
    bc_perm = _bc_perm()
    w_dt = jnp.pad(w_bf[:, o2 + CONV_DIM:][:, _dt_perm()], ((0, 0), (0, DT_PAD - 2 * H_SSD)))
    conv_w_p = conv_w[0][:, bc_perm]
    conv_b_p = conv_b[0][bc_perm].reshape(1, CONV_DIM)
    dtb_g = _group_rows(dt_bias[0])
    alog_g = _group_rows(A_log[0])
    dsk = jnp.broadcast_to(jnp.repeat(D_skip[0], SSD_HEADDIM)[:, None], (D_SSD, LANES))
    lams = [a[0].reshape(1, HD_QK) for a in (lambda_q1, lambda_k1, lambda_q2, lambda_k2)]
    params = (norm_g[0].reshape(1, D_MODEL), w_bf, w_dt, lams, subln_g[0].reshape(1, HD_V),
              conv_w_p, conv_b_p, dtb_g, alog_g, dsk, ssd_norm_g[0].reshape(1, D_SSD),
              w_out[0].astype(BF16))
    fg = final_g.reshape(1, D_MODEL)

    assert b_dec < SUBLANES
    cond = jnp.zeros((SUBLANES, D_MODEL), F32).at[:b_dec].set(c).at[b_dec].set(c_ctx)
    mod = _adaln(cond, w_mod[0], b_mod[0]).reshape(SUBLANES, 1, 3 * D_MODEL)

    y_prompt, (k_ctx, v_ctx), h_ctx = _layer(
        x_prompt, mod, lambda b: b_dec, params, None, None, None, True, lam_init, fg,
        tm=256, tq=256, heads_per_step=H_ATT, groups_per_step=SSD_GROUPS, tm_out=256)

    rope_tabs = _rope_tables(l_dec)
    k_past = cache_k[:, 0].reshape(b_dec, l_past, D_ATT).astype(BF16)
    vt_past = cache_v[:, 0].reshape(b_dec, l_past, D_ATT).transpose(0, 2, 1).astype(BF16)
    h0 = state_ssd[:, 0].reshape(b_dec, 2, SSD_GROUPS, GROUP_W, D_STATE)
    y_sample, _, _ = _layer(
        x_sample, mod, lambda b: b, params, rope_tabs, (k_past, vt_past), h0, False, lam_init, fg,
        tm=512, tq=512, heads_per_step=1, groups_per_step=1, tm_out=1024)

    new_cache_k = k_ctx.reshape(b_ctx, 1, l_ctx, H_ATT, 2 * HD_QK)
    new_cache_v = v_ctx.reshape(b_ctx, 1, l_ctx, H_ATT, HD_V)
    new_state = h_ctx.reshape(b_ctx, 1, 2, H_SSD, SSD_HEADDIM, D_STATE)
    return (y_prompt, y_sample, new_cache_k, new_cache_v, new_state)
```
